```python
import jax, jax.numpy as jnp
from jax import lax
import numpy as np

D_MODEL = 2048
BATCH = 8
SEQ = 4096
DEPTH = 4

HEAD_DIM = 128
BLOCK = 128
EPS = 1e-6
ROPE_THETA = 500000.0
PARTIAL_ROPE = HEAD_DIM // 4
FOX_HEADS = (D_MODEL // 4) // HEAD_DIM
FOX_W = FOX_HEADS * HEAD_DIM
FORGET_BIAS_CENTER = 2.0
MLA_HEADS = (D_MODEL // 2) // HEAD_DIM
MLA_Q_RANK = D_MODEL // 4
MLA_KV_RANK = D_MODEL // 4
MLA_NOPE = 128
MLA_ROPE = 64
MLA_V = HEAD_DIM
MLA_W = MLA_HEADS * MLA_V
DIL_HEADS = (D_MODEL // 4) // HEAD_DIM
DIL_W = DIL_HEADS * HEAD_DIM
DIL_BRANCHES = ((128, 1), (512, 4), (2048, 16))
MIX_W = FOX_W + MLA_W + DIL_W
IN_SIZES = (FOX_W, FOX_W, FOX_W, FOX_HEADS,
            MLA_Q_RANK, MLA_KV_RANK, MLA_ROPE,
            DIL_W, DIL_W, DIL_W)
IN_W = sum(IN_SIZES)
D_FF = 5632

kernel_name = "hybrid_fox_mla_dilated_macaron"


def rms_norm(x, g):
    xf = x.astype(jnp.float32)
    y = xf * lax.rsqrt(jnp.mean(xf * xf, axis=-1, keepdims=True) + EPS)
    return (y * g.astype(jnp.float32)).astype(x.dtype)


def swiglu(h, w_gate, w_up, w_down):
    return (jax.nn.silu(h @ w_gate) * (h @ w_up)) @ w_down


def rope_tables(seq, dim):
    inv = 1.0 / (ROPE_THETA ** (jnp.arange(0, dim, 2, dtype=jnp.float32) / dim))
    ang = jnp.arange(seq, dtype=jnp.float32)[:, None] * inv[None, :]
    return jnp.cos(ang), jnp.sin(ang)


def apply_rope(x, cos, sin):
    x1, x2 = jnp.split(x, 2, axis=-1)
    c, s = cos.astype(x.dtype), sin.astype(x.dtype)
    return jnp.concatenate([x1 * c - x2 * s, x1 * s + x2 * c], axis=-1)


def partial_rope(x, cos, sin):
    return jnp.concatenate([apply_rope(x[..., :PARTIAL_ROPE], cos, sin), x[..., PARTIAL_ROPE:]], axis=-1)


def to_heads(t, n_heads):
    b, s, _ = t.shape
    return t.reshape(b, s, n_heads, -1).transpose(0, 2, 1, 3)


def merge_heads(t):
    b, h, s, d = t.shape
    return t.transpose(0, 2, 1, 3).reshape(b, s, h * d)


def causal_block_attention(q, k, v, scale, cum_log_f=None):
    b, h, s_len, _ = q.shape
    nb = s_len // BLOCK
    kpos = jnp.arange(s_len)
    xs = [jnp.arange(nb), q.reshape(b, h, nb, BLOCK, -1).transpose(2, 0, 1, 3, 4)]
    if cum_log_f is not None:
        xs.append(cum_log_f.reshape(b, h, nb, BLOCK).transpose(2, 0, 1, 3))

    def attend(blk):
        i, qi = blk[0], blk[1]
        sc = jnp.einsum("bhqd,bhkd->bhqk", qi, k, preferred_element_type=jnp.float32) * scale
        if cum_log_f is not None:
            sc = sc + (blk[2][..., :, None] - cum_log_f[..., None, :])
        qpos = i * BLOCK + jnp.arange(BLOCK)
        sc = jnp.where(kpos[None, :] <= qpos[:, None], sc, -jnp.inf)
        p = jax.nn.softmax(sc, axis=-1)
        return jnp.einsum("bhqk,bhkd->bhqd", p.astype(v.dtype), v)

    out = lax.map(attend, tuple(xs))
    return out.transpose(1, 2, 0, 3, 4).reshape(b, h, s_len, -1)


def dilated_branch(q, k, v, window, dilation):
    b, h, s_len, hd = q.shape
    L = s_len // dilation
    n_back = window // dilation
    Lp = -(-L // BLOCK) * BLOCK
    nb = Lp // BLOCK

    def to_blocks(t):
        t = t.reshape(b, h, L, dilation, hd).transpose(0, 1, 3, 2, 4)
        t = jnp.pad(t, ((0, 0), (0, 0), (0, 0), (0, Lp - L), (0, 0)))
        return t.reshape(b, h, dilation, nb, BLOCK, hd)

    def with_prev(t):
        prev = jnp.pad(t, ((0, 0), (0, 0), (0, 0), (1, 0), (0, 0), (0, 0)))[:, :, :, :-1]
        return jnp.concatenate([prev, t], axis=4)

    qb = to_blocks(q)
    kc = with_prev(to_blocks(k))
    vc = with_prev(to_blocks(v))
    sc = jnp.einsum("bhrnqd,bhrnkd->bhrnqk", qb, kc, preferred_element_type=jnp.float32) * (hd ** -0.5)
    kidx = jnp.arange(2 * BLOCK)
    dist = (BLOCK + jnp.arange(BLOCK))[:, None] - kidx[None, :]
    band = (dist >= 0) & (dist <= n_back)
    has_prev = (jnp.arange(nb)[:, None, None] > 0) | (kidx[None, None, :] >= BLOCK)
    sc = jnp.where(band[None] & has_prev, sc, -jnp.inf)
    m = jnp.max(sc, axis=-1, keepdims=True)
    e = jnp.exp(sc - m)
    l = jnp.sum(e, axis=-1, keepdims=True)
    o = jnp.einsum("bhrnqk,bhrnkd->bhrnqd", (e / l).astype(v.dtype), vc)
    lse = (m + jnp.log(l))[..., 0]
    o = o.reshape(b, h, dilation, Lp, hd)[:, :, :, :L].transpose(0, 1, 3, 2, 4).reshape(b, h, s_len, hd)
    lse = lse.reshape(b, h, dilation, Lp)[..., :L].transpose(0, 1, 3, 2).reshape(b, h, s_len)
    return o, lse


def dilated_mixture(q, k, v):
    outs, lses = [], []
    for window, dilation in DIL_BRANCHES:
        o, lse = dilated_branch(q, k, v, window, dilation)
        outs.append(o)
        lses.append(lse)
    wts = jax.nn.softmax(jnp.stack(lses, axis=0), axis=0)
    return jnp.sum(wts[..., None].astype(q.dtype) * jnp.stack(outs, axis=0), axis=0)


def _fwd_setup_inputs(seed: int = 0) -> dict:
    key = jax.random.key(seed)
    ks = iter(jax.random.split(key, 32))

    def dense(shape, fan_in):
        return jax.random.normal(next(ks), shape, jnp.float32) * (fan_in ** -0.5)

    def gain(shape):
        return 1.0 + 0.02 * jax.random.normal(next(ks), shape, jnp.float32)

    x = jax.random.normal(next(ks), (BATCH, SEQ, D_MODEL), jnp.float32)
    return {
        "x": x,
        "ffn1_norm": gain((DEPTH, D_MODEL)),
        "ffn1_w_gate": dense((DEPTH, D_MODEL, D_FF), D_MODEL),
        "ffn1_w_up": dense((DEPTH, D_MODEL, D_FF), D_MODEL),
        "ffn1_w_down": dense((DEPTH, D_FF, D_MODEL), D_FF),
        "mix_norm": gain((DEPTH, D_MODEL)),
        "w_in": dense((DEPTH, D_MODEL, IN_W), D_MODEL),
        "fox_forget_bias": FORGET_BIAS_CENTER + 0.5 * jax.random.normal(next(ks), (DEPTH, FOX_HEADS), jnp.float32),
        "mla_q_norm": gain((DEPTH, MLA_Q_RANK)),
        "mla_kv_norm": gain((DEPTH, MLA_KV_RANK)),
        "mla_w_uq": dense((DEPTH, MLA_Q_RANK, MLA_HEADS * (MLA_NOPE + MLA_ROPE)), MLA_Q_RANK),
        "mla_w_ukv": dense((DEPTH, MLA_KV_RANK, MLA_HEADS * (MLA_NOPE + MLA_V)), MLA_KV_RANK),
        "w_out": dense((DEPTH, MIX_W, D_MODEL), MIX_W),
        "ffn2_norm": gain((DEPTH, D_MODEL)),
        "ffn2_w_gate": dense((DEPTH, D_MODEL, D_FF), D_MODEL),
        "ffn2_w_up": dense((DEPTH, D_MODEL, D_FF), D_MODEL),
        "ffn2_w_down": dense((DEPTH, D_FF, D_MODEL), D_FF),
        "final_norm": gain((D_MODEL,)),
    }


def _fwd_reference(x, ffn1_norm, ffn1_w_gate, ffn1_w_up, ffn1_w_down, mix_norm, w_in, fox_forget_bias,
              mla_q_norm, mla_kv_norm, mla_w_uq, mla_w_ukv, w_out, ffn2_norm, ffn2_w_gate, ffn2_w_up,
              ffn2_w_down, final_norm):
    b, s_len, _ = x.shape
    cos_p, sin_p = rope_tables(s_len, PARTIAL_ROPE)
    cos_m, sin_m = rope_tables(s_len, MLA_ROPE)
    offsets = []
    acc = 0
    for size in IN_SIZES[:-1]:
        acc += size
        offsets.append(acc)

    for l in range(DEPTH):
        x = x + 0.5 * swiglu(rms_norm(x, ffn1_norm[l]), ffn1_w_gate[l], ffn1_w_up[l], ffn1_w_down[l])

        h = rms_norm(x, mix_norm[l])
        proj = h @ w_in[l]
        fq, fk, fv, f_logit, c_q, c_kv, k_r, dq, dk, dv = jnp.split(proj, offsets, axis=-1)

        log_f = jax.nn.log_sigmoid((f_logit + fox_forget_bias[l]).astype(jnp.float32))
        cum = jnp.cumsum(log_f, axis=1).transpose(0, 2, 1)
        out_a = causal_block_attention(to_heads(fq, FOX_HEADS), to_heads(fk, FOX_HEADS),
                                       to_heads(fv, FOX_HEADS), HEAD_DIM ** -0.5, cum)

        q_b = to_heads(rms_norm(c_q, mla_q_norm[l]) @ mla_w_uq[l], MLA_HEADS)
        q_b = jnp.concatenate([q_b[..., :MLA_NOPE], apply_rope(q_b[..., MLA_NOPE:], cos_m, sin_m)], axis=-1)
        kv_b = to_heads(rms_norm(c_kv, mla_kv_norm[l]) @ mla_w_ukv[l], MLA_HEADS)
        k_rope = apply_rope(k_r[:, None], cos_m, sin_m)
        k_b = jnp.concatenate([kv_b[..., :MLA_NOPE],
                               jnp.broadcast_to(k_rope, (b, MLA_HEADS, s_len, MLA_ROPE))], axis=-1)
        out_b = causal_block_attention(q_b, k_b, kv_b[..., MLA_NOPE:], (MLA_NOPE + MLA_ROPE) ** -0.5)

        out_c = dilated_mixture(partial_rope(to_heads(dq, DIL_HEADS), cos_p, sin_p),
                                partial_rope(to_heads(dk, DIL_HEADS), cos_p, sin_p),
                                to_heads(dv, DIL_HEADS))

        mixed = jnp.concatenate([merge_heads(out_a), merge_heads(out_b), merge_heads(out_c)], axis=-1)
        x = x + mixed @ w_out[l]

        x = x + 0.5 * swiglu(rms_norm(x, ffn2_norm[l]), ffn2_w_gate[l], ffn2_w_up[l], ffn2_w_down[l])

    return rms_norm(x, final_norm)


import jax as _jax
import jax.numpy as _jnp

TWIN_FORMAT = 'train_step'
FWD_PARAMS = ['x', 'ffn1_norm', 'ffn1_w_gate', 'ffn1_w_up', 'ffn1_w_down', 'mix_norm', 'w_in', 'fox_forget_bias', 'mla_q_norm', 'mla_kv_norm', 'mla_w_uq', 'mla_w_ukv', 'w_out', 'ffn2_norm', 'ffn2_w_gate', 'ffn2_w_up', 'ffn2_w_down', 'final_norm']
TWIN_WEIGHTS = ['ffn1_norm', 'ffn1_w_gate', 'ffn1_w_up', 'ffn1_w_down', 'mix_norm', 'w_in', 'fox_forget_bias', 'mla_q_norm', 'mla_kv_norm', 'mla_w_uq', 'mla_w_ukv', 'w_out', 'ffn2_norm', 'ffn2_w_gate', 'ffn2_w_up', 'ffn2_w_down', 'final_norm']
TWIN_DIFF_INPUT = 'x'
TWIN_INPUTS = ['x', 'ffn1_norm', 'ffn1_w_gate', 'ffn1_w_up', 'ffn1_w_down', 'mix_norm', 'w_in', 'fox_forget_bias', 'mla_q_norm', 'mla_kv_norm', 'mla_w_uq', 'mla_w_ukv', 'w_out', 'ffn2_norm', 'ffn2_w_gate', 'ffn2_w_up', 'ffn2_w_down', 'final_norm', 'loss_target', 'm_ffn1_norm', 'm_ffn1_w_gate', 'm_ffn1_w_up', 'm_ffn1_w_down', 'm_mix_norm', 'm_w_in', 'm_fox_forget_bias', 'm_mla_q_norm', 'm_mla_kv_norm', 'm_mla_w_uq', 'm_mla_w_ukv', 'm_w_out', 'm_ffn2_norm', 'm_ffn2_w_gate', 'm_ffn2_w_up', 'm_ffn2_w_down', 'm_final_norm', 'v_ffn1_norm', 'v_ffn1_w_gate', 'v_ffn1_w_up', 'v_ffn1_w_down', 'v_mix_norm', 'v_w_in', 'v_fox_forget_bias', 'v_mla_q_norm', 'v_mla_kv_norm', 'v_mla_w_uq', 'v_mla_w_ukv', 'v_w_out', 'v_ffn2_norm', 'v_ffn2_w_gate', 'v_ffn2_w_up', 'v_ffn2_w_down', 'v_final_norm']
TWIN_OUTPUTS = ['loss', 'grad_x', 'grad_ffn1_norm', 'grad_ffn1_w_gate', 'grad_ffn1_w_up', 'grad_ffn1_w_down', 'grad_mix_norm', 'grad_w_in', 'grad_fox_forget_bias', 'grad_mla_q_norm', 'grad_mla_kv_norm', 'grad_mla_w_uq', 'grad_mla_w_ukv', 'grad_w_out', 'grad_ffn2_norm', 'grad_ffn2_w_gate', 'grad_ffn2_w_up', 'grad_ffn2_w_down', 'grad_final_norm', 'delta_ffn1_norm', 'delta_ffn1_w_gate', 'delta_ffn1_w_up', 'delta_ffn1_w_down', 'delta_mix_norm', 'delta_w_in', 'delta_fox_forget_bias', 'delta_mla_q_norm', 'delta_mla_kv_norm', 'delta_mla_w_uq', 'delta_mla_w_ukv', 'delta_w_out', 'delta_ffn2_norm', 'delta_ffn2_w_gate', 'delta_ffn2_w_up', 'delta_ffn2_w_down', 'delta_final_norm', 'new_m_ffn1_norm', 'new_m_ffn1_w_gate', 'new_m_ffn1_w_up', 'new_m_ffn1_w_down', 'new_m_mix_norm', 'new_m_w_in', 'new_m_fox_forget_bias', 'new_m_mla_q_norm', 'new_m_mla_kv_norm', 'new_m_mla_w_uq', 'new_m_mla_w_ukv', 'new_m_w_out', 'new_m_ffn2_norm', 'new_m_ffn2_w_gate', 'new_m_ffn2_w_up', 'new_m_ffn2_w_down', 'new_m_final_norm', 'new_v_ffn1_norm', 'new_v_ffn1_w_gate', 'new_v_ffn1_w_up', 'new_v_ffn1_w_down', 'new_v_mix_norm', 'new_v_w_in', 'new_v_fox_forget_bias', 'new_v_mla_q_norm', 'new_v_mla_kv_norm', 'new_v_mla_w_uq', 'new_v_mla_w_ukv', 'new_v_w_out', 'new_v_ffn2_norm', 'new_v_ffn2_w_gate', 'new_v_ffn2_w_up', 'new_v_ffn2_w_down', 'new_v_final_norm']
TWIN_LEAF_KINDS = {'loss': 'loss', 'grad_x': 'grad_x', 'grad_ffn1_norm': 'grad_w', 'grad_ffn1_w_gate': 'grad_w', 'grad_ffn1_w_up': 'grad_w', 'grad_ffn1_w_down': 'grad_w', 'grad_mix_norm': 'grad_w', 'grad_w_in': 'grad_w', 'grad_fox_forget_bias': 'grad_w', 'grad_mla_q_norm': 'grad_w', 'grad_mla_kv_norm': 'grad_w', 'grad_mla_w_uq': 'grad_w', 'grad_mla_w_ukv': 'grad_w', 'grad_w_out': 'grad_w', 'grad_ffn2_norm': 'grad_w', 'grad_ffn2_w_gate': 'grad_w', 'grad_ffn2_w_up': 'grad_w', 'grad_ffn2_w_down': 'grad_w', 'grad_final_norm': 'grad_w', 'delta_ffn1_norm': 'delta_w', 'delta_ffn1_w_gate': 'delta_w', 'delta_ffn1_w_up': 'delta_w', 'delta_ffn1_w_down': 'delta_w', 'delta_mix_norm': 'delta_w', 'delta_w_in': 'delta_w', 'delta_fox_forget_bias': 'delta_w', 'delta_mla_q_norm': 'delta_w', 'delta_mla_kv_norm': 'delta_w', 'delta_mla_w_uq': 'delta_w', 'delta_mla_w_ukv': 'delta_w', 'delta_w_out': 'delta_w', 'delta_ffn2_norm': 'delta_w', 'delta_ffn2_w_gate': 'delta_w', 'delta_ffn2_w_up': 'delta_w', 'delta_ffn2_w_down': 'delta_w', 'delta_final_norm': 'delta_w', 'new_m_ffn1_norm': 'new_m', 'new_m_ffn1_w_gate': 'new_m', 'new_m_ffn1_w_up': 'new_m', 'new_m_ffn1_w_down': 'new_m', 'new_m_mix_norm': 'new_m', 'new_m_w_in': 'new_m', 'new_m_fox_forget_bias': 'new_m', 'new_m_mla_q_norm': 'new_m', 'new_m_mla_kv_norm': 'new_m', 'new_m_mla_w_uq': 'new_m', 'new_m_mla_w_ukv': 'new_m', 'new_m_w_out': 'new_m', 'new_m_ffn2_norm': 'new_m', 'new_m_ffn2_w_gate': 'new_m', 'new_m_ffn2_w_up': 'new_m', 'new_m_ffn2_w_down': 'new_m', 'new_m_final_norm': 'new_m', 'new_v_ffn1_norm': 'new_v', 'new_v_ffn1_w_gate': 'new_v', 'new_v_ffn1_w_up': 'new_v', 'new_v_ffn1_w_down': 'new_v', 'new_v_mix_norm': 'new_v', 'new_v_w_in': 'new_v', 'new_v_fox_forget_bias': 'new_v', 'new_v_mla_q_norm': 'new_v', 'new_v_mla_kv_norm': 'new_v', 'new_v_mla_w_uq': 'new_v', 'new_v_mla_w_ukv': 'new_v', 'new_v_w_out': 'new_v', 'new_v_ffn2_norm': 'new_v', 'new_v_ffn2_w_gate': 'new_v', 'new_v_ffn2_w_up': 'new_v', 'new_v_ffn2_w_down': 'new_v', 'new_v_final_norm': 'new_v'}


def _forward(args):
    return _fwd_reference(*[args[k] for k in FWD_PARAMS])


def _output_shape():
    out = _jax.eval_shape(lambda: _forward(_fwd_setup_inputs(0)))
    return out.shape, out.dtype

N_MICROBATCH = 1
ADAM_LR = 0.001
ADAM_B1 = 0.9
ADAM_B2 = 0.999
ADAM_EPS = 1e-08
ADAM_WD = 0.01
ADAM_STEP = 10
PER_EXAMPLE_BATCH_AXIS = {'x': 0, 'loss_target': 0}
SHARED_INPUTS = []
_WEIGHT_DTYPES = {'ffn1_norm': _jnp.float32, 'ffn1_w_gate': _jnp.float32, 'ffn1_w_up': _jnp.float32, 'ffn1_w_down': _jnp.float32, 'mix_norm': _jnp.float32, 'w_in': _jnp.float32, 'fox_forget_bias': _jnp.float32, 'mla_q_norm': _jnp.float32, 'mla_kv_norm': _jnp.float32, 'mla_w_uq': _jnp.float32, 'mla_w_ukv': _jnp.float32, 'w_out': _jnp.float32, 'ffn2_norm': _jnp.float32, 'ffn2_w_gate': _jnp.float32, 'ffn2_w_up': _jnp.float32, 'ffn2_w_down': _jnp.float32, 'final_norm': _jnp.float32}
MOMENT_SCALE = {'ffn1_norm': 4.047317e-02, 'ffn1_w_gate': 1.682735e-02, 'ffn1_w_up': 1.627749e-02, 'ffn1_w_down': 2.698951e-02, 'mix_norm': 3.977902e-02, 'w_in': 2.776149e-02, 'fox_forget_bias': 2.167151e-01, 'mla_q_norm': 2.035620e-02, 'mla_kv_norm': 3.897100e-02, 'mla_w_uq': 1.150043e-02, 'mla_w_ukv': 1.972984e-02, 'w_out': 2.997479e-02, 'ffn2_norm': 3.583780e-02, 'ffn2_w_gate': 1.537863e-02, 'ffn2_w_up': 1.487479e-02, 'ffn2_w_down': 2.468440e-02, 'final_norm': 1.597750e+01}


def _to_microbatches(a, axis):
    t = _jnp.moveaxis(a, axis, 0)
    t = t.reshape((N_MICROBATCH, t.shape[0] // N_MICROBATCH) + t.shape[1:])
    return _jnp.moveaxis(t, 1, axis + 1)


def setup_inputs(seed: int = 0) -> dict:
    inp = _fwd_setup_inputs(seed)
    key = _jax.random.fold_in(_jax.random.key(seed), 7919)
    shape, _ = _output_shape()
    out = dict(inp)
    out["loss_target"] = _jax.random.normal(_jax.random.fold_in(key, 0), shape, _jnp.float32)
    for i, name in enumerate(TWIN_WEIGHTS):
        w = inp[name].astype(_jnp.float32)
        if MOMENT_SCALE is None:
            s = _jnp.sqrt(_jnp.mean(_jnp.square(w)) + 1e-30)
        else:
            s = MOMENT_SCALE[name]
        km, kv = _jax.random.split(_jax.random.fold_in(key, i + 1))
        out[name] = w
        out["m_" + name] = s * _jax.random.normal(km, w.shape, _jnp.float32)
        out["v_" + name] = (s * s) * _jax.random.uniform(kv, w.shape, _jnp.float32, 0.5, 1.5)
    if N_MICROBATCH > 1:
        for name, axis in PER_EXAMPLE_BATCH_AXIS.items():
            out[name] = _to_microbatches(out[name], axis)
    return {'x': out['x'], 'ffn1_norm': out['ffn1_norm'], 'ffn1_w_gate': out['ffn1_w_gate'], 'ffn1_w_up': out['ffn1_w_up'], 'ffn1_w_down': out['ffn1_w_down'], 'mix_norm': out['mix_norm'], 'w_in': out['w_in'], 'fox_forget_bias': out['fox_forget_bias'], 'mla_q_norm': out['mla_q_norm'], 'mla_kv_norm': out['mla_kv_norm'], 'mla_w_uq': out['mla_w_uq'], 'mla_w_ukv': out['mla_w_ukv'], 'w_out': out['w_out'], 'ffn2_norm': out['ffn2_norm'], 'ffn2_w_gate': out['ffn2_w_gate'], 'ffn2_w_up': out['ffn2_w_up'], 'ffn2_w_down': out['ffn2_w_down'], 'final_norm': out['final_norm'], 'loss_target': out['loss_target'], 'm_ffn1_norm': out['m_ffn1_norm'], 'm_ffn1_w_gate': out['m_ffn1_w_gate'], 'm_ffn1_w_up': out['m_ffn1_w_up'], 'm_ffn1_w_down': out['m_ffn1_w_down'], 'm_mix_norm': out['m_mix_norm'], 'm_w_in': out['m_w_in'], 'm_fox_forget_bias': out['m_fox_forget_bias'], 'm_mla_q_norm': out['m_mla_q_norm'], 'm_mla_kv_norm': out['m_mla_kv_norm'], 'm_mla_w_uq': out['m_mla_w_uq'], 'm_mla_w_ukv': out['m_mla_w_ukv'], 'm_w_out': out['m_w_out'], 'm_ffn2_norm': out['m_ffn2_norm'], 'm_ffn2_w_gate': out['m_ffn2_w_gate'], 'm_ffn2_w_up': out['m_ffn2_w_up'], 'm_ffn2_w_down': out['m_ffn2_w_down'], 'm_final_norm': out['m_final_norm'], 'v_ffn1_norm': out['v_ffn1_norm'], 'v_ffn1_w_gate': out['v_ffn1_w_gate'], 'v_ffn1_w_up': out['v_ffn1_w_up'], 'v_ffn1_w_down': out['v_ffn1_w_down'], 'v_mix_norm': out['v_mix_norm'], 'v_w_in': out['v_w_in'], 'v_fox_forget_bias': out['v_fox_forget_bias'], 'v_mla_q_norm': out['v_mla_q_norm'], 'v_mla_kv_norm': out['v_mla_kv_norm'], 'v_mla_w_uq': out['v_mla_w_uq'], 'v_mla_w_ukv': out['v_mla_w_ukv'], 'v_w_out': out['v_w_out'], 'v_ffn2_norm': out['v_ffn2_norm'], 'v_ffn2_w_gate': out['v_ffn2_w_gate'], 'v_ffn2_w_up': out['v_ffn2_w_up'], 'v_ffn2_w_down': out['v_ffn2_w_down'], 'v_final_norm': out['v_final_norm']}


def _loss(weights, diff, rest, loss_target):
    with _jax.named_scope("forward"):
        args = {**rest, TWIN_DIFF_INPUT: diff, **{k: w.astype(_WEIGHT_DTYPES[k]) for k, w in weights.items()}}
        y = _forward(args)
    with _jax.named_scope("loss_head"):
        err = _jnp.square(y.astype(_jnp.float32) - loss_target)
        return 0.5 * _jnp.sum(_jnp.mean(err, axis=-1)) if err.ndim else 0.5 * err


def _adamw(w, g, m, v):
    m = ADAM_B1 * m + (1.0 - ADAM_B1) * g
    v = ADAM_B2 * v + (1.0 - ADAM_B2) * _jnp.square(g)
    m_hat = m / (1.0 - ADAM_B1 ** ADAM_STEP)
    v_hat = v / (1.0 - ADAM_B2 ** ADAM_STEP)
    delta = -ADAM_LR * (m_hat / (_jnp.sqrt(v_hat) + ADAM_EPS) + ADAM_WD * w)
    return delta, m, v


def reference(x, ffn1_norm, ffn1_w_gate, ffn1_w_up, ffn1_w_down, mix_norm, w_in, fox_forget_bias, mla_q_norm, mla_kv_norm, mla_w_uq, mla_w_ukv, w_out, ffn2_norm, ffn2_w_gate, ffn2_w_up, ffn2_w_down, final_norm, loss_target, m_ffn1_norm, m_ffn1_w_gate, m_ffn1_w_up, m_ffn1_w_down, m_mix_norm, m_w_in, m_fox_forget_bias, m_mla_q_norm, m_mla_kv_norm, m_mla_w_uq, m_mla_w_ukv, m_w_out, m_ffn2_norm, m_ffn2_w_gate, m_ffn2_w_up, m_ffn2_w_down, m_final_norm, v_ffn1_norm, v_ffn1_w_gate, v_ffn1_w_up, v_ffn1_w_down, v_mix_norm, v_w_in, v_fox_forget_bias, v_mla_q_norm, v_mla_kv_norm, v_mla_w_uq, v_mla_w_ukv, v_w_out, v_ffn2_norm, v_ffn2_w_gate, v_ffn2_w_up, v_ffn2_w_down, v_final_norm):
    given = dict(x=x, ffn1_norm=ffn1_norm, ffn1_w_gate=ffn1_w_gate, ffn1_w_up=ffn1_w_up, ffn1_w_down=ffn1_w_down, mix_norm=mix_norm, w_in=w_in, fox_forget_bias=fox_forget_bias, mla_q_norm=mla_q_norm, mla_kv_norm=mla_kv_norm, mla_w_uq=mla_w_uq, mla_w_ukv=mla_w_ukv, w_out=w_out, ffn2_norm=ffn2_norm, ffn2_w_gate=ffn2_w_gate, ffn2_w_up=ffn2_w_up, ffn2_w_down=ffn2_w_down, final_norm=final_norm, loss_target=loss_target, m_ffn1_norm=m_ffn1_norm, m_ffn1_w_gate=m_ffn1_w_gate, m_ffn1_w_up=m_ffn1_w_up, m_ffn1_w_down=m_ffn1_w_down, m_mix_norm=m_mix_norm, m_w_in=m_w_in, m_fox_forget_bias=m_fox_forget_bias, m_mla_q_norm=m_mla_q_norm, m_mla_kv_norm=m_mla_kv_norm, m_mla_w_uq=m_mla_w_uq, m_mla_w_ukv=m_mla_w_ukv, m_w_out=m_w_out, m_ffn2_norm=m_ffn2_norm, m_ffn2_w_gate=m_ffn2_w_gate, m_ffn2_w_up=m_ffn2_w_up, m_ffn2_w_down=m_ffn2_w_down, m_final_norm=m_final_norm, v_ffn1_norm=v_ffn1_norm, v_ffn1_w_gate=v_ffn1_w_gate, v_ffn1_w_up=v_ffn1_w_up, v_ffn1_w_down=v_ffn1_w_down, v_mix_norm=v_mix_norm, v_w_in=v_w_in, v_fox_forget_bias=v_fox_forget_bias, v_mla_q_norm=v_mla_q_norm, v_mla_kv_norm=v_mla_kv_norm, v_mla_w_uq=v_mla_w_uq, v_mla_w_ukv=v_mla_w_ukv, v_w_out=v_w_out, v_ffn2_norm=v_ffn2_norm, v_ffn2_w_gate=v_ffn2_w_gate, v_ffn2_w_up=v_ffn2_w_up, v_ffn2_w_down=v_ffn2_w_down, v_final_norm=v_final_norm)
    weights = {n: given[n] for n in TWIN_WEIGHTS}
    shared = {n: given[n] for n in SHARED_INPUTS}
    per_example = {n: given[n] for n in ['x']}
    grad_fn = _jax.value_and_grad(_loss, argnums=(0, 1))

    def one_microbatch(ex, loss_target):
        ex = dict(ex)
        diff = ex.pop(TWIN_DIFF_INPUT)
        return grad_fn(weights, diff, {**shared, **ex}, loss_target)

    if N_MICROBATCH == 1:
        loss, (grad_w, grad_x) = one_microbatch(per_example, given["loss_target"])
    else:
        def body(carry, xs):
            loss_sum, grad_sum = carry
            l_k, (gw_k, gx_k) = one_microbatch(xs[0], xs[1])
            with _jax.named_scope("update"):
                return (loss_sum + l_k, _jax.tree.map(_jnp.add, grad_sum, gw_k)), gx_k

        init = (_jnp.zeros((), _jnp.float32), _jax.tree.map(_jnp.zeros_like, weights))
        (loss, grad_w), grad_x = _jax.lax.scan(body, init, (per_example, given["loss_target"]))
    with _jax.named_scope("update"):
        delta_w, new_m, new_v = {}, {}, {}
        for n in TWIN_WEIGHTS:
            delta_w[n], new_m[n], new_v[n] = _adamw(weights[n], grad_w[n], given["m_" + n], given["v_" + n])
    return (loss, grad_x, *[grad_w[n] for n in TWIN_WEIGHTS], *[delta_w[n] for n in TWIN_WEIGHTS],
            *[new_m[n] for n in TWIN_WEIGHTS], *[new_v[n] for n in TWIN_WEIGHTS])
```

```python
import functools

import jax
import jax.numpy as jnp
from jax import lax
from jax.experimental import pallas as pl
from jax.experimental.pallas import tpu as pltpu

F32 = jnp.float32
BF16 = jnp.bfloat16
MESH = pl.DeviceIdType.MESH
ANY = pl.BlockSpec(memory_space=pl.ANY)

LANES = 128
VMEM_LIMIT_BYTES = 48 * 2 ** 20
HEAD = 128
EPS = 1e-6
NEG = -1e30
ATTN_TILE = 512
ROPE_THETA = 500000.0
MLA_ROPE = 64
PARTIAL_ROPE = HEAD // 4
DIL_BRANCHES = ((128, 1), (512, 4), (2048, 16))
DIL_REACH = max(w for w, _ in DIL_BRANCHES)
FORGET_LANE = MLA_ROPE
ADAM_LR, ADAM_B1, ADAM_B2, ADAM_EPS, ADAM_WD, ADAM_STEP = 0.001, 0.9, 0.999, 1e-08, 0.01, 10

BIG = ("ffn1_w_gate", "ffn1_w_up", "ffn1_w_down", "w_in", "mla_w_uq", "mla_w_ukv", "w_out",
       "ffn2_w_gate", "ffn2_w_up", "ffn2_w_down")
ROW_SHARDED = ("ffn1_w_down", "w_out", "ffn2_w_down")
SMALL = ("ffn1_norm", "mix_norm", "fox_forget_bias", "mla_q_norm", "mla_kv_norm", "ffn2_norm", "final_norm")
WEIGHTS = ("ffn1_norm", "ffn1_w_gate", "ffn1_w_up", "ffn1_w_down", "mix_norm", "w_in", "fox_forget_bias",
           "mla_q_norm", "mla_kv_norm", "mla_w_uq", "mla_w_ukv", "w_out", "ffn2_norm", "ffn2_w_gate",
           "ffn2_w_up", "ffn2_w_down", "final_norm")


def _params(*sem):
    return pltpu.CompilerParams(dimension_semantics=sem, vmem_limit_bytes=VMEM_LIMIT_BYTES)


def _div_tile(n, cap):
    if n <= cap:
        return n
    best = 0
    for t in range(LANES, cap + 1, LANES):
        if n % t == 0:
            best = t
    assert best, (n, cap)
    return best


def _mm(name, a, b, *, ta=False, tb=False, res=None, alpha=1.0, out_dtype=BF16):
    m, k = (a.shape[1], a.shape[0]) if ta else a.shape
    n, kb = (b.shape[0], b.shape[1]) if tb else (b.shape[1], b.shape[0])
    assert k == kb, (name, a.shape, b.shape)
    tm, tn, tk = _div_tile(m, 512), _div_tile(n, 512), _div_tile(k, 2048)
    nk = k // tk
    a_spec = (pl.BlockSpec((tk, tm), lambda i, j, kk: (kk, i)) if ta
              else pl.BlockSpec((tm, tk), lambda i, j, kk: (i, kk)))
    b_spec = (pl.BlockSpec((tn, tk), lambda i, j, kk: (j, kk)) if tb
              else pl.BlockSpec((tk, tn), lambda i, j, kk: (kk, j)))
    dims = (((0 if ta else 1,), (1 if tb else 0,)), ((), ()))
    operands, specs = [a, b], [a_spec, b_spec]
    if res is not None:
        operands.append(res)
        specs.append(pl.BlockSpec((tm, tn), lambda i, j, kk: (i, j)))

    def body(*refs):
        a_ref, b_ref = refs[0], refs[1]
        o_ref, acc_ref = refs[-2], refs[-1]
        kk = pl.program_id(2)

        @pl.when(kk == 0)
        def _():
            acc_ref[...] = jnp.zeros_like(acc_ref)

        acc_ref[...] += lax.dot_general(a_ref[...], b_ref[...], dims, preferred_element_type=F32)

        @pl.when(kk == nk - 1)
        def _():
            out = acc_ref[...] * alpha
            if res is not None:
                out = out + refs[2][...].astype(F32)
            o_ref[...] = out.astype(o_ref.dtype)

    return pl.pallas_call(
        body, name=name, grid=(m // tm, n // tn, nk), in_specs=specs,
        out_specs=pl.BlockSpec((tm, tn), lambda i, j, kk: (i, j)),
        out_shape=jax.ShapeDtypeStruct((m, n), out_dtype),
        scratch_shapes=[pltpu.VMEM((tm, tn), F32)],
        compiler_params=_params("parallel", "parallel", "arbitrary"))(*operands)


def _ffn_up(name, h, wg, wu):
    s, d = h.shape
    f = wg.shape[1]
    tm, tn = _div_tile(s, 512), _div_tile(f, 512)

    def body(h_ref, wg_ref, wu_ref, a_ref, b_ref, z_ref):
        hv = h_ref[...]
        a = jnp.dot(hv, wg_ref[...], preferred_element_type=F32)
        b = jnp.dot(hv, wu_ref[...], preferred_element_type=F32)
        a_ref[...] = a.astype(BF16)
        b_ref[...] = b.astype(BF16)
        z_ref[...] = (a * jax.nn.sigmoid(a) * b).astype(BF16)

    tile = pl.BlockSpec((tm, tn), lambda i, j: (i, j))
    w_spec = pl.BlockSpec((d, tn), lambda i, j: (0, j))
    return pl.pallas_call(
        body, name=name, grid=(s // tm, f // tn),
        in_specs=[pl.BlockSpec((tm, d), lambda i, j: (i, 0)), w_spec, w_spec],
        out_specs=[tile, tile, tile],
        out_shape=[jax.ShapeDtypeStruct((s, f), BF16)] * 3,
        compiler_params=_params("parallel", "parallel"))(h, wg, wu)


def _ffn_bwd_mid(name, dy, wd, a, b, alpha):
    s, d = dy.shape
    f = wd.shape[0]
    tm, tn = _div_tile(s, 512), _div_tile(f, 512)

    def body(dy_ref, wd_ref, a_ref, b_ref, da_ref, db_ref):
        dz = lax.dot_general(dy_ref[...], wd_ref[...], (((1,), (1,)), ((), ())),
                             preferred_element_type=F32) * alpha
        av = a_ref[...].astype(F32)
        sg = jax.nn.sigmoid(av)
        db_ref[...] = (dz * av * sg).astype(BF16)
        da_ref[...] = (dz * b_ref[...].astype(F32) * (sg * (1.0 + av * (1.0 - sg)))).astype(BF16)

    tile = pl.BlockSpec((tm, tn), lambda i, j: (i, j))
    return pl.pallas_call(
        body, name=name, grid=(s // tm, f // tn),
        in_specs=[pl.BlockSpec((tm, d), lambda i, j: (i, 0)), pl.BlockSpec((tn, d), lambda i, j: (j, 0)),
                  tile, tile],
        out_specs=[tile, tile],
        out_shape=[jax.ShapeDtypeStruct((s, f), BF16)] * 2,
        compiler_params=_params("parallel", "parallel"))(dy, wd, a, b)


def _rms_fwd(name, x, g, *, width=None, col=0):
    s = x.shape[0]
    w = x.shape[1] if width is None else width
    assert col % w == 0
    cb, tr = col // w, min(256, s)

    def body(x_ref, g_ref, h_ref):
        xf = x_ref[...].astype(F32)
        r = lax.rsqrt(jnp.mean(xf * xf, axis=-1, keepdims=True) + EPS)
        h_ref[...] = (xf * r * g_ref[...]).astype(BF16)

    return pl.pallas_call(
        body, name=name, grid=(s // tr,),
        in_specs=[pl.BlockSpec((tr, w), lambda i: (i, cb)), pl.BlockSpec((1, w), lambda i: (0, 0))],
        out_specs=pl.BlockSpec((tr, w), lambda i: (i, 0)),
        out_shape=jax.ShapeDtypeStruct((s, w), BF16),
        compiler_params=_params("parallel"))(x, g)


def _rms_bwd(name, dh, x, g, *, res=None, width=None, col=0):
    s = x.shape[0]
    w = x.shape[1] if width is None else width
    assert col % w == 0
    cb, tr = col // w, min(256, s)
    has_res = res is not None

    def body(*refs):
        dh_ref, x_ref, g_ref = refs[:3]
        dg_ref = refs[-1]
        i = pl.program_id(0)
        xf = x_ref[...].astype(F32)
        r = lax.rsqrt(jnp.mean(xf * xf, axis=-1, keepdims=True) + EPS)
        xh = xf * r
        d = dh_ref[...].astype(F32)
        dxh = d * g_ref[...]
        dx = r * (dxh - xh * jnp.mean(dxh * xh, axis=-1, keepdims=True))
        if has_res:
            dx = dx + refs[3][...]
            refs[4][...] = dx
            refs[5][...] = dx.astype(BF16)
        else:
            refs[3][...] = dx.astype(BF16)

        @pl.when(i == 0)
        def _():
            dg_ref[...] = jnp.zeros_like(dg_ref)

        dg_ref[0:1, :] += jnp.sum(d * xh, axis=0, keepdims=True)

    row = pl.BlockSpec((tr, w), lambda i: (i, 0))
    in_specs = [row, pl.BlockSpec((tr, w), lambda i: (i, cb)), pl.BlockSpec((1, w), lambda i: (0, 0))]
    operands = [dh, x, g]
    dg_spec = pl.BlockSpec((8, w), lambda i: (0, 0))
    dg_shape = jax.ShapeDtypeStruct((8, w), F32)
    if has_res:
        in_specs.append(row)
        operands.append(res)
        out_specs = [row, row, dg_spec]
        out_shape = [jax.ShapeDtypeStruct((s, w), F32), jax.ShapeDtypeStruct((s, w), BF16), dg_shape]
    else:
        out_specs = [row, dg_spec]
        out_shape = [jax.ShapeDtypeStruct((s, w), BF16), dg_shape]
    return pl.pallas_call(
        body, name=name, grid=(s // tr,), in_specs=in_specs, out_specs=out_specs, out_shape=out_shape,
        compiler_params=_params("arbitrary"))(*operands)


def _loss_head(name, x, g, target):
    s, d = x.shape
    tr = min(256, s)
    n = s // tr

    def body(x_ref, g_ref, t_ref, dx_ref, dxb_ref, dg_ref, loss_ref, sq_ref):
        i = pl.program_id(0)
        xf = x_ref[...]
        r = lax.rsqrt(jnp.mean(xf * xf, axis=-1, keepdims=True) + EPS)
        xh = xf * r
        gv = g_ref[...]
        err = xh * gv - t_ref[...]
        dy = err * (1.0 / d)
        dxh = dy * gv
        dx = r * (dxh - xh * jnp.mean(dxh * xh, axis=-1, keepdims=True))
        dx_ref[...] = dx
        dxb_ref[...] = dx.astype(BF16)

        @pl.when(i == 0)
        def _():
            dg_ref[...] = jnp.zeros_like(dg_ref)
            sq_ref[...] = jnp.zeros_like(sq_ref)

        dg_ref[0:1, :] += jnp.sum(dy * xh, axis=0, keepdims=True)
        sq_ref[...] += jnp.sum(err * err, axis=0, keepdims=True)

        @pl.when(i == n - 1)
        def _():
            total = jnp.sum(sq_ref[...], axis=1, keepdims=True) * (0.5 / d)
            loss_ref[...] = jnp.broadcast_to(total, loss_ref.shape)

    row = pl.BlockSpec((tr, d), lambda i: (i, 0))
    return pl.pallas_call(
        body, name=name, grid=(n,),
        in_specs=[row, pl.BlockSpec((1, d), lambda i: (0, 0)), row],
        out_specs=[row, row, pl.BlockSpec((8, d), lambda i: (0, 0)), pl.BlockSpec((8, LANES), lambda i: (0, 0))],
        out_shape=[jax.ShapeDtypeStruct((s, d), F32), jax.ShapeDtypeStruct((s, d), BF16),
                   jax.ShapeDtypeStruct((8, d), F32), jax.ShapeDtypeStruct((8, LANES), F32)],
        scratch_shapes=[pltpu.VMEM((1, d), F32)],
        compiler_params=_params("arbitrary"))(x, g, target)


def _rope(name, x, tabs, half, *, width, col=0, nblk=1, transpose=False, out_dtype=BF16):
    s = x.shape[0]
    assert col % width == 0
    cb0, tr = col // width, min(256, s)

    def body(x_ref, c_ref, s1_ref, s2_ref, y_ref):
        xf = x_ref[...].astype(F32)
        if transpose:
            y = (xf * c_ref[...] + pltpu.roll(xf * s1_ref[...], half, 1)
                 + pltpu.roll(xf * s2_ref[...], width - half, 1))
        else:
            y = (xf * c_ref[...] + pltpu.roll(xf, width - half, 1) * s1_ref[...]
                 + pltpu.roll(xf, half, 1) * s2_ref[...])
        y_ref[...] = y.astype(y_ref.dtype)

    tab = pl.BlockSpec((tr, width), lambda i, j: (i, 0))
    return pl.pallas_call(
        body, name=name, grid=(s // tr, nblk),
        in_specs=[pl.BlockSpec((tr, width), lambda i, j: (i, cb0 + j)), tab, tab, tab],
        out_specs=pl.BlockSpec((tr, width), lambda i, j: (i, j)),
        out_shape=jax.ShapeDtypeStruct((s, nblk * width), out_dtype),
        compiler_params=_params("parallel", "parallel"))(x, *tabs)


def _split3(v):
    hi = v.astype(BF16)
    r1 = v - hi.astype(F32)
    mid = r1.astype(BF16)
    lo = (r1 - mid.astype(F32)).astype(BF16)
    return hi, mid, lo


def _tri_matmul(tri, v):
    hi, mid, lo = _split3(v)
    return (jnp.dot(tri, hi, preferred_element_type=F32) + jnp.dot(tri, mid, preferred_element_type=F32)
            + jnp.dot(tri, lo, preferred_element_type=F32))


def _log_sigmoid(v):
    return jnp.minimum(v, 0.0) - jnp.log(1.0 + jnp.exp(-jnp.abs(v)))


def _gate_fwd(name, tail, bias, mask):
    s = tail.shape[0]
    tr = min(512, s)

    def body(t_ref, b_ref, m_ref, cum_ref, carry_ref):
        i = pl.program_id(0)

        @pl.when(i == 0)
        def _():
            carry_ref[...] = jnp.zeros_like(carry_ref)

        lf = _log_sigmoid(t_ref[...] + b_ref[...]) * m_ref[...]
        r = lax.broadcasted_iota(jnp.int32, (tr, tr), 0)
        c = lax.broadcasted_iota(jnp.int32, (tr, tr), 1)
        cum = _tri_matmul((r >= c).astype(BF16), lf) + carry_ref[...]
        cum_ref[...] = cum
        carry_ref[...] = cum[tr - 1:tr, :]

    vec = pl.BlockSpec((1, LANES), lambda i: (0, 0))
    return pl.pallas_call(
        body, name=name, grid=(s // tr,),
        in_specs=[pl.BlockSpec((tr, LANES), lambda i: (i, 0)), vec, vec],
        out_specs=pl.BlockSpec((tr, LANES), lambda i: (i, 0)),
        out_shape=jax.ShapeDtypeStruct((s, LANES), F32),
        scratch_shapes=[pltpu.VMEM((1, LANES), F32)],
        compiler_params=_params("arbitrary"))(tail, bias, mask)


def _gate_bwd(name, dcum, tail, bias, mask):
    s = tail.shape[0]
    tr = min(512, s)
    n = s // tr

    def body(dc_ref, t_ref, b_ref, m_ref, dt_ref, db_ref, carry_ref):
        i = pl.program_id(0)

        @pl.when(i == 0)
        def _():
            carry_ref[...] = jnp.zeros_like(carry_ref)
            db_ref[...] = jnp.zeros_like(db_ref)

        r = lax.broadcasted_iota(jnp.int32, (tr, tr), 0)
        c = lax.broadcasted_iota(jnp.int32, (tr, tr), 1)
        dlf = _tri_matmul((r <= c).astype(BF16), dc_ref[...]) + carry_ref[...]
        carry_ref[...] = dlf[0:1, :]
        v = t_ref[...] + b_ref[...]
        dt = dlf * jax.nn.sigmoid(-v) * m_ref[...]
        dt_ref[...] = dt
        db_ref[0:1, :] += jnp.sum(dt, axis=0, keepdims=True)

    vec = pl.BlockSpec((1, LANES), lambda i: (0, 0))
    rev = pl.BlockSpec((tr, LANES), lambda i: (n - 1 - i, 0))
    return pl.pallas_call(
        body, name=name, grid=(n,),
        in_specs=[rev, rev, vec, vec],
        out_specs=[rev, pl.BlockSpec((8, LANES), lambda i: (0, 0))],
        out_shape=[jax.ShapeDtypeStruct((s, LANES), F32), jax.ShapeDtypeStruct((8, LANES), F32)],
        scratch_shapes=[pltpu.VMEM((1, LANES), F32)],
        compiler_params=_params("arbitrary"))(dcum, tail, bias, mask)


NT = (((1,), (1,)), ((), ()))
TN = (((0,), (0,)), ((), ()))


def _weights_and_scores(mode, scale, t, qi, kb, q1, k1, q2, k2, cq, ck):
    sc = lax.dot_general(q1, k1, NT, preferred_element_type=F32)
    if q2 is not None:
        sc = sc + lax.dot_general(q2, k2, NT, preferred_element_type=F32)
    sc = sc * scale
    if cq is not None:
        sc = sc + (cq - ck)
    row = lax.broadcasted_iota(jnp.int32, (t, t), 0)
    col = lax.broadcasted_iota(jnp.int32, (t, t), 1)
    dist = row - col + (qi - kb) * t
    back = dist >= 0
    if mode == "dil":
        w = jnp.zeros((t, t), F32)
        for window, dil in DIL_BRANCHES:
            w = w + (back & (dist <= window) & ((dist & (dil - 1)) == 0)).astype(F32)
    else:
        w = back.astype(F32)
    return jnp.where(w > 0.0, sc, NEG), w


def _attn_geometry(mode, s):
    t = min(ATTN_TILE, s)
    nq = s // t
    nsteps = min(nq, DIL_REACH // t + 1) if mode == "dil" else nq
    return t, nq, nsteps


def _attn_operands(mode, nh, t, ops, hf, qf, kf):
    def cols(width, base, rows):
        return pl.BlockSpec((t, width), lambda *g: (rows(*g), base + hf(*g)))

    if mode == "fox":
        pm, ccol, crow = ops
        return ([pm, pm, pm, ccol, crow],
                [cols(HEAD, 0, qf), cols(HEAD, nh, kf), cols(HEAD, 2 * nh, kf),
                 pl.BlockSpec((None, t, 1), lambda *g: (hf(*g), qf(*g), 0)),
                 pl.BlockSpec((None, 1, t), lambda *g: (hf(*g), 0, kf(*g)))])
    if mode == "mla":
        qb, kv, kr = ops
        return ([qb, kv, kr],
                [cols(2 * HEAD, 0, qf), cols(2 * HEAD, 0, kf), pl.BlockSpec((t, HEAD), lambda *g: (kf(*g), 0))])
    q, k, pm, vbase = ops
    return [q, k, pm], [cols(HEAD, 0, qf), cols(HEAD, 0, kf), cols(HEAD, vbase, kf)]


def _unpack(mode, refs):
    if mode == "fox":
        return refs[0][...], refs[1][...], refs[2][...], None, None, refs[3][...], refs[4][...]
    if mode == "mla":
        q, kv = refs[0][...], refs[1][...]
        return q[:, :HEAD], kv[:, :HEAD], kv[:, HEAD:], q[:, HEAD:], refs[2][...], None, None
    return refs[0][...], refs[1][...], refs[2][...], None, None, None, None


def _attn_fwd(name, mode, nh, s, ops, scale):
    t, nq, nsteps = _attn_geometry(mode, s)
    hf = lambda h, i, kk: h
    qf = lambda h, i, kk: i
    kf = lambda h, i, kk: jnp.maximum(i - (nsteps - 1) + kk, 0)
    operands, specs = _attn_operands(mode, nh, t, ops, hf, qf, kf)
    n_in = len(operands)

    def body(*refs):
        o_ref, lse_ref, m_ref, l_ref, acc_ref = refs[n_in:]
        i, kk = pl.program_id(1), pl.program_id(2)
        kb = i - (nsteps - 1) + kk

        @pl.when(kk == 0)
        def _():
            m_ref[...] = jnp.full_like(m_ref, NEG)
            l_ref[...] = jnp.zeros_like(l_ref)
            acc_ref[...] = jnp.zeros_like(acc_ref)

        @pl.when(kb >= 0)
        def _():
            q1, k1, v, q2, k2, cq, ck = _unpack(mode, refs)
            sc, w = _weights_and_scores(mode, scale, t, i, kb, q1, k1, q2, k2, cq, ck)
            m_old = m_ref[...]
            m_new = jnp.maximum(m_old, jnp.max(sc, axis=1, keepdims=True))
            p = w * jnp.exp(sc - m_new)
            a = jnp.exp(m_old - m_new)
            l_ref[...] = a * l_ref[...] + jnp.sum(p, axis=1, keepdims=True)
            acc_ref[...] = a * acc_ref[...] + jnp.dot(p.astype(BF16), v, preferred_element_type=F32)
            m_ref[...] = m_new

        @pl.when(kk == nsteps - 1)
        def _():
            o_ref[...] = (acc_ref[...] / l_ref[...]).astype(BF16)
            lse_ref[...] = m_ref[...] + jnp.log(l_ref[...])

    return pl.pallas_call(
        body, name=name, grid=(nh, nq, nsteps), in_specs=specs,
        out_specs=[pl.BlockSpec((t, HEAD), lambda h, i, kk: (i, h)),
                   pl.BlockSpec((None, t, 1), lambda h, i, kk: (h, i, 0))],
        out_shape=[jax.ShapeDtypeStruct((s, nh * HEAD), BF16), jax.ShapeDtypeStruct((nh, s, 1), F32)],
        scratch_shapes=[pltpu.VMEM((t, 1), F32), pltpu.VMEM((t, 1), F32), pltpu.VMEM((t, HEAD), F32)],
        compiler_params=_params("parallel", "parallel", "arbitrary"))(*operands)


def _attn_bwd_q(name, mode, nh, s, ops, scale, o, do, do_base, lse):
    t, nq, nsteps = _attn_geometry(mode, s)
    hf = lambda h, i, kk: h
    qf = lambda h, i, kk: i
    kf = lambda h, i, kk: jnp.maximum(i - (nsteps - 1) + kk, 0)
    operands, specs = _attn_operands(mode, nh, t, ops, hf, qf, kf)
    operands += [o, do, lse]
    specs += [pl.BlockSpec((t, HEAD), lambda h, i, kk: (i, do_base + h)),
              pl.BlockSpec((t, HEAD), lambda h, i, kk: (i, do_base + h)),
              pl.BlockSpec((None, t, 1), lambda h, i, kk: (h, i, 0))]
    n_in = len(operands)
    wq = 2 * HEAD if mode == "mla" else HEAD

    fox = mode == "fox"

    def body(*refs):
        o_ref, do_ref, lse_ref = refs[n_in - 3:n_in]
        dq_ref = refs[n_in]
        acc_ref, delta_ref, rows_ref = refs[-3:]
        i, kk = pl.program_id(1), pl.program_id(2)
        kb = i - (nsteps - 1) + kk

        @pl.when(kk == 0)
        def _():
            acc_ref[...] = jnp.zeros_like(acc_ref)
            rows_ref[...] = jnp.zeros_like(rows_ref)
            delta_ref[...] = jnp.sum(o_ref[...].astype(F32) * do_ref[...].astype(F32), axis=1, keepdims=True)

        @pl.when(kb >= 0)
        def _():
            q1, k1, v, q2, k2, cq, ck = _unpack(mode, refs)
            sc, w = _weights_and_scores(mode, scale, t, i, kb, q1, k1, q2, k2, cq, ck)
            p = w * jnp.exp(sc - lse_ref[...])
            dp = lax.dot_general(do_ref[...], v, NT, preferred_element_type=F32)
            ds32 = p * (dp - delta_ref[...])
            ds = ds32.astype(BF16)
            if fox:
                rows_ref[...] += jnp.sum(ds32, axis=1, keepdims=True)
            if mode == "mla":
                acc_ref[:, :HEAD] += jnp.dot(ds, k1, preferred_element_type=F32)
                acc_ref[:, HEAD:] += jnp.dot(ds, k2, preferred_element_type=F32)
            else:
                acc_ref[...] += jnp.dot(ds, k1, preferred_element_type=F32)

        @pl.when(kk == nsteps - 1)
        def _():
            dq_ref[...] = (acc_ref[...] * scale).astype(BF16)
            if fox:
                refs[n_in + 1][...] = rows_ref[...]

    out_specs = [pl.BlockSpec((t, wq), lambda h, i, kk: (i, h))]
    out_shape = [jax.ShapeDtypeStruct((s, nh * wq), BF16)]
    if fox:
        out_specs.append(pl.BlockSpec((None, t, 1), lambda h, i, kk: (h, i, 0)))
        out_shape.append(jax.ShapeDtypeStruct((nh, s, 1), F32))
    out = pl.pallas_call(
        body, name=name, grid=(nh, nq, nsteps), in_specs=specs, out_specs=out_specs, out_shape=out_shape,
        scratch_shapes=[pltpu.VMEM((t, wq), F32), pltpu.VMEM((t, 1), F32), pltpu.VMEM((t, 1), F32)],
        compiler_params=_params("parallel", "parallel", "arbitrary"))(*operands)
    return out if fox else out[0]


def _attn_bwd_kv(name, mode, nh, s, ops, scale, o, do, do_base, lse):
    t, nq, nsteps = _attn_geometry(mode, s)
    hf = lambda j, h, qq: h
    qf = lambda j, h, qq: jnp.minimum(j + qq, nq - 1)
    kf = lambda j, h, qq: j
    operands, specs = _attn_operands(mode, nh, t, ops, hf, qf, kf)
    operands += [o, do, lse]
    specs += [pl.BlockSpec((t, HEAD), lambda j, h, qq: (qf(j, h, qq), do_base + h)),
              pl.BlockSpec((t, HEAD), lambda j, h, qq: (qf(j, h, qq), do_base + h)),
              pl.BlockSpec((None, t, 1), lambda j, h, qq: (h, qf(j, h, qq), 0))]
    n_in = len(operands)
    head_tile = pl.BlockSpec((t, HEAD), lambda j, h, qq: (j, h))
    if mode == "fox":
        out_specs = [head_tile, head_tile, pl.BlockSpec((None, 1, t), lambda j, h, qq: (h, 0, j))]
        out_shape = [jax.ShapeDtypeStruct((s, nh * HEAD), BF16)] * 2 + [jax.ShapeDtypeStruct((nh, 1, s), F32)]
        scratch = [pltpu.VMEM((t, HEAD), F32), pltpu.VMEM((t, HEAD), F32), pltpu.VMEM((1, t), F32)]
    elif mode == "mla":
        out_specs = [pl.BlockSpec((t, 2 * HEAD), lambda j, h, qq: (j, h)),
                     pl.BlockSpec((t, HEAD), lambda j, h, qq: (j, 0))]
        out_shape = [jax.ShapeDtypeStruct((s, nh * 2 * HEAD), BF16), jax.ShapeDtypeStruct((s, HEAD), F32)]
        scratch = [pltpu.VMEM((t, HEAD), F32), pltpu.VMEM((t, HEAD), F32), pltpu.VMEM((t, HEAD), F32)]
    else:
        out_specs = [head_tile, head_tile]
        out_shape = [jax.ShapeDtypeStruct((s, nh * HEAD), BF16)] * 2
        scratch = [pltpu.VMEM((t, HEAD), F32), pltpu.VMEM((t, HEAD), F32)]
    n_out = len(out_specs)

    def body(*refs):
        o_ref, do_ref, lse_ref = refs[n_in - 3:n_in]
        outs = refs[n_in:n_in + n_out]
        dk_acc, dv_acc = refs[n_in + n_out], refs[n_in + n_out + 1]
        extra = refs[n_in + n_out + 2] if mode != "dil" else None
        j, h, qq = pl.program_id(0), pl.program_id(1), pl.program_id(2)
        qb = j + qq

        @pl.when(qq == 0)
        def _():
            dk_acc[...] = jnp.zeros_like(dk_acc)
            dv_acc[...] = jnp.zeros_like(dv_acc)
            if mode == "fox":
                extra[...] = jnp.zeros_like(extra)

        if mode == "mla":
            @pl.when((qq == 0) & (h == 0))
            def _():
                extra[...] = jnp.zeros_like(extra)

        @pl.when(qb < nq)
        def _():
            q1, k1, v, q2, k2, cq, ck = _unpack(mode, refs)
            sc, w = _weights_and_scores(mode, scale, t, qb, j, q1, k1, q2, k2, cq, ck)
            p = w * jnp.exp(sc - lse_ref[...])
            dov = do_ref[...]
            delta = jnp.sum(o_ref[...].astype(F32) * dov.astype(F32), axis=1, keepdims=True)
            dv_acc[...] += lax.dot_general(p.astype(BF16), dov, TN, preferred_element_type=F32)
            dp = lax.dot_general(dov, v, NT, preferred_element_type=F32)
            ds = p * (dp - delta)
            dsb = ds.astype(BF16)
            dk_acc[...] += lax.dot_general(dsb, q1, TN, preferred_element_type=F32)
            if mode == "mla":
                extra[...] += lax.dot_general(dsb, q2, TN, preferred_element_type=F32)
            if mode == "fox":
                extra[...] -= jnp.sum(ds, axis=0, keepdims=True)

        @pl.when(qq == nsteps - 1)
        def _():
            if mode == "mla":
                outs[0][:, :HEAD] = (dk_acc[...] * scale).astype(BF16)
                outs[0][:, HEAD:] = dv_acc[...].astype(BF16)
            else:
                outs[0][...] = (dk_acc[...] * scale).astype(BF16)
                outs[1][...] = dv_acc[...].astype(BF16)
            if mode == "fox":
                outs[2][...] = extra[...]

        if mode == "mla":
            @pl.when((qq == nsteps - 1) & (h == nh - 1))
            def _():
                outs[1][...] = extra[...] * scale

    return pl.pallas_call(
        body, name=name, grid=(nq, nh, nsteps), in_specs=specs, out_specs=out_specs, out_shape=out_shape,
        scratch_shapes=scratch,
        compiler_params=_params("parallel", "arbitrary", "arbitrary"))(*operands)


def _adamw(name, w, g, m, v):
    nl, r, c = w.shape
    tr = _div_tile(r, 128) if r % 128 == 0 else r

    def body(w_ref, g_ref, m_ref, v_ref, d_ref, nm_ref, nv_ref):
        gv = g_ref[...]
        nm = ADAM_B1 * m_ref[...] + (1.0 - ADAM_B1) * gv
        nv = ADAM_B2 * v_ref[...] + (1.0 - ADAM_B2) * jnp.square(gv)
        m_hat = nm / (1.0 - ADAM_B1 ** ADAM_STEP)
        v_hat = nv / (1.0 - ADAM_B2 ** ADAM_STEP)
        d_ref[...] = -ADAM_LR * (m_hat / (jnp.sqrt(v_hat) + ADAM_EPS) + ADAM_WD * w_ref[...])
        nm_ref[...] = nm
        nv_ref[...] = nv

    blk = pl.BlockSpec((None, tr, c), lambda l, i: (l, i, 0))
    return pl.pallas_call(
        body, name=name, grid=(nl, r // tr), in_specs=[blk] * 4, out_specs=[blk] * 3,
        out_shape=[jax.ShapeDtypeStruct(w.shape, F32)] * 3,
        compiler_params=_params("parallel", "parallel"))(w, g, m, v)


def _add_half(name, g, recv, c_arr):
    nl, r, c = g.shape
    hl = nl // 2
    tr = _div_tile(r, 128) if r % 128 == 0 else r

    def body(c_ref, g_ref, r_ref, o_ref):
        o_ref[...] = (g_ref[...] + r_ref[...]).astype(BF16)

    grid_spec = pltpu.PrefetchScalarGridSpec(
        num_scalar_prefetch=1, grid=(hl, r // tr),
        in_specs=[pl.BlockSpec((None, tr, c), lambda l, i, c_ref: (c_ref[0] * hl + l, i, 0)),
                  pl.BlockSpec((None, tr, c), lambda l, i, c_ref: (l, i, 0))],
        out_specs=pl.BlockSpec((None, tr, c), lambda l, i, c_ref: (l, i, 0)))
    return pl.pallas_call(
        body, name=name, grid_spec=grid_spec, out_shape=jax.ShapeDtypeStruct((hl, r, c), BF16),
        compiler_params=_params("parallel", "parallel"))(c_arr, g, recv)


def _sum_slots(name, rc, order, out_dtype=F32):
    _, nl, r, c = rc.shape
    tr = _div_tile(r, 128) if r % 128 == 0 else r

    def body(*refs):
        acc = refs[0][...].astype(F32)
        for ref in refs[1:-1]:
            acc = acc + ref[...].astype(F32)
        refs[-1][...] = acc.astype(out_dtype)

    def slot(k):
        return pl.BlockSpec((None, None, tr, c), lambda l, i: (k, l, i, 0))

    return pl.pallas_call(
        body, name=name, grid=(nl, r // tr), in_specs=[slot(k) for k in order],
        out_specs=pl.BlockSpec((None, tr, c), lambda l, i: (l, i, 0)),
        out_shape=jax.ShapeDtypeStruct((nl, r, c), out_dtype),
        compiler_params=_params("parallel", "parallel"))(*([rc] * len(order)))


def _exchange(name, ins, out_shapes, plan, n_remote, n_local):
    n_in, n_out = len(ins), len(out_shapes)

    def body(*refs):
        in_refs, out_refs = refs[:n_in], refs[n_in:n_in + n_out]
        send_sems, recv_sems, local_sems = refs[n_in + n_out:]
        x, y, c = lax.axis_index("x"), lax.axis_index("y"), lax.axis_index("c")
        remote, local = plan(in_refs, out_refs, x, y, c)
        assert len(remote) == n_remote and len(local) == n_local
        copies = []
        for k, (src, dst, (fx, fy, fc)) in enumerate(remote):
            peer = (1 - x if fx else x, 1 - y if fy else y, 1 - c if fc else c)
            copies.append(pltpu.make_async_remote_copy(
                src_ref=src, dst_ref=dst, send_sem=send_sems.at[k], recv_sem=recv_sems.at[k],
                device_id=peer, device_id_type=MESH))
        own = [pltpu.make_async_copy(src, dst, local_sems.at[k]) for k, (src, dst) in enumerate(local)]
        for cp in copies + own:
            cp.start()
        for cp in copies:
            cp.wait_recv()
        for cp in copies:
            cp.wait_send()
        for cp in own:
            cp.wait()

    return pl.pallas_call(
        body, name=name, in_specs=[ANY] * n_in, out_specs=[ANY] * n_out, out_shape=list(out_shapes),
        scratch_shapes=[pltpu.SemaphoreType.DMA((n_remote,)), pltpu.SemaphoreType.DMA((n_remote,)),
                        pltpu.SemaphoreType.DMA((max(n_local, 1),))])(*ins)


CHIP_FLIPS = ((1, 0), (0, 1), (1, 1))


def _allgather_weight(name, w):
    nl = w.shape[0]
    hl = nl // 2

    def body(w_ref, out_ref, send_sems, recv_sems, local_sem):
        x, y, c = lax.axis_index("x"), lax.axis_index("y"), lax.axis_index("c")
        sibling = (x, y, 1 - c)
        chips = [(1 - x if fx else x, 1 - y if fy else y) for fx, fy in CHIP_FLIPS]

        def half(chip, core):
            return out_ref.at[2 * chip[0] + chip[1], pl.ds(core * hl, hl)]

        def copy(k, src, dst, to):
            return pltpu.make_async_remote_copy(src_ref=src, dst_ref=dst, send_sem=send_sems.at[k],
                                                recv_sem=recv_sems.at[k], device_id=to, device_id_type=MESH)

        mine = pltpu.make_async_copy(w_ref, out_ref.at[2 * x + y], local_sem)
        mine.start()
        first = [copy(j, w_ref.at[pl.ds(c * hl, hl)], half((x, y), c), (*chip, c)) for j, chip in enumerate(chips)]
        for cp in first:
            cp.start()
        passed = [copy(3 + j, half(chip, c), half(chip, c), sibling) for j, chip in enumerate(chips)]
        for j, chip in enumerate(chips):
            copy(j, half(chip, c), half(chip, c), (*chip, c)).wait_recv()
            passed[j].start()
        for j, chip in enumerate(chips):
            copy(3 + j, half(chip, 1 - c), half(chip, 1 - c), sibling).wait_recv()
        for cp in first + passed:
            cp.wait_send()
        mine.wait()

    return pl.pallas_call(
        body, name=name, in_specs=[ANY], out_specs=ANY,
        out_shape=jax.ShapeDtypeStruct((4,) + w.shape, w.dtype),
        scratch_shapes=[pltpu.SemaphoreType.DMA((6,)), pltpu.SemaphoreType.DMA((6,)),
                        pltpu.SemaphoreType.DMA])(w)


def _sibling_halves(name, g):
    hl = g.shape[0] // 2

    def plan(ins, outs, x, y, c):
        return [(ins[0].at[pl.ds((1 - c) * hl, hl)], outs[0], (0, 0, 1))], []

    return _exchange(name, [g], [jax.ShapeDtypeStruct((hl,) + g.shape[1:], g.dtype)], plan, 1, 0)[0]


def _chip_exchange(name, pst):
    def plan(ins, outs, x, y, c):
        remote = []
        for k, (fx, fy) in enumerate(CHIP_FLIPS):
            px, py = (1 - x if fx else x), (1 - y if fy else y)
            remote.append((ins[0].at[2 * px + py], outs[0].at[k], (fx, fy, 0)))
        return remote, [(ins[0].at[2 * x + y], outs[0].at[3])]

    return _exchange(name, [pst], [jax.ShapeDtypeStruct(pst.shape, pst.dtype)], plan, 3, 1)[0]


def _swap_halves(name, s):
    hl = s.shape[0]

    def plan(ins, outs, x, y, c):
        mine = outs[0].at[pl.ds(c * hl, hl)]
        return [(ins[0], mine, (0, 0, 1))], [(ins[0], mine)]

    return _exchange(name, [s], [jax.ShapeDtypeStruct((2 * hl,) + s.shape[1:], s.dtype)], plan, 1, 1)[0]


def _gather_all(name, blk):
    def plan(ins, outs, x, y, c):
        mine = outs[0].at[4 * x + 2 * y + c]
        remote = [(ins[0], mine, (m >> 2 & 1, m >> 1 & 1, m & 1)) for m in range(1, 8)]
        return remote, [(ins[0], mine)]

    return _exchange(name, [blk], [jax.ShapeDtypeStruct((8,) + blk.shape, blk.dtype)], plan, 7, 1)[0]


class _Dims:
    def __init__(self, d, in_w):
        self.d = d
        self.fox_h = (d // 4) // HEAD
        self.mla_h = (d // 2) // HEAD
        self.dil_h = (d // 4) // HEAD
        self.gw = d // 4
        self.rank = d // 4
        gw, fh = self.gw, self.fox_h
        sizes = (gw, gw, gw, fh, self.rank, self.rank, MLA_ROPE, gw, gw, gw)
        assert sum(sizes) == in_w
        offs = [0]
        for z in sizes:
            offs.append(offs[-1] + z)
        self.nat = dict(zip(("fq", "fk", "fv", "fl", "cq", "ckv", "kr", "dq", "dk", "dv"), zip(offs[:-1], sizes)))
        self.main_order = ("fq", "fk", "fv", "cq", "ckv", "dq", "dk", "dv")
        self.main_w = 8 * gw
        self.col = {n: i * gw for i, n in enumerate(self.main_order)}

    def align_w_in(self, w):
        parts = [w[:, self.nat[n][0]:self.nat[n][0] + self.nat[n][1]] for n in self.main_order]
        parts += [w[:, self.nat["kr"][0]:self.nat["kr"][0] + MLA_ROPE],
                  w[:, self.nat["fl"][0]:self.nat["fl"][0] + self.fox_h],
                  jnp.zeros((w.shape[0], LANES - MLA_ROPE - self.fox_h), w.dtype)]
        return jnp.concatenate(parts, axis=1)

    def unalign_w_in(self, g):
        gw, t0 = self.gw, self.main_w
        src = {n: g[:, self.col[n]:self.col[n] + gw] for n in self.main_order}
        src["kr"] = g[:, t0:t0 + MLA_ROPE]
        src["fl"] = g[:, t0 + FORGET_LANE:t0 + FORGET_LANE + self.fox_h]
        return jnp.concatenate([src[n] for n in ("fq", "fk", "fv", "fl", "cq", "ckv", "kr", "dq", "dk", "dv")], axis=1)

    def pad_w_uq(self, w):
        r = w.shape[0]
        w3 = w.reshape(r, self.mla_h, HEAD + MLA_ROPE)
        return jnp.pad(w3, ((0, 0), (0, 0), (0, HEAD - MLA_ROPE))).reshape(r, self.mla_h * 2 * HEAD)

    def unpad_w_uq(self, g):
        r = g.shape[0]
        return g.reshape(r, self.mla_h, 2 * HEAD)[:, :, :HEAD + MLA_ROPE].reshape(r, self.mla_h * (HEAD + MLA_ROPE))


def _rope_tables(s):
    def tables(dim):
        inv = 1.0 / (ROPE_THETA ** (jnp.arange(0, dim, 2, dtype=F32) / dim))
        ang = jnp.arange(s, dtype=F32)[:, None] * inv[None, :]
        return jnp.cos(ang), jnp.sin(ang)

    def build(cos, sin, lead, trail_one, trail_zero):
        z = jnp.zeros_like(sin)
        ones = lambda n: jnp.ones((s, n), F32)
        zeros = lambda n: jnp.zeros((s, n), F32)
        c = jnp.concatenate([ones(lead), cos, cos, ones(trail_one), zeros(trail_zero)], axis=1)
        s1 = jnp.concatenate([zeros(lead), -sin, z, zeros(trail_one + trail_zero)], axis=1)
        s2 = jnp.concatenate([zeros(lead), z, sin, zeros(trail_one + trail_zero)], axis=1)
        return c, s1, s2

    cm, sm = tables(MLA_ROPE)
    cp, sp = tables(PARTIAL_ROPE)
    return {"mla_q": build(cm, sm, HEAD, 0, HEAD - MLA_ROPE),
            "mla_k": build(cm, sm, 0, 0, LANES - MLA_ROPE),
            "dil": build(cp, sp, 0, HEAD - PARTIAL_ROPE, 0)}


def _ffn_forward(tag, x, g, wg, wu, wd):
    h = _rms_fwd(tag + "_norm", x, g)
    a, b, z = _ffn_up(tag + "_up", h, wg, wu)
    y = _mm(tag + "_down", z, wd, res=x, alpha=0.5, out_dtype=F32)
    return y, (x, h, a, b, z)


def _ffn_backward(tag, dx, dxb, saved, g, wg, wu, wd):
    x, h, a, b, z = saved
    da, db = _ffn_bwd_mid(tag + "_bwd_mid", dxb, wd, a, b, 0.5)
    g_wd = _mm(tag + "_dwd", z, dxb, ta=True, alpha=0.5, out_dtype=F32)
    g_wg = _mm(tag + "_dwg", h, da, ta=True, out_dtype=F32)
    g_wu = _mm(tag + "_dwu", h, db, ta=True, out_dtype=F32)
    dh = _mm(tag + "_dh_gate", da, wg, tb=True, out_dtype=F32)
    dh = _mm(tag + "_dh_up", db, wu, tb=True, res=dh, out_dtype=F32)
    dx, dxb, dg = _rms_bwd(tag + "_norm_bwd", dh, x, g, res=dx)
    return dx, dxb, dg[0], g_wg, g_wu, g_wd


def _mix_forward(dm, tabs, x, lw, s):
    d, gw = dm.d, dm.gw
    h = _rms_fwd("mix_norm", x, lw["mix_norm"])
    pm = _mm("mix_in_main", h, lw["w_in_main"])
    tail = _mm("mix_in_tail", h, lw["w_in_tail"], out_dtype=F32)
    cum = _gate_fwd("fox_gate", tail, lw["gate_bias"], lw["gate_mask"])
    cum_h = cum[:, FORGET_LANE:FORGET_LANE + dm.fox_h].T
    ccol, crow = cum_h[:, :, None], cum_h[:, None, :]
    fox_ops = (pm, ccol, crow)
    out_a, lse_a = _attn_fwd("fox_fwd", "fox", dm.fox_h, s, fox_ops, HEAD ** -0.5)

    cqn = _rms_fwd("mla_q_norm", pm, lw["mla_q_norm"], width=gw, col=dm.col["cq"])
    ckvn = _rms_fwd("mla_kv_norm", pm, lw["mla_kv_norm"], width=gw, col=dm.col["ckv"])
    qb_raw = _mm("mla_uq", cqn, lw["mla_w_uq"])
    qb = _rope("mla_q_rope", qb_raw, tabs["mla_q"], MLA_ROPE // 2, width=2 * HEAD, nblk=dm.mla_h)
    kv = _mm("mla_ukv", ckvn, lw["mla_w_ukv"])
    kr = _rope("mla_k_rope", tail, tabs["mla_k"], MLA_ROPE // 2, width=LANES)
    mla_ops = (qb, kv, kr)
    out_b, lse_b = _attn_fwd("mla_fwd", "mla", dm.mla_h, s, mla_ops, (HEAD + MLA_ROPE) ** -0.5)

    dqr = _rope("dil_q_rope", pm, tabs["dil"], PARTIAL_ROPE // 2, width=HEAD, col=dm.col["dq"], nblk=dm.dil_h)
    dkr = _rope("dil_k_rope", pm, tabs["dil"], PARTIAL_ROPE // 2, width=HEAD, col=dm.col["dk"], nblk=dm.dil_h)
    dil_ops = (dqr, dkr, pm, dm.col["dv"] // HEAD)
    out_c, lse_c = _attn_fwd("dil_fwd", "dil", dm.dil_h, s, dil_ops, HEAD ** -0.5)

    mixed = jnp.concatenate([out_a, out_b, out_c], axis=1)
    y = _mm("mix_out", mixed, lw["w_out"], res=x, out_dtype=F32)
    saved = (x, h, pm, tail, fox_ops, lse_a, cqn, ckvn, mla_ops, lse_b, dil_ops, lse_c, mixed)
    return y, saved


def _mix_backward(dm, tabs, dx, dxb, saved, lw, s):
    x, h, pm, tail, fox_ops, lse_a, cqn, ckvn, mla_ops, lse_b, dil_ops, lse_c, mixed = saved
    gw = dm.gw
    grads = {}
    grads["w_out"] = _mm("mix_dwout", mixed, dxb, ta=True, out_dtype=F32)
    dmixed = _mm("mix_dmixed", dxb, lw["w_out"], tb=True)
    nha, nhb, nhc = dm.fox_h, dm.mla_h, dm.dil_h

    sa = HEAD ** -0.5
    dfq, dcum_q = _attn_bwd_q("fox_bwd_q", "fox", nha, s, fox_ops, sa, mixed, dmixed, 0, lse_a)
    dfk, dfv, dcum = _attn_bwd_kv("fox_bwd_kv", "fox", nha, s, fox_ops, sa, mixed, dmixed, 0, lse_a)

    sb = (HEAD + MLA_ROPE) ** -0.5
    dqb = _attn_bwd_q("mla_bwd_q", "mla", nhb, s, mla_ops, sb, mixed, dmixed, nha, lse_b)
    dkv, dkr = _attn_bwd_kv("mla_bwd_kv", "mla", nhb, s, mla_ops, sb, mixed, dmixed, nha, lse_b)
    dqb_raw = _rope("mla_q_rope_bwd", dqb, tabs["mla_q"], MLA_ROPE // 2, width=2 * HEAD, nblk=nhb, transpose=True)
    grads["mla_w_uq"] = _mm("mla_dwuq", cqn, dqb_raw, ta=True, out_dtype=F32)
    dcqn = _mm("mla_dcqn", dqb_raw, lw["mla_w_uq"], tb=True)
    grads["mla_w_ukv"] = _mm("mla_dwukv", ckvn, dkv, ta=True, out_dtype=F32)
    dckvn = _mm("mla_dckvn", dkv, lw["mla_w_ukv"], tb=True)
    dcq, dg_q = _rms_bwd("mla_q_norm_bwd", dcqn, pm, lw["mla_q_norm"], width=gw, col=dm.col["cq"])
    dckv, dg_kv = _rms_bwd("mla_kv_norm_bwd", dckvn, pm, lw["mla_kv_norm"], width=gw, col=dm.col["ckv"])
    dkr_raw = _rope("mla_k_rope_bwd", dkr, tabs["mla_k"], MLA_ROPE // 2, width=LANES, transpose=True, out_dtype=F32)

    sc = HEAD ** -0.5
    ddqr = _attn_bwd_q("dil_bwd_q", "dil", nhc, s, dil_ops, sc, mixed, dmixed, nha + nhb, lse_c)
    ddkr, ddv = _attn_bwd_kv("dil_bwd_kv", "dil", nhc, s, dil_ops, sc, mixed, dmixed, nha + nhb, lse_c)
    ddq = _rope("dil_q_rope_bwd", ddqr, tabs["dil"], PARTIAL_ROPE // 2, width=HEAD, nblk=nhc, transpose=True)
    ddk = _rope("dil_k_rope_bwd", ddkr, tabs["dil"], PARTIAL_ROPE // 2, width=HEAD, nblk=nhc, transpose=True)

    dcum_lanes = jnp.pad((dcum[:, 0, :] + dcum_q[:, :, 0]).T, ((0, 0), (FORGET_LANE, LANES - FORGET_LANE - nha)))
    dgate, dbias = _gate_bwd("fox_gate_bwd", dcum_lanes, tail, lw["gate_bias"], lw["gate_mask"])
    dtail = (dgate + dkr_raw).astype(BF16)
    dpm = jnp.concatenate([dfq, dfk, dfv, dcq, dckv, ddq, ddk, ddv], axis=1)

    g_main = _mm("mix_dwin_main", h, dpm, ta=True, out_dtype=F32)
    g_tail = _mm("mix_dwin_tail", h, dtail, ta=True, out_dtype=F32)
    grads["w_in"] = jnp.concatenate([g_main, g_tail], axis=1)
    dh = _mm("mix_dh_main", dpm, lw["w_in_main"], tb=True, out_dtype=F32)
    dh = _mm("mix_dh_tail", dtail, lw["w_in_tail"], tb=True, res=dh, out_dtype=F32)
    dx, dxb, dg = _rms_bwd("mix_norm_bwd", dh, x, lw["mix_norm"], res=dx)
    grads["mix_norm"] = dg[0]
    grads["mla_q_norm"] = dg_q[0]
    grads["mla_kv_norm"] = dg_kv[0]
    grads["fox_forget_bias"] = dbias[0, FORGET_LANE:FORGET_LANE + nha]
    return dx, dxb, grads


def kernel(x, ffn1_norm, ffn1_w_gate, ffn1_w_up, ffn1_w_down, mix_norm, w_in, fox_forget_bias, mla_q_norm, mla_kv_norm, mla_w_uq, mla_w_ukv, w_out, ffn2_norm, ffn2_w_gate, ffn2_w_up, ffn2_w_down, final_norm, loss_target, m_ffn1_norm, m_ffn1_w_gate, m_ffn1_w_up, m_ffn1_w_down, m_mix_norm, m_w_in, m_fox_forget_bias, m_mla_q_norm, m_mla_kv_norm, m_mla_w_uq, m_mla_w_ukv, m_w_out, m_ffn2_norm, m_ffn2_w_gate, m_ffn2_w_up, m_ffn2_w_down, m_final_norm, v_ffn1_norm, v_ffn1_w_gate, v_ffn1_w_up, v_ffn1_w_down, v_mix_norm, v_w_in, v_fox_forget_bias, v_mla_q_norm, v_mla_kv_norm, v_mla_w_uq, v_mla_w_ukv, v_w_out, v_ffn2_norm, v_ffn2_w_gate, v_ffn2_w_up, v_ffn2_w_down, v_final_norm):
    p = dict(locals())
    s, d = x.shape[1], x.shape[2]
    depth = ffn1_norm.shape[0]
    dm = _Dims(d, 4 * w_in.shape[2])
    tabs = _rope_tables(s)
    c_arr = lax.axis_index("c").astype(jnp.int32).reshape(1)

    full = {}
    for n in BIG:
        stacked = _allgather_weight("ag_" + n, p[n].astype(BF16))
        if n in ROW_SHARDED:
            full[n] = [stacked[:, l].reshape(-1, stacked.shape[-1]) for l in range(depth)]
        else:
            full[n] = [jnp.transpose(stacked[:, l], (1, 0, 2)).reshape(stacked.shape[2], -1) for l in range(depth)]
    gate_lanes = ((0, 0), (FORGET_LANE, LANES - FORGET_LANE - dm.fox_h))
    gate_mask = jnp.pad(jnp.ones((1, dm.fox_h), F32), gate_lanes)
    layers = []
    for l in range(depth):
        w_in_al = dm.align_w_in(full["w_in"][l])
        lw = {n: full[n][l] for n in BIG}
        lw["w_in_main"] = w_in_al[:, :dm.main_w]
        lw["w_in_tail"] = w_in_al[:, dm.main_w:]
        lw["mla_w_uq"] = dm.pad_w_uq(full["mla_w_uq"][l])
        for n in ("ffn1_norm", "mix_norm", "mla_q_norm", "mla_kv_norm", "ffn2_norm"):
            lw[n] = p[n][l][None, :]
        lw["gate_bias"] = jnp.pad(fox_forget_bias[l][None, :], gate_lanes)
        lw["gate_mask"] = gate_mask
        layers.append(lw)

    xs = x[0]
    saved = []
    for l, lw in enumerate(layers):
        xs, s1 = _ffn_forward("ffn1", xs, lw["ffn1_norm"], lw["ffn1_w_gate"], lw["ffn1_w_up"], lw["ffn1_w_down"])
        xs, s2 = _mix_forward(dm, tabs, xs, lw, s)
        xs, s3 = _ffn_forward("ffn2", xs, lw["ffn2_norm"], lw["ffn2_w_gate"], lw["ffn2_w_up"], lw["ffn2_w_down"])
        saved.append((s1, s2, s3))
    dx, dxb, dg_final, loss_blk = _loss_head("loss_head", xs, final_norm[None, :], loss_target[0])

    g_layers = [None] * depth
    for l in reversed(range(depth)):
        lw = layers[l]
        s1, s2, s3 = saved[l]
        gl = {}
        dx, dxb, gl["ffn2_norm"], gl["ffn2_w_gate"], gl["ffn2_w_up"], gl["ffn2_w_down"] = _ffn_backward(
            "ffn2", dx, dxb, s3, lw["ffn2_norm"], lw["ffn2_w_gate"], lw["ffn2_w_up"], lw["ffn2_w_down"])
        dx, dxb, gm = _mix_backward(dm, tabs, dx, dxb, s2, lw, s)
        gl.update(gm)
        gl["w_in"] = dm.unalign_w_in(gl["w_in"])
        gl["mla_w_uq"] = dm.unpad_w_uq(gl["mla_w_uq"])
        dx, dxb, gl["ffn1_norm"], gl["ffn1_w_gate"], gl["ffn1_w_up"], gl["ffn1_w_down"] = _ffn_backward(
            "ffn1", dx, dxb, s1, lw["ffn1_norm"], lw["ffn1_w_gate"], lw["ffn1_w_up"], lw["ffn1_w_down"])
        g_layers[l] = gl
    grad_x = dx[None]

    grads, deltas, new_m, new_v = {}, {}, {}, {}
    hl = depth // 2
    for n in BIG:
        g_nat = jnp.stack([g_layers[l][n] for l in range(depth)])
        recv = _sibling_halves("rs_sibling_" + n, g_nat)
        pair = _add_half("rs_pair_" + n, g_nat, recv, c_arr)
        rr, cc = pair.shape[1], pair.shape[2]
        if n in ROW_SHARDED:
            pst = jnp.transpose(pair.reshape(hl, 4, rr // 4, cc), (1, 0, 2, 3))
        else:
            pst = jnp.transpose(pair.reshape(hl, rr, 4, cc // 4), (2, 0, 1, 3))
        got = _chip_exchange("rs_chips_" + n, pst)
        mine = _sum_slots("rs_sum_" + n, got, (3, 0, 1, 2))
        grads[n] = _swap_halves("rs_swap_" + n, mine)
        deltas[n], new_m[n], new_v[n] = _adamw("adamw_" + n, p[n], grads[n], p["m_" + n], p["v_" + n])

    def pack(get):
        flat = lambda n: jnp.pad(get(n).reshape(1, -1), ((0, 0), (0, d - get(n).size)))
        rows = [get("ffn1_norm"), get("mix_norm"), get("ffn2_norm"), get("final_norm")[None, :],
                flat("mla_q_norm"), flat("mla_kv_norm"), flat("fox_forget_bias")]
        return jnp.concatenate(rows, axis=0)

    def unpack(blk):
        o = 3 * depth
        return {"ffn1_norm": blk[0:depth], "mix_norm": blk[depth:2 * depth], "ffn2_norm": blk[2 * depth:o],
                "final_norm": blk[o], "mla_q_norm": blk[o + 1, :depth * dm.rank].reshape(depth, dm.rank),
                "mla_kv_norm": blk[o + 2, :depth * dm.rank].reshape(depth, dm.rank),
                "fox_forget_bias": blk[o + 3, :depth * dm.fox_h].reshape(depth, dm.fox_h)}

    assert depth * dm.rank <= d
    local_small = {n: (dg_final[0] if n == "final_norm" else jnp.stack([g_layers[l][n] for l in range(depth)]))
                   for n in SMALL}
    n_rows = 3 * depth + 4
    pad_rows = -(n_rows + 1) % 8
    blk = jnp.concatenate([pack(lambda n: local_small[n]), jnp.broadcast_to(loss_blk[0:1, 0:1], (1, d)),
                           jnp.zeros((pad_rows, d), F32)], axis=0)
    everyone = _gather_all("small_gather", blk)[:, None]
    total = _sum_slots("small_sum", everyone, tuple(range(8)))[0]
    loss = total[n_rows, 0]
    small_g = unpack(total)
    w_blk = jnp.concatenate([pack(lambda n: p[n]), jnp.zeros((pad_rows + 1, d), F32)], axis=0)
    m_blk = jnp.concatenate([pack(lambda n: p["m_" + n]), jnp.zeros((pad_rows + 1, d), F32)], axis=0)
    v_blk = jnp.concatenate([pack(lambda n: p["v_" + n]), jnp.zeros((pad_rows + 1, d), F32)], axis=0)
    g_blk = jnp.concatenate([total[:n_rows], jnp.zeros((pad_rows + 1, d), F32)], axis=0)
    d_blk, nm_blk, nv_blk = _adamw("adamw_small", w_blk[None], g_blk[None], m_blk[None], v_blk[None])
    small_d, small_m, small_v = unpack(d_blk[0]), unpack(nm_blk[0]), unpack(nv_blk[0])
    for n in SMALL:
        grads[n], deltas[n], new_m[n], new_v[n] = small_g[n], small_d[n], small_m[n], small_v[n]

    return (loss, grad_x, *[grads[n] for n in WEIGHTS], *[deltas[n] for n in WEIGHTS],
            *[new_m[n] for n in WEIGHTS], *[new_v[n] for n in WEIGHTS])
```

```python
import functools

import jax
import jax.numpy as jnp
from jax import lax
from jax.experimental import pallas as pl
from jax.experimental.pallas import tpu as pltpu

F32 = jnp.float32
BF16 = jnp.bfloat16
MESH = pl.DeviceIdType.MESH
ANY = pl.BlockSpec(memory_space=pl.ANY)

LANES = 128
VMEM_LIMIT_BYTES = 56 * 2 ** 20
HEAD = 128
EPS = 1e-6
NEG = -1e30
ATTN_TILE = 512
ROPE_THETA = 500000.0
MLA_ROPE = 64
PARTIAL_ROPE = HEAD // 4
DIL_BRANCHES = ((128, 1), (512, 4), (2048, 16))
DIL_REACH = max(w for w, _ in DIL_BRANCHES)
FORGET_LANE = MLA_ROPE
ADAM_LR, ADAM_B1, ADAM_B2, ADAM_EPS, ADAM_WD, ADAM_STEP = 0.001, 0.9, 0.999, 1e-08, 0.01, 10

BIG = ("ffn1_w_gate", "ffn1_w_up", "ffn1_w_down", "w_in", "mla_w_uq", "mla_w_ukv", "w_out",
       "ffn2_w_gate", "ffn2_w_up", "ffn2_w_down")
ROW_SHARDED = ("ffn1_w_down", "w_out", "ffn2_w_down")
SMALL = ("ffn1_norm", "mix_norm", "fox_forget_bias", "mla_q_norm", "mla_kv_norm", "ffn2_norm", "final_norm")
WEIGHTS = ("ffn1_norm", "ffn1_w_gate", "ffn1_w_up", "ffn1_w_down", "mix_norm", "w_in", "fox_forget_bias",
           "mla_q_norm", "mla_kv_norm", "mla_w_uq", "mla_w_ukv", "w_out", "ffn2_norm", "ffn2_w_gate",
           "ffn2_w_up", "ffn2_w_down", "final_norm")


def _params(*sem):
    return pltpu.CompilerParams(dimension_semantics=sem, vmem_limit_bytes=VMEM_LIMIT_BYTES)


def _div_tile(n, cap):
    if n <= cap:
        return n
    best = 0
    for t in range(LANES, cap + 1, LANES):
        if n % t == 0:
            best = t
    assert best, (n, cap)
    return best


MM_TILE_BUDGET_BYTES = 34 * 2 ** 20


def _mm_tiles(m, n, k, pairs, out_bytes, has_res):
    def cands(x, cap):
        return [x] if x <= LANES else [t for t in range(LANES, min(x, cap) + 1, LANES) if x % t == 0]

    best = None
    for tm in cands(m, 1024):
        for tn in cands(n, 1408):
            for tk in cands(k, 2048):
                need = (2 * 2 * pairs * (tm * tk + tk * tn) + 4 * tm * tn + 2 * out_bytes * tm * tn
                        + (2 * 4 * tm * tn if has_res else 0))
                if need > MM_TILE_BUDGET_BYTES:
                    continue
                key = (-(1.0 / tm + 1.0 / tn), tk)
                if best is None or key > best[0]:
                    best = (key, (tm, tn, tk))
    assert best is not None, (m, n, k)
    return best[1]


def _mm(name, a, b, *, ta=False, tb=False, a2=None, b2=None, res=None, alpha=1.0, out_dtype=BF16):
    m, k = (a.shape[1], a.shape[0]) if ta else a.shape
    n, kb = (b.shape[0], b.shape[1]) if tb else (b.shape[1], b.shape[0])
    assert k == kb, (name, a.shape, b.shape)
    pairs = 1 if a2 is None else 2
    if pairs == 2:
        assert a2.shape == a.shape and b2.shape == b.shape
    tm, tn, tk = _mm_tiles(m, n, k, pairs, jnp.dtype(out_dtype).itemsize, res is not None)
    nk = k // tk
    a_spec = (pl.BlockSpec((tk, tm), lambda i, j, kk: (kk, i)) if ta
              else pl.BlockSpec((tm, tk), lambda i, j, kk: (i, kk)))
    b_spec = (pl.BlockSpec((tn, tk), lambda i, j, kk: (j, kk)) if tb
              else pl.BlockSpec((tk, tn), lambda i, j, kk: (kk, j)))
    dims = (((0 if ta else 1,), (1 if tb else 0,)), ((), ()))
    operands, specs = [a, b], [a_spec, b_spec]
    if pairs == 2:
        operands += [a2, b2]
        specs += [a_spec, b_spec]
    n_mm = len(operands)
    if res is not None:
        operands.append(res)
        specs.append(pl.BlockSpec((tm, tn), lambda i, j, kk: (i, j)))

    def body(*refs):
        o_ref, acc_ref = refs[-2], refs[-1]
        kk = pl.program_id(2)

        @pl.when(kk == 0)
        def _():
            acc_ref[...] = jnp.zeros_like(acc_ref)

        acc_ref[...] += lax.dot_general(refs[0][...], refs[1][...], dims, preferred_element_type=F32)
        if pairs == 2:
            acc_ref[...] += lax.dot_general(refs[2][...], refs[3][...], dims, preferred_element_type=F32)

        @pl.when(kk == nk - 1)
        def _():
            out = acc_ref[...] * alpha
            if res is not None:
                out = out + refs[n_mm][...].astype(F32)
            o_ref[...] = out.astype(o_ref.dtype)

    return pl.pallas_call(
        body, name=name, grid=(m // tm, n // tn, nk), in_specs=specs,
        out_specs=pl.BlockSpec((tm, tn), lambda i, j, kk: (i, j)),
        out_shape=jax.ShapeDtypeStruct((m, n), out_dtype),
        scratch_shapes=[pltpu.VMEM((tm, tn), F32)],
        compiler_params=_params("parallel", "parallel", "arbitrary"))(*operands)


def _ffn_up(name, h, wg, wu):
    s, d = h.shape
    f = wg.shape[1]
    tm, tn = _div_tile(s, 512), _div_tile(f, 512)

    def body(h_ref, wg_ref, wu_ref, a_ref, b_ref, z_ref):
        hv = h_ref[...]
        a = jnp.dot(hv, wg_ref[...], preferred_element_type=F32)
        b = jnp.dot(hv, wu_ref[...], preferred_element_type=F32)
        a_ref[...] = a.astype(BF16)
        b_ref[...] = b.astype(BF16)
        z_ref[...] = (a * jax.nn.sigmoid(a) * b).astype(BF16)

    tile = pl.BlockSpec((tm, tn), lambda i, j: (i, j))
    w_spec = pl.BlockSpec((d, tn), lambda i, j: (0, j))
    return pl.pallas_call(
        body, name=name, grid=(s // tm, f // tn),
        in_specs=[pl.BlockSpec((tm, d), lambda i, j: (i, 0)), w_spec, w_spec],
        out_specs=[tile, tile, tile],
        out_shape=[jax.ShapeDtypeStruct((s, f), BF16)] * 3,
        compiler_params=_params("parallel", "parallel"))(h, wg, wu)


def _ffn_bwd_mid(name, dy, wd, a, b, alpha):
    s, d = dy.shape
    f = wd.shape[0]
    tm, tn = _div_tile(s, 512), _div_tile(f, 512)

    def body(dy_ref, wd_ref, a_ref, b_ref, da_ref, db_ref):
        dz = lax.dot_general(dy_ref[...], wd_ref[...], (((1,), (1,)), ((), ())),
                             preferred_element_type=F32) * alpha
        av = a_ref[...].astype(F32)
        sg = jax.nn.sigmoid(av)
        db_ref[...] = (dz * av * sg).astype(BF16)
        da_ref[...] = (dz * b_ref[...].astype(F32) * (sg * (1.0 + av * (1.0 - sg)))).astype(BF16)

    tile = pl.BlockSpec((tm, tn), lambda i, j: (i, j))
    return pl.pallas_call(
        body, name=name, grid=(s // tm, f // tn),
        in_specs=[pl.BlockSpec((tm, d), lambda i, j: (i, 0)), pl.BlockSpec((tn, d), lambda i, j: (j, 0)),
                  tile, tile],
        out_specs=[tile, tile],
        out_shape=[jax.ShapeDtypeStruct((s, f), BF16)] * 2,
        compiler_params=_params("parallel", "parallel"))(dy, wd, a, b)


def _rms_fwd(name, x, g, *, width=None, col=0):
    s = x.shape[0]
    w = x.shape[1] if width is None else width
    assert col % w == 0
    cb, tr = col // w, min(256, s)

    def body(x_ref, g_ref, h_ref):
        xf = x_ref[...].astype(F32)
        r = lax.rsqrt(jnp.mean(xf * xf, axis=-1, keepdims=True) + EPS)
        h_ref[...] = (xf * r * g_ref[...]).astype(BF16)

    return pl.pallas_call(
        body, name=name, grid=(s // tr,),
        in_specs=[pl.BlockSpec((tr, w), lambda i: (i, cb)), pl.BlockSpec((1, w), lambda i: (0, 0))],
        out_specs=pl.BlockSpec((tr, w), lambda i: (i, 0)),
        out_shape=jax.ShapeDtypeStruct((s, w), BF16),
        compiler_params=_params("parallel"))(x, g)


def _rms_bwd(name, dh, x, g, *, res=None, width=None, col=0):
    s = x.shape[0]
    w = x.shape[1] if width is None else width
    assert col % w == 0
    cb, tr = col // w, min(256, s)
    has_res = res is not None

    def body(*refs):
        dh_ref, x_ref, g_ref = refs[:3]
        dg_ref = refs[-1]
        i = pl.program_id(0)
        xf = x_ref[...].astype(F32)
        r = lax.rsqrt(jnp.mean(xf * xf, axis=-1, keepdims=True) + EPS)
        xh = xf * r
        d = dh_ref[...].astype(F32)
        dxh = d * g_ref[...]
        dx = r * (dxh - xh * jnp.mean(dxh * xh, axis=-1, keepdims=True))
        if has_res:
            dx = dx + refs[3][...]
            refs[4][...] = dx
            refs[5][...] = dx.astype(BF16)
        else:
            refs[3][...] = dx.astype(BF16)

        @pl.when(i == 0)
        def _():
            dg_ref[...] = jnp.zeros_like(dg_ref)

        dg_ref[0:1, :] += jnp.sum(d * xh, axis=0, keepdims=True)

    row = pl.BlockSpec((tr, w), lambda i: (i, 0))
    in_specs = [row, pl.BlockSpec((tr, w), lambda i: (i, cb)), pl.BlockSpec((1, w), lambda i: (0, 0))]
    operands = [dh, x, g]
    dg_spec = pl.BlockSpec((8, w), lambda i: (0, 0))
    dg_shape = jax.ShapeDtypeStruct((8, w), F32)
    if has_res:
        in_specs.append(row)
        operands.append(res)
        out_specs = [row, row, dg_spec]
        out_shape = [jax.ShapeDtypeStruct((s, w), F32), jax.ShapeDtypeStruct((s, w), BF16), dg_shape]
    else:
        out_specs = [row, dg_spec]
        out_shape = [jax.ShapeDtypeStruct((s, w), BF16), dg_shape]
    return pl.pallas_call(
        body, name=name, grid=(s // tr,), in_specs=in_specs, out_specs=out_specs, out_shape=out_shape,
        compiler_params=_params("arbitrary"))(*operands)


def _loss_head(name, x, g, target):
    s, d = x.shape
    tr = min(256, s)
    n = s // tr

    def body(x_ref, g_ref, t_ref, dx_ref, dxb_ref, dg_ref, loss_ref, sq_ref):
        i = pl.program_id(0)
        xf = x_ref[...]
        r = lax.rsqrt(jnp.mean(xf * xf, axis=-1, keepdims=True) + EPS)
        xh = xf * r
        gv = g_ref[...]
        err = xh * gv - t_ref[...]
        dy = err * (1.0 / d)
        dxh = dy * gv
        dx = r * (dxh - xh * jnp.mean(dxh * xh, axis=-1, keepdims=True))
        dx_ref[...] = dx
        dxb_ref[...] = dx.astype(BF16)

        @pl.when(i == 0)
        def _():
            dg_ref[...] = jnp.zeros_like(dg_ref)
            sq_ref[...] = jnp.zeros_like(sq_ref)

        dg_ref[0:1, :] += jnp.sum(dy * xh, axis=0, keepdims=True)
        sq_ref[...] += jnp.sum(err * err, axis=0, keepdims=True)

        @pl.when(i == n - 1)
        def _():
            total = jnp.sum(sq_ref[...], axis=1, keepdims=True) * (0.5 / d)
            loss_ref[...] = jnp.broadcast_to(total, loss_ref.shape)

    row = pl.BlockSpec((tr, d), lambda i: (i, 0))
    return pl.pallas_call(
        body, name=name, grid=(n,),
        in_specs=[row, pl.BlockSpec((1, d), lambda i: (0, 0)), row],
        out_specs=[row, row, pl.BlockSpec((8, d), lambda i: (0, 0)), pl.BlockSpec((8, LANES), lambda i: (0, 0))],
        out_shape=[jax.ShapeDtypeStruct((s, d), F32), jax.ShapeDtypeStruct((s, d), BF16),
                   jax.ShapeDtypeStruct((8, d), F32), jax.ShapeDtypeStruct((8, LANES), F32)],
        scratch_shapes=[pltpu.VMEM((1, d), F32)],
        compiler_params=_params("arbitrary"))(x, g, target)


def _rope(name, x, tabs, half, *, width, col=0, nblk=1, transpose=False, out_dtype=BF16):
    s = x.shape[0]
    assert col % width == 0
    cb0, tr = col // width, min(256, s)

    def body(x_ref, c_ref, s1_ref, s2_ref, y_ref):
        xf = x_ref[...].astype(F32)
        if transpose:
            y = (xf * c_ref[...] + pltpu.roll(xf * s1_ref[...], half, 1)
                 + pltpu.roll(xf * s2_ref[...], width - half, 1))
        else:
            y = (xf * c_ref[...] + pltpu.roll(xf, width - half, 1) * s1_ref[...]
                 + pltpu.roll(xf, half, 1) * s2_ref[...])
        y_ref[...] = y.astype(y_ref.dtype)

    tab = pl.BlockSpec((tr, width), lambda i, j: (i, 0))
    return pl.pallas_call(
        body, name=name, grid=(s // tr, nblk),
        in_specs=[pl.BlockSpec((tr, width), lambda i, j: (i, cb0 + j)), tab, tab, tab],
        out_specs=pl.BlockSpec((tr, width), lambda i, j: (i, j)),
        out_shape=jax.ShapeDtypeStruct((s, nblk * width), out_dtype),
        compiler_params=_params("parallel", "parallel"))(x, *tabs)


def _split3(v):
    hi = v.astype(BF16)
    r1 = v - hi.astype(F32)
    mid = r1.astype(BF16)
    lo = (r1 - mid.astype(F32)).astype(BF16)
    return hi, mid, lo


def _tri_matmul(tri, v):
    hi, mid, lo = _split3(v)
    return (jnp.dot(tri, hi, preferred_element_type=F32) + jnp.dot(tri, mid, preferred_element_type=F32)
            + jnp.dot(tri, lo, preferred_element_type=F32))


def _log_sigmoid(v):
    return jnp.minimum(v, 0.0) - jnp.log(1.0 + jnp.exp(-jnp.abs(v)))


def _gate_fwd(name, tail, bias, mask):
    s = tail.shape[0]
    tr = min(512, s)

    def body(t_ref, b_ref, m_ref, cum_ref, carry_ref):
        i = pl.program_id(0)

        @pl.when(i == 0)
        def _():
            carry_ref[...] = jnp.zeros_like(carry_ref)

        lf = _log_sigmoid(t_ref[...] + b_ref[...]) * m_ref[...]
        r = lax.broadcasted_iota(jnp.int32, (tr, tr), 0)
        c = lax.broadcasted_iota(jnp.int32, (tr, tr), 1)
        cum = _tri_matmul((r >= c).astype(BF16), lf) + carry_ref[...]
        cum_ref[...] = cum
        carry_ref[...] = cum[tr - 1:tr, :]

    vec = pl.BlockSpec((1, LANES), lambda i: (0, 0))
    return pl.pallas_call(
        body, name=name, grid=(s // tr,),
        in_specs=[pl.BlockSpec((tr, LANES), lambda i: (i, 0)), vec, vec],
        out_specs=pl.BlockSpec((tr, LANES), lambda i: (i, 0)),
        out_shape=jax.ShapeDtypeStruct((s, LANES), F32),
        scratch_shapes=[pltpu.VMEM((1, LANES), F32)],
        compiler_params=_params("arbitrary"))(tail, bias, mask)


def _gate_bwd(name, dcum, tail, bias, mask):
    s = tail.shape[0]
    tr = min(512, s)
    n = s // tr

    def body(dc_ref, t_ref, b_ref, m_ref, dt_ref, db_ref, carry_ref):
        i = pl.program_id(0)

        @pl.when(i == 0)
        def _():
            carry_ref[...] = jnp.zeros_like(carry_ref)
            db_ref[...] = jnp.zeros_like(db_ref)

        r = lax.broadcasted_iota(jnp.int32, (tr, tr), 0)
        c = lax.broadcasted_iota(jnp.int32, (tr, tr), 1)
        dlf = _tri_matmul((r <= c).astype(BF16), dc_ref[...]) + carry_ref[...]
        carry_ref[...] = dlf[0:1, :]
        v = t_ref[...] + b_ref[...]
        dt = dlf * jax.nn.sigmoid(-v) * m_ref[...]
        dt_ref[...] = dt
        db_ref[0:1, :] += jnp.sum(dt, axis=0, keepdims=True)

    vec = pl.BlockSpec((1, LANES), lambda i: (0, 0))
    rev = pl.BlockSpec((tr, LANES), lambda i: (n - 1 - i, 0))
    return pl.pallas_call(
        body, name=name, grid=(n,),
        in_specs=[rev, rev, vec, vec],
        out_specs=[rev, pl.BlockSpec((8, LANES), lambda i: (0, 0))],
        out_shape=[jax.ShapeDtypeStruct((s, LANES), F32), jax.ShapeDtypeStruct((8, LANES), F32)],
        scratch_shapes=[pltpu.VMEM((1, LANES), F32)],
        compiler_params=_params("arbitrary"))(dcum, tail, bias, mask)


NT = (((1,), (1,)), ((), ()))
TN = (((0,), (0,)), ((), ()))


def _weights_and_scores(mode, scale, t, qi, kb, q1, k1, q2, k2, cq, ck, masked=True):
    sc = lax.dot_general(q1, k1, NT, preferred_element_type=F32)
    if q2 is not None:
        sc = sc + lax.dot_general(q2, k2, NT, preferred_element_type=F32)
    sc = sc * scale
    if cq is not None:
        sc = sc + (cq - ck)
    if not masked:
        return sc, None
    row = lax.broadcasted_iota(jnp.int32, (t, t), 0)
    col = lax.broadcasted_iota(jnp.int32, (t, t), 1)
    dist = row - col + (qi - kb) * t
    back = dist >= 0
    if mode == "dil":
        w = jnp.zeros((t, t), F32)
        for window, dil in DIL_BRANCHES:
            w = w + (back & (dist <= window) & ((dist & (dil - 1)) == 0)).astype(F32)
    else:
        w = back.astype(F32)
    return jnp.where(w > 0.0, sc, NEG), w


def _causal_steps(mode, valid, diagonal, step):
    if mode == "dil":
        pl.when(valid)(functools.partial(step, True))
    else:
        pl.when(valid & jnp.logical_not(diagonal))(functools.partial(step, False))
        pl.when(diagonal)(functools.partial(step, True))


def _attn_geometry(mode, s):
    t = min(ATTN_TILE, s)
    nq = s // t
    nsteps = min(nq, DIL_REACH // t + 1) if mode == "dil" else nq
    return t, nq, nsteps


def _attn_operands(mode, nh, t, ops, hf, qf, kf):
    def cols(width, base, rows):
        return pl.BlockSpec((t, width), lambda *g: (rows(*g), base + hf(*g)))

    if mode == "fox":
        pm, ccol, crow = ops
        return ([pm, pm, pm, ccol, crow],
                [cols(HEAD, 0, qf), cols(HEAD, nh, kf), cols(HEAD, 2 * nh, kf),
                 pl.BlockSpec((None, t, 1), lambda *g: (hf(*g), qf(*g), 0)),
                 pl.BlockSpec((None, 1, t), lambda *g: (hf(*g), 0, kf(*g)))])
    if mode == "mla":
        qb, kv, kr = ops
        return ([qb, kv, kr],
                [cols(2 * HEAD, 0, qf), cols(2 * HEAD, 0, kf), pl.BlockSpec((t, HEAD), lambda *g: (kf(*g), 0))])
    q, k, pm, vbase = ops
    return [q, k, pm], [cols(HEAD, 0, qf), cols(HEAD, 0, kf), cols(HEAD, vbase, kf)]


def _unpack(mode, refs):
    if mode == "fox":
        return refs[0][...], refs[1][...], refs[2][...], None, None, refs[3][...], refs[4][...]
    if mode == "mla":
        q, kv = refs[0][...], refs[1][...]
        return q[:, :HEAD], kv[:, :HEAD], kv[:, HEAD:], q[:, HEAD:], refs[2][...], None, None
    return refs[0][...], refs[1][...], refs[2][...], None, None, None, None


def _attn_fwd(name, mode, nh, s, ops, scale):
    t, nq, nsteps = _attn_geometry(mode, s)
    hf = lambda h, i, kk: h
    qf = lambda h, i, kk: i
    kf = lambda h, i, kk: jnp.maximum(i - (nsteps - 1) + kk, 0)
    operands, specs = _attn_operands(mode, nh, t, ops, hf, qf, kf)
    n_in = len(operands)

    def body(*refs):
        o_ref, lse_ref, m_ref, l_ref, acc_ref = refs[n_in:]
        i, kk = pl.program_id(1), pl.program_id(2)
        kb = i - (nsteps - 1) + kk

        @pl.when(kk == 0)
        def _():
            m_ref[...] = jnp.full_like(m_ref, NEG)
            l_ref[...] = jnp.zeros_like(l_ref)
            acc_ref[...] = jnp.zeros_like(acc_ref)

        def step(masked):
            q1, k1, v, q2, k2, cq, ck = _unpack(mode, refs)
            sc, w = _weights_and_scores(mode, scale, t, i, kb, q1, k1, q2, k2, cq, ck, masked)
            m_old = m_ref[...]
            m_new = jnp.maximum(m_old, jnp.max(sc, axis=1, keepdims=True))
            p = jnp.exp(sc - m_new)
            if w is not None:
                p = w * p
            a = jnp.exp(m_old - m_new)
            l_ref[...] = a * l_ref[...] + jnp.sum(p, axis=1, keepdims=True)
            acc_ref[...] = a * acc_ref[...] + jnp.dot(p.astype(BF16), v, preferred_element_type=F32)
            m_ref[...] = m_new

        _causal_steps(mode, kb >= 0, kb == i, step)

        @pl.when(kk == nsteps - 1)
        def _():
            o_ref[...] = (acc_ref[...] / l_ref[...]).astype(BF16)
            lse_ref[...] = m_ref[...] + jnp.log(l_ref[...])

    return pl.pallas_call(
        body, name=name, grid=(nh, nq, nsteps), in_specs=specs,
        out_specs=[pl.BlockSpec((t, HEAD), lambda h, i, kk: (i, h)),
                   pl.BlockSpec((None, t, 1), lambda h, i, kk: (h, i, 0))],
        out_shape=[jax.ShapeDtypeStruct((s, nh * HEAD), BF16), jax.ShapeDtypeStruct((nh, s, 1), F32)],
        scratch_shapes=[pltpu.VMEM((t, 1), F32), pltpu.VMEM((t, 1), F32), pltpu.VMEM((t, HEAD), F32)],
        compiler_params=_params("parallel", "parallel", "arbitrary"))(*operands)


def _attn_bwd_q(name, mode, nh, s, ops, scale, o, do, do_base, lse):
    t, nq, nsteps = _attn_geometry(mode, s)
    hf = lambda h, i, kk: h
    qf = lambda h, i, kk: i
    kf = lambda h, i, kk: jnp.maximum(i - (nsteps - 1) + kk, 0)
    operands, specs = _attn_operands(mode, nh, t, ops, hf, qf, kf)
    operands += [o, do, lse]
    specs += [pl.BlockSpec((t, HEAD), lambda h, i, kk: (i, do_base + h)),
              pl.BlockSpec((t, HEAD), lambda h, i, kk: (i, do_base + h)),
              pl.BlockSpec((None, t, 1), lambda h, i, kk: (h, i, 0))]
    n_in = len(operands)
    wq = 2 * HEAD if mode == "mla" else HEAD

    fox = mode == "fox"

    def body(*refs):
        o_ref, do_ref, lse_ref = refs[n_in - 3:n_in]
        dq_ref = refs[n_in]
        acc_ref, delta_ref, rows_ref = refs[-3:]
        i, kk = pl.program_id(1), pl.program_id(2)
        kb = i - (nsteps - 1) + kk

        @pl.when(kk == 0)
        def _():
            acc_ref[...] = jnp.zeros_like(acc_ref)
            rows_ref[...] = jnp.zeros_like(rows_ref)
            delta_ref[...] = jnp.sum(o_ref[...].astype(F32) * do_ref[...].astype(F32), axis=1, keepdims=True)

        def step(masked):
            q1, k1, v, q2, k2, cq, ck = _unpack(mode, refs)
            sc, w = _weights_and_scores(mode, scale, t, i, kb, q1, k1, q2, k2, cq, ck, masked)
            p = jnp.exp(sc - lse_ref[...])
            if w is not None:
                p = w * p
            dp = lax.dot_general(do_ref[...], v, NT, preferred_element_type=F32)
            ds32 = p * (dp - delta_ref[...])
            ds = ds32.astype(BF16)
            if fox:
                rows_ref[...] += jnp.sum(ds32, axis=1, keepdims=True)
            if mode == "mla":
                acc_ref[:, :HEAD] += jnp.dot(ds, k1, preferred_element_type=F32)
                acc_ref[:, HEAD:] += jnp.dot(ds, k2, preferred_element_type=F32)
            else:
                acc_ref[...] += jnp.dot(ds, k1, preferred_element_type=F32)

        _causal_steps(mode, kb >= 0, kb == i, step)

        @pl.when(kk == nsteps - 1)
        def _():
            dq_ref[...] = (acc_ref[...] * scale).astype(BF16)
            if fox:
                refs[n_in + 1][...] = rows_ref[...]

    out_specs = [pl.BlockSpec((t, wq), lambda h, i, kk: (i, h))]
    out_shape = [jax.ShapeDtypeStruct((s, nh * wq), BF16)]
    if fox:
        out_specs.append(pl.BlockSpec((None, t, 1), lambda h, i, kk: (h, i, 0)))
        out_shape.append(jax.ShapeDtypeStruct((nh, s, 1), F32))
    out = pl.pallas_call(
        body, name=name, grid=(nh, nq, nsteps), in_specs=specs, out_specs=out_specs, out_shape=out_shape,
        scratch_shapes=[pltpu.VMEM((t, wq), F32), pltpu.VMEM((t, 1), F32), pltpu.VMEM((t, 1), F32)],
        compiler_params=_params("parallel", "parallel", "arbitrary"))(*operands)
    return out if fox else out[0]


def _attn_bwd_kv(name, mode, nh, s, ops, scale, o, do, do_base, lse):
    t, nq, nsteps = _attn_geometry(mode, s)
    hf = lambda j, h, qq: h
    qf = lambda j, h, qq: jnp.minimum(j + qq, nq - 1)
    kf = lambda j, h, qq: j
    operands, specs = _attn_operands(mode, nh, t, ops, hf, qf, kf)
    operands += [o, do, lse]
    specs += [pl.BlockSpec((t, HEAD), lambda j, h, qq: (qf(j, h, qq), do_base + h)),
              pl.BlockSpec((t, HEAD), lambda j, h, qq: (qf(j, h, qq), do_base + h)),
              pl.BlockSpec((None, t, 1), lambda j, h, qq: (h, qf(j, h, qq), 0))]
    n_in = len(operands)
    head_tile = pl.BlockSpec((t, HEAD), lambda j, h, qq: (j, h))
    if mode == "fox":
        out_specs = [head_tile, head_tile, pl.BlockSpec((None, 1, t), lambda j, h, qq: (h, 0, j))]
        out_shape = [jax.ShapeDtypeStruct((s, nh * HEAD), BF16)] * 2 + [jax.ShapeDtypeStruct((nh, 1, s), F32)]
        scratch = [pltpu.VMEM((t, HEAD), F32), pltpu.VMEM((t, HEAD), F32), pltpu.VMEM((1, t), F32)]
    elif mode == "mla":
        out_specs = [pl.BlockSpec((t, 2 * HEAD), lambda j, h, qq: (j, h)),
                     pl.BlockSpec((t, HEAD), lambda j, h, qq: (j, 0))]
        out_shape = [jax.ShapeDtypeStruct((s, nh * 2 * HEAD), BF16), jax.ShapeDtypeStruct((s, HEAD), F32)]
        scratch = [pltpu.VMEM((t, HEAD), F32), pltpu.VMEM((t, HEAD), F32), pltpu.VMEM((t, HEAD), F32)]
    else:
        out_specs = [head_tile, head_tile]
        out_shape = [jax.ShapeDtypeStruct((s, nh * HEAD), BF16)] * 2
        scratch = [pltpu.VMEM((t, HEAD), F32), pltpu.VMEM((t, HEAD), F32)]
    n_out = len(out_specs)

    def body(*refs):
        o_ref, do_ref, lse_ref = refs[n_in - 3:n_in]
        outs = refs[n_in:n_in + n_out]
        dk_acc, dv_acc = refs[n_in + n_out], refs[n_in + n_out + 1]
        extra = refs[n_in + n_out + 2] if mode != "dil" else None
        j, h, qq = pl.program_id(0), pl.program_id(1), pl.program_id(2)
        qb = j + qq

        @pl.when(qq == 0)
        def _():
            dk_acc[...] = jnp.zeros_like(dk_acc)
            dv_acc[...] = jnp.zeros_like(dv_acc)
            if mode == "fox":
                extra[...] = jnp.zeros_like(extra)

        if mode == "mla":
            @pl.when((qq == 0) & (h == 0))
            def _():
                extra[...] = jnp.zeros_like(extra)

        def step(masked):
            q1, k1, v, q2, k2, cq, ck = _unpack(mode, refs)
            sc, w = _weights_and_scores(mode, scale, t, qb, j, q1, k1, q2, k2, cq, ck, masked)
            p = jnp.exp(sc - lse_ref[...])
            if w is not None:
                p = w * p
            dov = do_ref[...]
            delta = jnp.sum(o_ref[...].astype(F32) * dov.astype(F32), axis=1, keepdims=True)
            dv_acc[...] += lax.dot_general(p.astype(BF16), dov, TN, preferred_element_type=F32)
            dp = lax.dot_general(dov, v, NT, preferred_element_type=F32)
            ds = p * (dp - delta)
            dsb = ds.astype(BF16)
            dk_acc[...] += lax.dot_general(dsb, q1, TN, preferred_element_type=F32)
            if mode == "mla":
                extra[...] += lax.dot_general(dsb, q2, TN, preferred_element_type=F32)
            if mode == "fox":
                extra[...] -= jnp.sum(ds, axis=0, keepdims=True)

        _causal_steps(mode, qb < nq, qq == 0, step)

        @pl.when(qq == nsteps - 1)
        def _():
            if mode == "mla":
                outs[0][:, :HEAD] = (dk_acc[...] * scale).astype(BF16)
                outs[0][:, HEAD:] = dv_acc[...].astype(BF16)
            else:
                outs[0][...] = (dk_acc[...] * scale).astype(BF16)
                outs[1][...] = dv_acc[...].astype(BF16)
            if mode == "fox":
                outs[2][...] = extra[...]

        if mode == "mla":
            @pl.when((qq == nsteps - 1) & (h == nh - 1))
            def _():
                outs[1][...] = extra[...] * scale

    return pl.pallas_call(
        body, name=name, grid=(nq, nh, nsteps), in_specs=specs, out_specs=out_specs, out_shape=out_shape,
        scratch_shapes=scratch,
        compiler_params=_params("parallel", "arbitrary", "arbitrary"))(*operands)


def _adamw(name, w, g, m, v):
    nl, r, c = w.shape
    tr = _div_tile(r, 128) if r % 128 == 0 else r

    def body(w_ref, g_ref, m_ref, v_ref, d_ref, nm_ref, nv_ref):
        gv = g_ref[...]
        nm = ADAM_B1 * m_ref[...] + (1.0 - ADAM_B1) * gv
        nv = ADAM_B2 * v_ref[...] + (1.0 - ADAM_B2) * jnp.square(gv)
        m_hat = nm / (1.0 - ADAM_B1 ** ADAM_STEP)
        v_hat = nv / (1.0 - ADAM_B2 ** ADAM_STEP)
        d_ref[...] = -ADAM_LR * (m_hat / (jnp.sqrt(v_hat) + ADAM_EPS) + ADAM_WD * w_ref[...])
        nm_ref[...] = nm
        nv_ref[...] = nv

    blk = pl.BlockSpec((None, tr, c), lambda l, i: (l, i, 0))
    return pl.pallas_call(
        body, name=name, grid=(nl, r // tr), in_specs=[blk] * 4, out_specs=[blk] * 3,
        out_shape=[jax.ShapeDtypeStruct(w.shape, F32)] * 3,
        compiler_params=_params("parallel", "parallel"))(w, g, m, v)


def _add_half(name, g, recv, c_arr):
    nl, r, c = g.shape
    hl = nl // 2
    tr = _div_tile(r, 128) if r % 128 == 0 else r

    def body(c_ref, g_ref, r_ref, o_ref):
        o_ref[...] = (g_ref[...] + r_ref[...]).astype(BF16)

    grid_spec = pltpu.PrefetchScalarGridSpec(
        num_scalar_prefetch=1, grid=(hl, r // tr),
        in_specs=[pl.BlockSpec((None, tr, c), lambda l, i, c_ref: (c_ref[0] * hl + l, i, 0)),
                  pl.BlockSpec((None, tr, c), lambda l, i, c_ref: (l, i, 0))],
        out_specs=pl.BlockSpec((None, tr, c), lambda l, i, c_ref: (l, i, 0)))
    return pl.pallas_call(
        body, name=name, grid_spec=grid_spec, out_shape=jax.ShapeDtypeStruct((hl, r, c), BF16),
        compiler_params=_params("parallel", "parallel"))(c_arr, g, recv)


def _sum_chips(name, pst, got, chip_arr):
    _, nl, r, c = pst.shape
    tr = _div_tile(r, 128) if r % 128 == 0 else r

    def body(chip_ref, own_ref, g0_ref, g1_ref, g2_ref, o_ref):
        o_ref[...] = ((own_ref[...].astype(F32) + g0_ref[...].astype(F32)) + g1_ref[...].astype(F32)
                      ) + g2_ref[...].astype(F32)

    def slot(k):
        return pl.BlockSpec((None, None, tr, c), lambda l, i, chip_ref: (k, l, i, 0))

    grid_spec = pltpu.PrefetchScalarGridSpec(
        num_scalar_prefetch=1, grid=(nl, r // tr),
        in_specs=[pl.BlockSpec((None, None, tr, c), lambda l, i, chip_ref: (chip_ref[0], l, i, 0)),
                  slot(0), slot(1), slot(2)],
        out_specs=pl.BlockSpec((None, tr, c), lambda l, i, chip_ref: (l, i, 0)))
    return pl.pallas_call(
        body, name=name, grid_spec=grid_spec, out_shape=jax.ShapeDtypeStruct((nl, r, c), F32),
        compiler_params=_params("parallel", "parallel"))(chip_arr, pst, got, got, got)


def _adamw_halves(name, w, mine, other, m, v, c_arr):
    nl, r, c = w.shape
    hl = nl // 2
    tr = _div_tile(r, 128) if r % 128 == 0 else r

    def body(c_ref, w_ref, mine_ref, other_ref, m_ref, v_ref, g_ref, d_ref, nm_ref, nv_ref):
        is_mine = (pl.program_id(0) // hl) == c_ref[0]
        gv = jnp.where(is_mine, mine_ref[...], other_ref[...])
        nm = ADAM_B1 * m_ref[...] + (1.0 - ADAM_B1) * gv
        nv = ADAM_B2 * v_ref[...] + (1.0 - ADAM_B2) * jnp.square(gv)
        m_hat = nm / (1.0 - ADAM_B1 ** ADAM_STEP)
        v_hat = nv / (1.0 - ADAM_B2 ** ADAM_STEP)
        g_ref[...] = gv
        d_ref[...] = -ADAM_LR * (m_hat / (jnp.sqrt(v_hat) + ADAM_EPS) + ADAM_WD * w_ref[...])
        nm_ref[...] = nm
        nv_ref[...] = nv

    blk = pl.BlockSpec((None, tr, c), lambda l, i, c_ref: (l, i, 0))
    half = pl.BlockSpec((None, tr, c), lambda l, i, c_ref: (l % hl, i, 0))
    grid_spec = pltpu.PrefetchScalarGridSpec(
        num_scalar_prefetch=1, grid=(nl, r // tr), in_specs=[blk, half, half, blk, blk], out_specs=[blk] * 4)
    return pl.pallas_call(
        body, name=name, grid_spec=grid_spec, out_shape=[jax.ShapeDtypeStruct(w.shape, F32)] * 4,
        compiler_params=_params("parallel", "parallel"))(c_arr, w, mine, other, m, v)


def _sum_slots(name, rc, order, out_dtype=F32):
    _, nl, r, c = rc.shape
    tr = _div_tile(r, 128) if r % 128 == 0 else r

    def body(*refs):
        acc = refs[0][...].astype(F32)
        for ref in refs[1:-1]:
            acc = acc + ref[...].astype(F32)
        refs[-1][...] = acc.astype(out_dtype)

    def slot(k):
        return pl.BlockSpec((None, None, tr, c), lambda l, i: (k, l, i, 0))

    return pl.pallas_call(
        body, name=name, grid=(nl, r // tr), in_specs=[slot(k) for k in order],
        out_specs=pl.BlockSpec((None, tr, c), lambda l, i: (l, i, 0)),
        out_shape=jax.ShapeDtypeStruct((nl, r, c), out_dtype),
        compiler_params=_params("parallel", "parallel"))(*([rc] * len(order)))


def _exchange(name, ins, out_shapes, plan, n_remote, n_local):
    n_in, n_out = len(ins), len(out_shapes)

    def body(*refs):
        in_refs, out_refs = refs[:n_in], refs[n_in:n_in + n_out]
        send_sems, recv_sems, local_sems = refs[n_in + n_out:]
        x, y, c = lax.axis_index("x"), lax.axis_index("y"), lax.axis_index("c")
        remote, local = plan(in_refs, out_refs, x, y, c)
        assert len(remote) == n_remote and len(local) == n_local
        copies = []
        for k, (src, dst, (fx, fy, fc)) in enumerate(remote):
            peer = (1 - x if fx else x, 1 - y if fy else y, 1 - c if fc else c)
            copies.append(pltpu.make_async_remote_copy(
                src_ref=src, dst_ref=dst, send_sem=send_sems.at[k], recv_sem=recv_sems.at[k],
                device_id=peer, device_id_type=MESH))
        own = [pltpu.make_async_copy(src, dst, local_sems.at[k]) for k, (src, dst) in enumerate(local)]
        for cp in copies + own:
            cp.start()
        for cp in copies:
            cp.wait_recv()
        for cp in copies:
            cp.wait_send()
        for cp in own:
            cp.wait()

    return pl.pallas_call(
        body, name=name, in_specs=[ANY] * n_in, out_specs=[ANY] * n_out, out_shape=list(out_shapes),
        scratch_shapes=[pltpu.SemaphoreType.DMA((n_remote,)), pltpu.SemaphoreType.DMA((n_remote,)),
                        pltpu.SemaphoreType.DMA((max(n_local, 1),))])(*ins)


CHIP_FLIPS = ((1, 0), (0, 1), (1, 1))


def _allgather_weight(name, w):
    nl = w.shape[0]
    hl = nl // 2

    def body(w_ref, out_ref, send_sems, recv_sems):
        x, y, c = lax.axis_index("x"), lax.axis_index("y"), lax.axis_index("c")
        sibling = (x, y, 1 - c)
        chips = [(1 - x if fx else x, 1 - y if fy else y) for fx, fy in CHIP_FLIPS]

        def half(chip, core):
            return out_ref.at[2 * chip[0] + chip[1], pl.ds(core * hl, hl)]

        def copy(k, src, dst, to):
            return pltpu.make_async_remote_copy(src_ref=src, dst_ref=dst, send_sem=send_sems.at[k],
                                                recv_sem=recv_sems.at[k], device_id=to, device_id_type=MESH)

        own = copy(6, w_ref, out_ref.at[2 * x + y], sibling)
        first = [copy(j, w_ref.at[pl.ds(c * hl, hl)], half((x, y), c), (*chip, c)) for j, chip in enumerate(chips)]
        for cp in first + [own]:
            cp.start()
        passed = [copy(3 + j, half(chip, c), half(chip, c), sibling) for j, chip in enumerate(chips)]
        for j, chip in enumerate(chips):
            copy(j, half(chip, c), half(chip, c), (*chip, c)).wait_recv()
            passed[j].start()
        for j, chip in enumerate(chips):
            copy(3 + j, half(chip, 1 - c), half(chip, 1 - c), sibling).wait_recv()
        own.wait_recv()
        for cp in first + passed + [own]:
            cp.wait_send()

    return pl.pallas_call(
        body, name=name, in_specs=[ANY], out_specs=ANY,
        out_shape=jax.ShapeDtypeStruct((4,) + w.shape, w.dtype),
        scratch_shapes=[pltpu.SemaphoreType.DMA((7,)), pltpu.SemaphoreType.DMA((7,))])(w)


def _sibling_halves(name, g):
    hl = g.shape[0] // 2

    def plan(ins, outs, x, y, c):
        return [(ins[0].at[pl.ds((1 - c) * hl, hl)], outs[0], (0, 0, 1))], []

    return _exchange(name, [g], [jax.ShapeDtypeStruct((hl,) + g.shape[1:], g.dtype)], plan, 1, 0)[0]


def _chip_exchange(name, pst):
    def plan(ins, outs, x, y, c):
        remote = []
        for k, (fx, fy) in enumerate(CHIP_FLIPS):
            px, py = (1 - x if fx else x), (1 - y if fy else y)
            remote.append((ins[0].at[2 * px + py], outs[0].at[k], (fx, fy, 0)))
        return remote, []

    return _exchange(name, [pst], [jax.ShapeDtypeStruct((3,) + pst.shape[1:], pst.dtype)], plan, 3, 0)[0]


def _swap_halves(name, s):
    def plan(ins, outs, x, y, c):
        return [(ins[0], outs[0], (0, 0, 1))], []

    return _exchange(name, [s], [jax.ShapeDtypeStruct(s.shape, s.dtype)], plan, 1, 0)[0]


def _gather_all(name, blk):
    def plan(ins, outs, x, y, c):
        mine = outs[0].at[4 * x + 2 * y + c]
        remote = [(ins[0], mine, (m >> 2 & 1, m >> 1 & 1, m & 1)) for m in range(1, 8)]
        return remote, [(ins[0], mine)]

    return _exchange(name, [blk], [jax.ShapeDtypeStruct((8,) + blk.shape, blk.dtype)], plan, 7, 1)[0]


class _Dims:
    def __init__(self, d, in_w):
        self.d = d
        self.fox_h = (d // 4) // HEAD
        self.mla_h = (d // 2) // HEAD
        self.dil_h = (d // 4) // HEAD
        self.gw = d // 4
        self.rank = d // 4
        gw, fh = self.gw, self.fox_h
        sizes = (gw, gw, gw, fh, self.rank, self.rank, MLA_ROPE, gw, gw, gw)
        assert sum(sizes) == in_w
        offs = [0]
        for z in sizes:
            offs.append(offs[-1] + z)
        self.nat = dict(zip(("fq", "fk", "fv", "fl", "cq", "ckv", "kr", "dq", "dk", "dv"), zip(offs[:-1], sizes)))
        self.main_order = ("fq", "fk", "fv", "cq", "ckv", "dq", "dk", "dv")
        self.main_w = 8 * gw
        self.col = {n: i * gw for i, n in enumerate(self.main_order)}

    def align_w_in(self, w):
        parts = [w[:, self.nat[n][0]:self.nat[n][0] + self.nat[n][1]] for n in self.main_order]
        parts += [w[:, self.nat["kr"][0]:self.nat["kr"][0] + MLA_ROPE],
                  w[:, self.nat["fl"][0]:self.nat["fl"][0] + self.fox_h],
                  jnp.zeros((w.shape[0], LANES - MLA_ROPE - self.fox_h), w.dtype)]
        return jnp.concatenate(parts, axis=1)

    def unalign_w_in(self, g):
        gw, t0 = self.gw, self.main_w
        src = {n: g[:, self.col[n]:self.col[n] + gw] for n in self.main_order}
        src["kr"] = g[:, t0:t0 + MLA_ROPE]
        src["fl"] = g[:, t0 + FORGET_LANE:t0 + FORGET_LANE + self.fox_h]
        return jnp.concatenate([src[n] for n in ("fq", "fk", "fv", "fl", "cq", "ckv", "kr", "dq", "dk", "dv")], axis=1)

    def pad_w_uq(self, w):
        r = w.shape[0]
        w3 = w.reshape(r, self.mla_h, HEAD + MLA_ROPE)
        return jnp.pad(w3, ((0, 0), (0, 0), (0, HEAD - MLA_ROPE))).reshape(r, self.mla_h * 2 * HEAD)

    def unpad_w_uq(self, g):
        r = g.shape[0]
        return g.reshape(r, self.mla_h, 2 * HEAD)[:, :, :HEAD + MLA_ROPE].reshape(r, self.mla_h * (HEAD + MLA_ROPE))


def _rope_tables(s):
    def tables(dim):
        inv = 1.0 / (ROPE_THETA ** (jnp.arange(0, dim, 2, dtype=F32) / dim))
        ang = jnp.arange(s, dtype=F32)[:, None] * inv[None, :]
        return jnp.cos(ang), jnp.sin(ang)

    def build(cos, sin, lead, trail_one, trail_zero):
        z = jnp.zeros_like(sin)
        ones = lambda n: jnp.ones((s, n), F32)
        zeros = lambda n: jnp.zeros((s, n), F32)
        c = jnp.concatenate([ones(lead), cos, cos, ones(trail_one), zeros(trail_zero)], axis=1)
        s1 = jnp.concatenate([zeros(lead), -sin, z, zeros(trail_one + trail_zero)], axis=1)
        s2 = jnp.concatenate([zeros(lead), z, sin, zeros(trail_one + trail_zero)], axis=1)
        return c, s1, s2

    cm, sm = tables(MLA_ROPE)
    cp, sp = tables(PARTIAL_ROPE)
    return {"mla_q": build(cm, sm, HEAD, 0, HEAD - MLA_ROPE),
            "mla_k": build(cm, sm, 0, 0, LANES - MLA_ROPE),
            "dil": build(cp, sp, 0, HEAD - PARTIAL_ROPE, 0)}


def _ffn_forward(tag, x, g, wg, wu, wd):
    h = _rms_fwd(tag + "_norm", x, g)
    a, b, z = _ffn_up(tag + "_up", h, wg, wu)
    y = _mm(tag + "_down", z, wd, res=x, alpha=0.5, out_dtype=F32)
    return y, (x, h, a, b, z)


def _ffn_backward(tag, dx, dxb, saved, g, wg, wu, wd):
    x, h, a, b, z = saved
    da, db = _ffn_bwd_mid(tag + "_bwd_mid", dxb, wd, a, b, 0.5)
    g_wd = _mm(tag + "_dwd", z, dxb, ta=True, alpha=0.5, out_dtype=F32)
    g_wg = _mm(tag + "_dwg", h, da, ta=True, out_dtype=F32)
    g_wu = _mm(tag + "_dwu", h, db, ta=True, out_dtype=F32)
    dh = _mm(tag + "_dh", da, wg, tb=True, a2=db, b2=wu, out_dtype=F32)
    dx, dxb, dg = _rms_bwd(tag + "_norm_bwd", dh, x, g, res=dx)
    return dx, dxb, dg[0], g_wg, g_wu, g_wd


def _mix_forward(dm, tabs, x, lw, s):
    d, gw = dm.d, dm.gw
    h = _rms_fwd("mix_norm", x, lw["mix_norm"])
    pm = _mm("mix_in_main", h, lw["w_in_main"])
    tail = _mm("mix_in_tail", h, lw["w_in_tail"], out_dtype=F32)
    cum = _gate_fwd("fox_gate", tail, lw["gate_bias"], lw["gate_mask"])
    cum_h = cum[:, FORGET_LANE:FORGET_LANE + dm.fox_h].T
    ccol, crow = cum_h[:, :, None], cum_h[:, None, :]
    fox_ops = (pm, ccol, crow)
    out_a, lse_a = _attn_fwd("fox_fwd", "fox", dm.fox_h, s, fox_ops, HEAD ** -0.5)

    cqn = _rms_fwd("mla_q_norm", pm, lw["mla_q_norm"], width=gw, col=dm.col["cq"])
    ckvn = _rms_fwd("mla_kv_norm", pm, lw["mla_kv_norm"], width=gw, col=dm.col["ckv"])
    qb_raw = _mm("mla_uq", cqn, lw["mla_w_uq"])
    qb = _rope("mla_q_rope", qb_raw, tabs["mla_q"], MLA_ROPE // 2, width=2 * HEAD, nblk=dm.mla_h)
    kv = _mm("mla_ukv", ckvn, lw["mla_w_ukv"])
    kr = _rope("mla_k_rope", tail, tabs["mla_k"], MLA_ROPE // 2, width=LANES)
    mla_ops = (qb, kv, kr)
    out_b, lse_b = _attn_fwd("mla_fwd", "mla", dm.mla_h, s, mla_ops, (HEAD + MLA_ROPE) ** -0.5)

    dqr = _rope("dil_q_rope", pm, tabs["dil"], PARTIAL_ROPE // 2, width=HEAD, col=dm.col["dq"], nblk=dm.dil_h)
    dkr = _rope("dil_k_rope", pm, tabs["dil"], PARTIAL_ROPE // 2, width=HEAD, col=dm.col["dk"], nblk=dm.dil_h)
    dil_ops = (dqr, dkr, pm, dm.col["dv"] // HEAD)
    out_c, lse_c = _attn_fwd("dil_fwd", "dil", dm.dil_h, s, dil_ops, HEAD ** -0.5)

    mixed = jnp.concatenate([out_a, out_b, out_c], axis=1)
    y = _mm("mix_out", mixed, lw["w_out"], res=x, out_dtype=F32)
    saved = (x, h, pm, tail, fox_ops, lse_a, cqn, ckvn, mla_ops, lse_b, dil_ops, lse_c, mixed)
    return y, saved


def _mix_backward(dm, tabs, dx, dxb, saved, lw, s):
    x, h, pm, tail, fox_ops, lse_a, cqn, ckvn, mla_ops, lse_b, dil_ops, lse_c, mixed = saved
    gw = dm.gw
    grads = {}
    grads["w_out"] = _mm("mix_dwout", mixed, dxb, ta=True, out_dtype=F32)
    dmixed = _mm("mix_dmixed", dxb, lw["w_out"], tb=True)
    nha, nhb, nhc = dm.fox_h, dm.mla_h, dm.dil_h

    sa = HEAD ** -0.5
    dfq, dcum_q = _attn_bwd_q("fox_bwd_q", "fox", nha, s, fox_ops, sa, mixed, dmixed, 0, lse_a)
    dfk, dfv, dcum = _attn_bwd_kv("fox_bwd_kv", "fox", nha, s, fox_ops, sa, mixed, dmixed, 0, lse_a)

    sb = (HEAD + MLA_ROPE) ** -0.5
    dqb = _attn_bwd_q("mla_bwd_q", "mla", nhb, s, mla_ops, sb, mixed, dmixed, nha, lse_b)
    dkv, dkr = _attn_bwd_kv("mla_bwd_kv", "mla", nhb, s, mla_ops, sb, mixed, dmixed, nha, lse_b)
    dqb_raw = _rope("mla_q_rope_bwd", dqb, tabs["mla_q"], MLA_ROPE // 2, width=2 * HEAD, nblk=nhb, transpose=True)
    grads["mla_w_uq"] = _mm("mla_dwuq", cqn, dqb_raw, ta=True, out_dtype=F32)
    dcqn = _mm("mla_dcqn", dqb_raw, lw["mla_w_uq"], tb=True)
    grads["mla_w_ukv"] = _mm("mla_dwukv", ckvn, dkv, ta=True, out_dtype=F32)
    dckvn = _mm("mla_dckvn", dkv, lw["mla_w_ukv"], tb=True)
    dcq, dg_q = _rms_bwd("mla_q_norm_bwd", dcqn, pm, lw["mla_q_norm"], width=gw, col=dm.col["cq"])
    dckv, dg_kv = _rms_bwd("mla_kv_norm_bwd", dckvn, pm, lw["mla_kv_norm"], width=gw, col=dm.col["ckv"])
    dkr_raw = _rope("mla_k_rope_bwd", dkr, tabs["mla_k"], MLA_ROPE // 2, width=LANES, transpose=True, out_dtype=F32)

    sc = HEAD ** -0.5
    ddqr = _attn_bwd_q("dil_bwd_q", "dil", nhc, s, dil_ops, sc, mixed, dmixed, nha + nhb, lse_c)
    ddkr, ddv = _attn_bwd_kv("dil_bwd_kv", "dil", nhc, s, dil_ops, sc, mixed, dmixed, nha + nhb, lse_c)
    ddq = _rope("dil_q_rope_bwd", ddqr, tabs["dil"], PARTIAL_ROPE // 2, width=HEAD, nblk=nhc, transpose=True)
    ddk = _rope("dil_k_rope_bwd", ddkr, tabs["dil"], PARTIAL_ROPE // 2, width=HEAD, nblk=nhc, transpose=True)

    dcum_lanes = jnp.pad((dcum[:, 0, :] + dcum_q[:, :, 0]).T, ((0, 0), (FORGET_LANE, LANES - FORGET_LANE - nha)))
    dgate, dbias = _gate_bwd("fox_gate_bwd", dcum_lanes, tail, lw["gate_bias"], lw["gate_mask"])
    dtail = (dgate + dkr_raw).astype(BF16)
    dpm = jnp.concatenate([dfq, dfk, dfv, dcq, dckv, ddq, ddk, ddv], axis=1)

    g_main = _mm("mix_dwin_main", h, dpm, ta=True, out_dtype=F32)
    g_tail = _mm("mix_dwin_tail", h, dtail, ta=True, out_dtype=F32)
    grads["w_in"] = jnp.concatenate([g_main, g_tail], axis=1)
    dh = _mm("mix_dh_main", dpm, lw["w_in_main"], tb=True, out_dtype=F32)
    dh = _mm("mix_dh_tail", dtail, lw["w_in_tail"], tb=True, res=dh, out_dtype=F32)
    dx, dxb, dg = _rms_bwd("mix_norm_bwd", dh, x, lw["mix_norm"], res=dx)
    grads["mix_norm"] = dg[0]
    grads["mla_q_norm"] = dg_q[0]
    grads["mla_kv_norm"] = dg_kv[0]
    grads["fox_forget_bias"] = dbias[0, FORGET_LANE:FORGET_LANE + nha]
    return dx, dxb, grads


def kernel(x, ffn1_norm, ffn1_w_gate, ffn1_w_up, ffn1_w_down, mix_norm, w_in, fox_forget_bias, mla_q_norm, mla_kv_norm, mla_w_uq, mla_w_ukv, w_out, ffn2_norm, ffn2_w_gate, ffn2_w_up, ffn2_w_down, final_norm, loss_target, m_ffn1_norm, m_ffn1_w_gate, m_ffn1_w_up, m_ffn1_w_down, m_mix_norm, m_w_in, m_fox_forget_bias, m_mla_q_norm, m_mla_kv_norm, m_mla_w_uq, m_mla_w_ukv, m_w_out, m_ffn2_norm, m_ffn2_w_gate, m_ffn2_w_up, m_ffn2_w_down, m_final_norm, v_ffn1_norm, v_ffn1_w_gate, v_ffn1_w_up, v_ffn1_w_down, v_mix_norm, v_w_in, v_fox_forget_bias, v_mla_q_norm, v_mla_kv_norm, v_mla_w_uq, v_mla_w_ukv, v_w_out, v_ffn2_norm, v_ffn2_w_gate, v_ffn2_w_up, v_ffn2_w_down, v_final_norm):
    p = dict(locals())
    s, d = x.shape[1], x.shape[2]
    depth = ffn1_norm.shape[0]
    dm = _Dims(d, 4 * w_in.shape[2])
    tabs = _rope_tables(s)
    c_arr = lax.axis_index("c").astype(jnp.int32).reshape(1)
    chip_arr = (2 * lax.axis_index("x") + lax.axis_index("y")).astype(jnp.int32).reshape(1)

    full = {}
    for n in BIG:
        stacked = _allgather_weight("ag_" + n, p[n].astype(BF16))
        if n in ROW_SHARDED:
            full[n] = [stacked[:, l].reshape(-1, stacked.shape[-1]) for l in range(depth)]
        else:
            full[n] = [jnp.transpose(stacked[:, l], (1, 0, 2)).reshape(stacked.shape[2], -1) for l in range(depth)]
    gate_lanes = ((0, 0), (FORGET_LANE, LANES - FORGET_LANE - dm.fox_h))
    gate_mask = jnp.pad(jnp.ones((1, dm.fox_h), F32), gate_lanes)
    layers = []
    for l in range(depth):
        w_in_al = dm.align_w_in(full["w_in"][l])
        lw = {n: full[n][l] for n in BIG}
        lw["w_in_main"] = w_in_al[:, :dm.main_w]
        lw["w_in_tail"] = w_in_al[:, dm.main_w:]
        lw["mla_w_uq"] = dm.pad_w_uq(full["mla_w_uq"][l])
        for n in ("ffn1_norm", "mix_norm", "mla_q_norm", "mla_kv_norm", "ffn2_norm"):
            lw[n] = p[n][l][None, :]
        lw["gate_bias"] = jnp.pad(fox_forget_bias[l][None, :], gate_lanes)
        lw["gate_mask"] = gate_mask
        layers.append(lw)

    xs = x[0]
    saved = []
    for l, lw in enumerate(layers):
        xs, s1 = _ffn_forward("ffn1", xs, lw["ffn1_norm"], lw["ffn1_w_gate"], lw["ffn1_w_up"], lw["ffn1_w_down"])
        xs, s2 = _mix_forward(dm, tabs, xs, lw, s)
        xs, s3 = _ffn_forward("ffn2", xs, lw["ffn2_norm"], lw["ffn2_w_gate"], lw["ffn2_w_up"], lw["ffn2_w_down"])
        saved.append((s1, s2, s3))
    dx, dxb, dg_final, loss_blk = _loss_head("loss_head", xs, final_norm[None, :], loss_target[0])

    g_layers = [None] * depth
    for l in reversed(range(depth)):
        lw = layers[l]
        s1, s2, s3 = saved[l]
        gl = {}
        dx, dxb, gl["ffn2_norm"], gl["ffn2_w_gate"], gl["ffn2_w_up"], gl["ffn2_w_down"] = _ffn_backward(
            "ffn2", dx, dxb, s3, lw["ffn2_norm"], lw["ffn2_w_gate"], lw["ffn2_w_up"], lw["ffn2_w_down"])
        dx, dxb, gm = _mix_backward(dm, tabs, dx, dxb, s2, lw, s)
        gl.update(gm)
        gl["w_in"] = dm.unalign_w_in(gl["w_in"])
        gl["mla_w_uq"] = dm.unpad_w_uq(gl["mla_w_uq"])
        dx, dxb, gl["ffn1_norm"], gl["ffn1_w_gate"], gl["ffn1_w_up"], gl["ffn1_w_down"] = _ffn_backward(
            "ffn1", dx, dxb, s1, lw["ffn1_norm"], lw["ffn1_w_gate"], lw["ffn1_w_up"], lw["ffn1_w_down"])
        g_layers[l] = gl
    grad_x = dx[None]

    grads, deltas, new_m, new_v = {}, {}, {}, {}
    hl = depth // 2
    for n in BIG:
        g_nat = jnp.stack([g_layers[l][n] for l in range(depth)])
        recv = _sibling_halves("rs_sibling_" + n, g_nat)
        pair = _add_half("rs_pair_" + n, g_nat, recv, c_arr)
        rr, cc = pair.shape[1], pair.shape[2]
        if n in ROW_SHARDED:
            pst = jnp.transpose(pair.reshape(hl, 4, rr // 4, cc), (1, 0, 2, 3))
        else:
            pst = jnp.transpose(pair.reshape(hl, rr, 4, cc // 4), (2, 0, 1, 3))
        got = _chip_exchange("rs_chips_" + n, pst)
        mine = _sum_chips("rs_sum_" + n, pst, got, chip_arr)
        other = _swap_halves("rs_swap_" + n, mine)
        grads[n], deltas[n], new_m[n], new_v[n] = _adamw_halves(
            "adamw_" + n, p[n], mine, other, p["m_" + n], p["v_" + n], c_arr)

    def pack(get):
        flat = lambda n: jnp.pad(get(n).reshape(1, -1), ((0, 0), (0, d - get(n).size)))
        rows = [get("ffn1_norm"), get("mix_norm"), get("ffn2_norm"), get("final_norm")[None, :],
                flat("mla_q_norm"), flat("mla_kv_norm"), flat("fox_forget_bias")]
        return jnp.concatenate(rows, axis=0)

    def unpack(blk):
        o = 3 * depth
        return {"ffn1_norm": blk[0:depth], "mix_norm": blk[depth:2 * depth], "ffn2_norm": blk[2 * depth:o],
                "final_norm": blk[o], "mla_q_norm": blk[o + 1, :depth * dm.rank].reshape(depth, dm.rank),
                "mla_kv_norm": blk[o + 2, :depth * dm.rank].reshape(depth, dm.rank),
                "fox_forget_bias": blk[o + 3, :depth * dm.fox_h].reshape(depth, dm.fox_h)}

    assert depth * dm.rank <= d
    local_small = {n: (dg_final[0] if n == "final_norm" else jnp.stack([g_layers[l][n] for l in range(depth)]))
                   for n in SMALL}
    n_rows = 3 * depth + 4
    pad_rows = -(n_rows + 1) % 8
    blk = jnp.concatenate([pack(lambda n: local_small[n]), jnp.broadcast_to(loss_blk[0:1, 0:1], (1, d)),
                           jnp.zeros((pad_rows, d), F32)], axis=0)
    everyone = _gather_all("small_gather", blk)[:, None]
    total = _sum_slots("small_sum", everyone, tuple(range(8)))[0]
    loss = total[n_rows, 0]
    small_g = unpack(total)
    w_blk = jnp.concatenate([pack(lambda n: p[n]), jnp.zeros((pad_rows + 1, d), F32)], axis=0)
    m_blk = jnp.concatenate([pack(lambda n: p["m_" + n]), jnp.zeros((pad_rows + 1, d), F32)], axis=0)
    v_blk = jnp.concatenate([pack(lambda n: p["v_" + n]), jnp.zeros((pad_rows + 1, d), F32)], axis=0)
    g_blk = jnp.concatenate([total[:n_rows], jnp.zeros((pad_rows + 1, d), F32)], axis=0)
    d_blk, nm_blk, nv_blk = _adamw("adamw_small", w_blk[None], g_blk[None], m_blk[None], v_blk[None])
    small_d, small_m, small_v = unpack(d_blk[0]), unpack(nm_blk[0]), unpack(nv_blk[0])
    for n in SMALL:
        grads[n], deltas[n], new_m[n], new_v[n] = small_g[n], small_d[n], small_m[n], small_v[n]

    return (loss, grad_x, *[grads[n] for n in WEIGHTS], *[deltas[n] for n in WEIGHTS],
            *[new_m[n] for n in WEIGHTS], *[new_v[n] for n in WEIGHTS])
```

```python
import functools

import jax
import jax.numpy as jnp
from jax import lax
from jax.experimental import pallas as pl
from jax.experimental.pallas import tpu as pltpu

F32 = jnp.float32
BF16 = jnp.bfloat16
MESH = pl.DeviceIdType.MESH
ANY = pl.BlockSpec(memory_space=pl.ANY)

LANES = 128
VMEM_LIMIT_BYTES = 56 * 2 ** 20
HEAD = 128
EPS = 1e-6
NEG = -1e30
ATTN_TILE = 512
ROPE_THETA = 500000.0
MLA_ROPE = 64
PARTIAL_ROPE = HEAD // 4
DIL_BRANCHES = ((128, 1), (512, 4), (2048, 16))
DIL_REACH = max(w for w, _ in DIL_BRANCHES)
FORGET_LANE = MLA_ROPE
ADAM_LR, ADAM_B1, ADAM_B2, ADAM_EPS, ADAM_WD, ADAM_STEP = 0.001, 0.9, 0.999, 1e-08, 0.01, 10

BIG = ("ffn1_w_gate", "ffn1_w_up", "ffn1_w_down", "w_in", "mla_w_uq", "mla_w_ukv", "w_out",
       "ffn2_w_gate", "ffn2_w_up", "ffn2_w_down")
ROW_SHARDED = ("ffn1_w_down", "w_out", "ffn2_w_down")
GATHER_BUNDLES = (("ffn1_w_gate", "ffn1_w_up"), ("ffn1_w_down",), ("mla_w_uq", "mla_w_ukv", "w_out"), ("w_in",),
                  ("ffn2_w_gate", "ffn2_w_up"), ("ffn2_w_down",))
REDUCE_BUNDLES = (("ffn2_w_gate", "ffn2_w_up", "ffn2_w_down"), ("ffn1_w_gate", "ffn1_w_up", "ffn1_w_down"),
                  ("w_in", "w_out", "mla_w_uq", "mla_w_ukv"))
SMALL = ("ffn1_norm", "mix_norm", "fox_forget_bias", "mla_q_norm", "mla_kv_norm", "ffn2_norm", "final_norm")
WEIGHTS = ("ffn1_norm", "ffn1_w_gate", "ffn1_w_up", "ffn1_w_down", "mix_norm", "w_in", "fox_forget_bias",
           "mla_q_norm", "mla_kv_norm", "mla_w_uq", "mla_w_ukv", "w_out", "ffn2_norm", "ffn2_w_gate",
           "ffn2_w_up", "ffn2_w_down", "final_norm")


def _params(*sem):
    return pltpu.CompilerParams(dimension_semantics=sem, vmem_limit_bytes=VMEM_LIMIT_BYTES)


def _div_tile(n, cap):
    if n <= cap:
        return n
    best = 0
    for t in range(LANES, cap + 1, LANES):
        if n % t == 0:
            best = t
    assert best, (n, cap)
    return best


MM_TILE_BUDGET_BYTES = 34 * 2 ** 20


def _mm_tiles(m, n, k, pairs, out_bytes, has_res):
    def cands(x, cap):
        return [x] if x <= LANES else [t for t in range(LANES, min(x, cap) + 1, LANES) if x % t == 0]

    best = None
    for tm in cands(m, 1024):
        for tn in cands(n, 1408):
            for tk in cands(k, 2048):
                need = (2 * 2 * pairs * (tm * tk + tk * tn) + 4 * tm * tn + 2 * out_bytes * tm * tn
                        + (2 * 4 * tm * tn if has_res else 0))
                if need > MM_TILE_BUDGET_BYTES:
                    continue
                key = (-(1.0 / tm + 1.0 / tn), tk)
                if best is None or key > best[0]:
                    best = (key, (tm, tn, tk))
    assert best is not None, (m, n, k)
    return best[1]


def _mm(name, a, b, *, ta=False, tb=False, a2=None, b2=None, res=None, alpha=1.0, out_dtype=BF16, queue=None):
    m, k = (a.shape[1], a.shape[0]) if ta else a.shape
    n, kb = (b.shape[0], b.shape[1]) if tb else (b.shape[1], b.shape[0])
    assert k == kb, (name, a.shape, b.shape)
    pairs = 1 if a2 is None else 2
    if pairs == 2:
        assert a2.shape == a.shape and b2.shape == b.shape
    tm, tn, tk = _mm_tiles(m, n, k, pairs, jnp.dtype(out_dtype).itemsize, res is not None)
    nk = k // tk
    a_spec = (pl.BlockSpec((tk, tm), lambda i, j, kk: (kk, i)) if ta
              else pl.BlockSpec((tm, tk), lambda i, j, kk: (i, kk)))
    b_spec = (pl.BlockSpec((tn, tk), lambda i, j, kk: (j, kk)) if tb
              else pl.BlockSpec((tk, tn), lambda i, j, kk: (kk, j)))
    dims = (((0 if ta else 1,), (1 if tb else 0,)), ((), ()))
    operands, specs = [a, b], [a_spec, b_spec]
    if pairs == 2:
        operands += [a2, b2]
        specs += [a_spec, b_spec]
    n_mm = len(operands)
    if res is not None:
        operands.append(res)
        specs.append(pl.BlockSpec((tm, tn), lambda i, j, kk: (i, j)))

    def body(*refs):
        o_ref, acc_ref = refs[-2], refs[-1]
        kk = pl.program_id(2)

        @pl.when(kk == 0)
        def _():
            acc_ref[...] = jnp.zeros_like(acc_ref)

        acc_ref[...] += lax.dot_general(refs[0][...], refs[1][...], dims, preferred_element_type=F32)
        if pairs == 2:
            acc_ref[...] += lax.dot_general(refs[2][...], refs[3][...], dims, preferred_element_type=F32)

        @pl.when(kk == nk - 1)
        def _():
            out = acc_ref[...] * alpha
            if res is not None:
                out = out + refs[n_mm][...].astype(F32)
            o_ref[...] = out.astype(o_ref.dtype)

    return _call(body, name, (m // tm, n // tn, nk), specs,
                 [pl.BlockSpec((tm, tn), lambda i, j, kk: (i, j))], [jax.ShapeDtypeStruct((m, n), out_dtype)],
                 [pltpu.VMEM((tm, tn), F32)], ("parallel", "parallel", "arbitrary"), operands, queue)[0]


def _ffn_up(name, h, wg, wu, queue=None):
    s, d = h.shape
    f = wg.shape[1]
    tm, tn = _div_tile(s, 512), _div_tile(f, 512)

    def body(h_ref, wg_ref, wu_ref, a_ref, b_ref, z_ref):
        hv = h_ref[...]
        a = jnp.dot(hv, wg_ref[...], preferred_element_type=F32)
        b = jnp.dot(hv, wu_ref[...], preferred_element_type=F32)
        a_ref[...] = a.astype(BF16)
        b_ref[...] = b.astype(BF16)
        z_ref[...] = (a * jax.nn.sigmoid(a) * b).astype(BF16)

    tile = pl.BlockSpec((tm, tn), lambda i, j: (i, j))
    w_spec = pl.BlockSpec((d, tn), lambda i, j: (0, j))
    return _call(body, name, (s // tm, f // tn), [pl.BlockSpec((tm, d), lambda i, j: (i, 0)), w_spec, w_spec],
                 [tile, tile, tile], [jax.ShapeDtypeStruct((s, f), BF16)] * 3, [], ("parallel", "parallel"),
                 [h, wg, wu], queue)


def _ffn_bwd_mid(name, dy, wd, a, b, alpha, queue=None):
    s, d = dy.shape
    f = wd.shape[0]
    tm, tn = _div_tile(s, 512), _div_tile(f, 512)

    def body(dy_ref, wd_ref, a_ref, b_ref, da_ref, db_ref):
        dz = lax.dot_general(dy_ref[...], wd_ref[...], (((1,), (1,)), ((), ())),
                             preferred_element_type=F32) * alpha
        av = a_ref[...].astype(F32)
        sg = jax.nn.sigmoid(av)
        db_ref[...] = (dz * av * sg).astype(BF16)
        da_ref[...] = (dz * b_ref[...].astype(F32) * (sg * (1.0 + av * (1.0 - sg)))).astype(BF16)

    tile = pl.BlockSpec((tm, tn), lambda i, j: (i, j))
    return _call(body, name, (s // tm, f // tn),
                 [pl.BlockSpec((tm, d), lambda i, j: (i, 0)), pl.BlockSpec((tn, d), lambda i, j: (j, 0)), tile, tile],
                 [tile, tile], [jax.ShapeDtypeStruct((s, f), BF16)] * 2, [], ("parallel", "parallel"),
                 [dy, wd, a, b], queue)


def _rms_fwd(name, x, g, *, width=None, col=0):
    s = x.shape[0]
    w = x.shape[1] if width is None else width
    assert col % w == 0
    cb, tr = col // w, min(256, s)

    def body(x_ref, g_ref, h_ref):
        xf = x_ref[...].astype(F32)
        r = lax.rsqrt(jnp.mean(xf * xf, axis=-1, keepdims=True) + EPS)
        h_ref[...] = (xf * r * g_ref[...]).astype(BF16)

    return pl.pallas_call(
        body, name=name, grid=(s // tr,),
        in_specs=[pl.BlockSpec((tr, w), lambda i: (i, cb)), pl.BlockSpec((1, w), lambda i: (0, 0))],
        out_specs=pl.BlockSpec((tr, w), lambda i: (i, 0)),
        out_shape=jax.ShapeDtypeStruct((s, w), BF16),
        compiler_params=_params("parallel"))(x, g)


def _rms_bwd(name, dh, x, g, *, res=None, width=None, col=0):
    s = x.shape[0]
    w = x.shape[1] if width is None else width
    assert col % w == 0
    cb, tr = col // w, min(256, s)
    has_res = res is not None

    def body(*refs):
        dh_ref, x_ref, g_ref = refs[:3]
        dg_ref = refs[-1]
        i = pl.program_id(0)
        xf = x_ref[...].astype(F32)
        r = lax.rsqrt(jnp.mean(xf * xf, axis=-1, keepdims=True) + EPS)
        xh = xf * r
        d = dh_ref[...].astype(F32)
        dxh = d * g_ref[...]
        dx = r * (dxh - xh * jnp.mean(dxh * xh, axis=-1, keepdims=True))
        if has_res:
            dx = dx + refs[3][...]
            refs[4][...] = dx
            refs[5][...] = dx.astype(BF16)
        else:
            refs[3][...] = dx.astype(BF16)

        @pl.when(i == 0)
        def _():
            dg_ref[...] = jnp.zeros_like(dg_ref)

        dg_ref[0:1, :] += jnp.sum(d * xh, axis=0, keepdims=True)

    row = pl.BlockSpec((tr, w), lambda i: (i, 0))
    in_specs = [row, pl.BlockSpec((tr, w), lambda i: (i, cb)), pl.BlockSpec((1, w), lambda i: (0, 0))]
    operands = [dh, x, g]
    dg_spec = pl.BlockSpec((8, w), lambda i: (0, 0))
    dg_shape = jax.ShapeDtypeStruct((8, w), F32)
    if has_res:
        in_specs.append(row)
        operands.append(res)
        out_specs = [row, row, dg_spec]
        out_shape = [jax.ShapeDtypeStruct((s, w), F32), jax.ShapeDtypeStruct((s, w), BF16), dg_shape]
    else:
        out_specs = [row, dg_spec]
        out_shape = [jax.ShapeDtypeStruct((s, w), BF16), dg_shape]
    return pl.pallas_call(
        body, name=name, grid=(s // tr,), in_specs=in_specs, out_specs=out_specs, out_shape=out_shape,
        compiler_params=_params("arbitrary"))(*operands)


def _loss_head(name, x, g, target):
    s, d = x.shape
    tr = min(256, s)
    n = s // tr

    def body(x_ref, g_ref, t_ref, dx_ref, dxb_ref, dg_ref, loss_ref, sq_ref):
        i = pl.program_id(0)
        xf = x_ref[...]
        r = lax.rsqrt(jnp.mean(xf * xf, axis=-1, keepdims=True) + EPS)
        xh = xf * r
        gv = g_ref[...]
        err = xh * gv - t_ref[...]
        dy = err * (1.0 / d)
        dxh = dy * gv
        dx = r * (dxh - xh * jnp.mean(dxh * xh, axis=-1, keepdims=True))
        dx_ref[...] = dx
        dxb_ref[...] = dx.astype(BF16)

        @pl.when(i == 0)
        def _():
            dg_ref[...] = jnp.zeros_like(dg_ref)
            sq_ref[...] = jnp.zeros_like(sq_ref)

        dg_ref[0:1, :] += jnp.sum(dy * xh, axis=0, keepdims=True)
        sq_ref[...] += jnp.sum(err * err, axis=0, keepdims=True)

        @pl.when(i == n - 1)
        def _():
            total = jnp.sum(sq_ref[...], axis=1, keepdims=True) * (0.5 / d)
            loss_ref[...] = jnp.broadcast_to(total, loss_ref.shape)

    row = pl.BlockSpec((tr, d), lambda i: (i, 0))
    return pl.pallas_call(
        body, name=name, grid=(n,),
        in_specs=[row, pl.BlockSpec((1, d), lambda i: (0, 0)), row],
        out_specs=[row, row, pl.BlockSpec((8, d), lambda i: (0, 0)), pl.BlockSpec((8, LANES), lambda i: (0, 0))],
        out_shape=[jax.ShapeDtypeStruct((s, d), F32), jax.ShapeDtypeStruct((s, d), BF16),
                   jax.ShapeDtypeStruct((8, d), F32), jax.ShapeDtypeStruct((8, LANES), F32)],
        scratch_shapes=[pltpu.VMEM((1, d), F32)],
        compiler_params=_params("arbitrary"))(x, g, target)


def _rope(name, x, tabs, half, *, width, col=0, nblk=1, transpose=False, out_dtype=BF16):
    s = x.shape[0]
    assert col % width == 0
    cb0, tr = col // width, min(256, s)

    def body(x_ref, c_ref, s1_ref, s2_ref, y_ref):
        xf = x_ref[...].astype(F32)
        if transpose:
            y = (xf * c_ref[...] + pltpu.roll(xf * s1_ref[...], half, 1)
                 + pltpu.roll(xf * s2_ref[...], width - half, 1))
        else:
            y = (xf * c_ref[...] + pltpu.roll(xf, width - half, 1) * s1_ref[...]
                 + pltpu.roll(xf, half, 1) * s2_ref[...])
        y_ref[...] = y.astype(y_ref.dtype)

    tab = pl.BlockSpec((tr, width), lambda i, j: (i, 0))
    return pl.pallas_call(
        body, name=name, grid=(s // tr, nblk),
        in_specs=[pl.BlockSpec((tr, width), lambda i, j: (i, cb0 + j)), tab, tab, tab],
        out_specs=pl.BlockSpec((tr, width), lambda i, j: (i, j)),
        out_shape=jax.ShapeDtypeStruct((s, nblk * width), out_dtype),
        compiler_params=_params("parallel", "parallel"))(x, *tabs)


def _split3(v):
    hi = v.astype(BF16)
    r1 = v - hi.astype(F32)
    mid = r1.astype(BF16)
    lo = (r1 - mid.astype(F32)).astype(BF16)
    return hi, mid, lo


def _tri_matmul(tri, v):
    hi, mid, lo = _split3(v)
    return (jnp.dot(tri, hi, preferred_element_type=F32) + jnp.dot(tri, mid, preferred_element_type=F32)
            + jnp.dot(tri, lo, preferred_element_type=F32))


def _log_sigmoid(v):
    return jnp.minimum(v, 0.0) - jnp.log(1.0 + jnp.exp(-jnp.abs(v)))


def _gate_fwd(name, tail, bias, mask):
    s = tail.shape[0]
    tr = min(512, s)

    def body(t_ref, b_ref, m_ref, cum_ref, carry_ref):
        i = pl.program_id(0)

        @pl.when(i == 0)
        def _():
            carry_ref[...] = jnp.zeros_like(carry_ref)

        lf = _log_sigmoid(t_ref[...] + b_ref[...]) * m_ref[...]
        r = lax.broadcasted_iota(jnp.int32, (tr, tr), 0)
        c = lax.broadcasted_iota(jnp.int32, (tr, tr), 1)
        cum = _tri_matmul((r >= c).astype(BF16), lf) + carry_ref[...]
        cum_ref[...] = cum
        carry_ref[...] = cum[tr - 1:tr, :]

    vec = pl.BlockSpec((1, LANES), lambda i: (0, 0))
    return pl.pallas_call(
        body, name=name, grid=(s // tr,),
        in_specs=[pl.BlockSpec((tr, LANES), lambda i: (i, 0)), vec, vec],
        out_specs=pl.BlockSpec((tr, LANES), lambda i: (i, 0)),
        out_shape=jax.ShapeDtypeStruct((s, LANES), F32),
        scratch_shapes=[pltpu.VMEM((1, LANES), F32)],
        compiler_params=_params("arbitrary"))(tail, bias, mask)


def _gate_bwd(name, dcum, tail, bias, mask):
    s = tail.shape[0]
    tr = min(512, s)
    n = s // tr

    def body(dc_ref, t_ref, b_ref, m_ref, dt_ref, db_ref, carry_ref):
        i = pl.program_id(0)

        @pl.when(i == 0)
        def _():
            carry_ref[...] = jnp.zeros_like(carry_ref)
            db_ref[...] = jnp.zeros_like(db_ref)

        r = lax.broadcasted_iota(jnp.int32, (tr, tr), 0)
        c = lax.broadcasted_iota(jnp.int32, (tr, tr), 1)
        dlf = _tri_matmul((r <= c).astype(BF16), dc_ref[...]) + carry_ref[...]
        carry_ref[...] = dlf[0:1, :]
        v = t_ref[...] + b_ref[...]
        dt = dlf * jax.nn.sigmoid(-v) * m_ref[...]
        dt_ref[...] = dt
        db_ref[0:1, :] += jnp.sum(dt, axis=0, keepdims=True)

    vec = pl.BlockSpec((1, LANES), lambda i: (0, 0))
    rev = pl.BlockSpec((tr, LANES), lambda i: (n - 1 - i, 0))
    return pl.pallas_call(
        body, name=name, grid=(n,),
        in_specs=[rev, rev, vec, vec],
        out_specs=[rev, pl.BlockSpec((8, LANES), lambda i: (0, 0))],
        out_shape=[jax.ShapeDtypeStruct((s, LANES), F32), jax.ShapeDtypeStruct((8, LANES), F32)],
        scratch_shapes=[pltpu.VMEM((1, LANES), F32)],
        compiler_params=_params("arbitrary"))(dcum, tail, bias, mask)


NT = (((1,), (1,)), ((), ()))
TN = (((0,), (0,)), ((), ()))


def _weights_and_scores(mode, scale, t, qi, kb, q1, k1, q2, k2, cq, ck, masked=True):
    sc = lax.dot_general(q1, k1, NT, preferred_element_type=F32)
    if q2 is not None:
        sc = sc + lax.dot_general(q2, k2, NT, preferred_element_type=F32)
    sc = sc * scale
    if cq is not None:
        sc = sc + (cq - ck)
    if not masked:
        return sc, None
    row = lax.broadcasted_iota(jnp.int32, (t, t), 0)
    col = lax.broadcasted_iota(jnp.int32, (t, t), 1)
    dist = row - col + (qi - kb) * t
    back = dist >= 0
    if mode == "dil":
        w = jnp.zeros((t, t), F32)
        for window, dil in DIL_BRANCHES:
            w = w + (back & (dist <= window) & ((dist & (dil - 1)) == 0)).astype(F32)
    else:
        w = back.astype(F32)
    return jnp.where(w > 0.0, sc, NEG), w


def _causal_steps(mode, valid, diagonal, step):
    if mode == "dil":
        pl.when(valid)(functools.partial(step, True))
    else:
        pl.when(valid & jnp.logical_not(diagonal))(functools.partial(step, False))
        pl.when(diagonal)(functools.partial(step, True))


def _attn_geometry(mode, s):
    t = min(ATTN_TILE, s)
    nq = s // t
    nsteps = min(nq, DIL_REACH // t + 1) if mode == "dil" else nq
    return t, nq, nsteps


def _attn_operands(mode, nh, t, ops, hf, qf, kf):
    def cols(width, base, rows):
        return pl.BlockSpec((t, width), lambda *g: (rows(*g), base + hf(*g)))

    if mode == "fox":
        pm, ccol, crow = ops
        return ([pm, pm, pm, ccol, crow],
                [cols(HEAD, 0, qf), cols(HEAD, nh, kf), cols(HEAD, 2 * nh, kf),
                 pl.BlockSpec((None, t, 1), lambda *g: (hf(*g), qf(*g), 0)),
                 pl.BlockSpec((None, 1, t), lambda *g: (hf(*g), 0, kf(*g)))])
    if mode == "mla":
        qb, kv, kr = ops
        return ([qb, kv, kr],
                [cols(2 * HEAD, 0, qf), cols(2 * HEAD, 0, kf), pl.BlockSpec((t, HEAD), lambda *g: (kf(*g), 0))])
    q, k, pm, vbase = ops
    return [q, k, pm], [cols(HEAD, 0, qf), cols(HEAD, 0, kf), cols(HEAD, vbase, kf)]


def _unpack(mode, refs):
    if mode == "fox":
        return refs[0][...], refs[1][...], refs[2][...], None, None, refs[3][...], refs[4][...]
    if mode == "mla":
        q, kv = refs[0][...], refs[1][...]
        return q[:, :HEAD], kv[:, :HEAD], kv[:, HEAD:], q[:, HEAD:], refs[2][...], None, None
    return refs[0][...], refs[1][...], refs[2][...], None, None, None, None


def _attn_fwd(name, mode, nh, s, ops, scale):
    t, nq, nsteps = _attn_geometry(mode, s)
    hf = lambda h, i, kk: h
    qf = lambda h, i, kk: i
    kf = lambda h, i, kk: jnp.maximum(i - (nsteps - 1) + kk, 0)
    operands, specs = _attn_operands(mode, nh, t, ops, hf, qf, kf)
    n_in = len(operands)

    def body(*refs):
        o_ref, lse_ref, m_ref, l_ref, acc_ref = refs[n_in:]
        i, kk = pl.program_id(1), pl.program_id(2)
        kb = i - (nsteps - 1) + kk

        @pl.when(kk == 0)
        def _():
            m_ref[...] = jnp.full_like(m_ref, NEG)
            l_ref[...] = jnp.zeros_like(l_ref)
            acc_ref[...] = jnp.zeros_like(acc_ref)

        def step(masked):
            q1, k1, v, q2, k2, cq, ck = _unpack(mode, refs)
            sc, w = _weights_and_scores(mode, scale, t, i, kb, q1, k1, q2, k2, cq, ck, masked)
            m_old = m_ref[...]
            m_new = jnp.maximum(m_old, jnp.max(sc, axis=1, keepdims=True))
            p = jnp.exp(sc - m_new)
            if w is not None:
                p = w * p
            a = jnp.exp(m_old - m_new)
            l_ref[...] = a * l_ref[...] + jnp.sum(p, axis=1, keepdims=True)
            acc_ref[...] = a * acc_ref[...] + jnp.dot(p.astype(BF16), v, preferred_element_type=F32)
            m_ref[...] = m_new

        _causal_steps(mode, kb >= 0, kb == i, step)

        @pl.when(kk == nsteps - 1)
        def _():
            o_ref[...] = (acc_ref[...] / l_ref[...]).astype(BF16)
            lse_ref[...] = m_ref[...] + jnp.log(l_ref[...])

    return pl.pallas_call(
        body, name=name, grid=(nh, nq, nsteps), in_specs=specs,
        out_specs=[pl.BlockSpec((t, HEAD), lambda h, i, kk: (i, h)),
                   pl.BlockSpec((None, t, 1), lambda h, i, kk: (h, i, 0))],
        out_shape=[jax.ShapeDtypeStruct((s, nh * HEAD), BF16), jax.ShapeDtypeStruct((nh, s, 1), F32)],
        scratch_shapes=[pltpu.VMEM((t, 1), F32), pltpu.VMEM((t, 1), F32), pltpu.VMEM((t, HEAD), F32)],
        compiler_params=_params("parallel", "parallel", "arbitrary"))(*operands)


def _attn_bwd_q(name, mode, nh, s, ops, scale, o, do, do_base, lse):
    t, nq, nsteps = _attn_geometry(mode, s)
    hf = lambda h, i, kk: h
    qf = lambda h, i, kk: i
    kf = lambda h, i, kk: jnp.maximum(i - (nsteps - 1) + kk, 0)
    operands, specs = _attn_operands(mode, nh, t, ops, hf, qf, kf)
    operands += [o, do, lse]
    specs += [pl.BlockSpec((t, HEAD), lambda h, i, kk: (i, do_base + h)),
              pl.BlockSpec((t, HEAD), lambda h, i, kk: (i, do_base + h)),
              pl.BlockSpec((None, t, 1), lambda h, i, kk: (h, i, 0))]
    n_in = len(operands)
    wq = 2 * HEAD if mode == "mla" else HEAD

    fox = mode == "fox"

    def body(*refs):
        o_ref, do_ref, lse_ref = refs[n_in - 3:n_in]
        dq_ref = refs[n_in]
        acc_ref, delta_ref, rows_ref = refs[-3:]
        i, kk = pl.program_id(1), pl.program_id(2)
        kb = i - (nsteps - 1) + kk

        @pl.when(kk == 0)
        def _():
            acc_ref[...] = jnp.zeros_like(acc_ref)
            rows_ref[...] = jnp.zeros_like(rows_ref)
            delta_ref[...] = jnp.sum(o_ref[...].astype(F32) * do_ref[...].astype(F32), axis=1, keepdims=True)

        def step(masked):
            q1, k1, v, q2, k2, cq, ck = _unpack(mode, refs)
            sc, w = _weights_and_scores(mode, scale, t, i, kb, q1, k1, q2, k2, cq, ck, masked)
            p = jnp.exp(sc - lse_ref[...])
            if w is not None:
                p = w * p
            dp = lax.dot_general(do_ref[...], v, NT, preferred_element_type=F32)
            ds32 = p * (dp - delta_ref[...])
            ds = ds32.astype(BF16)
            if fox:
                rows_ref[...] += jnp.sum(ds32, axis=1, keepdims=True)
            if mode == "mla":
                acc_ref[:, :HEAD] += jnp.dot(ds, k1, preferred_element_type=F32)
                acc_ref[:, HEAD:] += jnp.dot(ds, k2, preferred_element_type=F32)
            else:
                acc_ref[...] += jnp.dot(ds, k1, preferred_element_type=F32)

        _causal_steps(mode, kb >= 0, kb == i, step)

        @pl.when(kk == nsteps - 1)
        def _():
            dq_ref[...] = (acc_ref[...] * scale).astype(BF16)
            if fox:
                refs[n_in + 1][...] = rows_ref[...]

    out_specs = [pl.BlockSpec((t, wq), lambda h, i, kk: (i, h))]
    out_shape = [jax.ShapeDtypeStruct((s, nh * wq), BF16)]
    if fox:
        out_specs.append(pl.BlockSpec((None, t, 1), lambda h, i, kk: (h, i, 0)))
        out_shape.append(jax.ShapeDtypeStruct((nh, s, 1), F32))
    out = pl.pallas_call(
        body, name=name, grid=(nh, nq, nsteps), in_specs=specs, out_specs=out_specs, out_shape=out_shape,
        scratch_shapes=[pltpu.VMEM((t, wq), F32), pltpu.VMEM((t, 1), F32), pltpu.VMEM((t, 1), F32)],
        compiler_params=_params("parallel", "parallel", "arbitrary"))(*operands)
    return out if fox else out[0]


def _attn_bwd_kv(name, mode, nh, s, ops, scale, o, do, do_base, lse):
    t, nq, nsteps = _attn_geometry(mode, s)
    hf = lambda j, h, qq: h
    qf = lambda j, h, qq: jnp.minimum(j + qq, nq - 1)
    kf = lambda j, h, qq: j
    operands, specs = _attn_operands(mode, nh, t, ops, hf, qf, kf)
    operands += [o, do, lse]
    specs += [pl.BlockSpec((t, HEAD), lambda j, h, qq: (qf(j, h, qq), do_base + h)),
              pl.BlockSpec((t, HEAD), lambda j, h, qq: (qf(j, h, qq), do_base + h)),
              pl.BlockSpec((None, t, 1), lambda j, h, qq: (h, qf(j, h, qq), 0))]
    n_in = len(operands)
    head_tile = pl.BlockSpec((t, HEAD), lambda j, h, qq: (j, h))
    if mode == "fox":
        out_specs = [head_tile, head_tile, pl.BlockSpec((None, 1, t), lambda j, h, qq: (h, 0, j))]
        out_shape = [jax.ShapeDtypeStruct((s, nh * HEAD), BF16)] * 2 + [jax.ShapeDtypeStruct((nh, 1, s), F32)]
        scratch = [pltpu.VMEM((t, HEAD), F32), pltpu.VMEM((t, HEAD), F32), pltpu.VMEM((1, t), F32)]
    elif mode == "mla":
        out_specs = [pl.BlockSpec((t, 2 * HEAD), lambda j, h, qq: (j, h)),
                     pl.BlockSpec((t, HEAD), lambda j, h, qq: (j, 0))]
        out_shape = [jax.ShapeDtypeStruct((s, nh * 2 * HEAD), BF16), jax.ShapeDtypeStruct((s, HEAD), F32)]
        scratch = [pltpu.VMEM((t, HEAD), F32), pltpu.VMEM((t, HEAD), F32), pltpu.VMEM((t, HEAD), F32)]
    else:
        out_specs = [head_tile, head_tile]
        out_shape = [jax.ShapeDtypeStruct((s, nh * HEAD), BF16)] * 2
        scratch = [pltpu.VMEM((t, HEAD), F32), pltpu.VMEM((t, HEAD), F32)]
    n_out = len(out_specs)

    def body(*refs):
        o_ref, do_ref, lse_ref = refs[n_in - 3:n_in]
        outs = refs[n_in:n_in + n_out]
        dk_acc, dv_acc = refs[n_in + n_out], refs[n_in + n_out + 1]
        extra = refs[n_in + n_out + 2] if mode != "dil" else None
        j, h, qq = pl.program_id(0), pl.program_id(1), pl.program_id(2)
        qb = j + qq

        @pl.when(qq == 0)
        def _():
            dk_acc[...] = jnp.zeros_like(dk_acc)
            dv_acc[...] = jnp.zeros_like(dv_acc)
            if mode == "fox":
                extra[...] = jnp.zeros_like(extra)

        if mode == "mla":
            @pl.when((qq == 0) & (h == 0))
            def _():
                extra[...] = jnp.zeros_like(extra)

        def step(masked):
            q1, k1, v, q2, k2, cq, ck = _unpack(mode, refs)
            sc, w = _weights_and_scores(mode, scale, t, qb, j, q1, k1, q2, k2, cq, ck, masked)
            p = jnp.exp(sc - lse_ref[...])
            if w is not None:
                p = w * p
            dov = do_ref[...]
            delta = jnp.sum(o_ref[...].astype(F32) * dov.astype(F32), axis=1, keepdims=True)
            dv_acc[...] += lax.dot_general(p.astype(BF16), dov, TN, preferred_element_type=F32)
            dp = lax.dot_general(dov, v, NT, preferred_element_type=F32)
            ds = p * (dp - delta)
            dsb = ds.astype(BF16)
            dk_acc[...] += lax.dot_general(dsb, q1, TN, preferred_element_type=F32)
            if mode == "mla":
                extra[...] += lax.dot_general(dsb, q2, TN, preferred_element_type=F32)
            if mode == "fox":
                extra[...] -= jnp.sum(ds, axis=0, keepdims=True)

        _causal_steps(mode, qb < nq, qq == 0, step)

        @pl.when(qq == nsteps - 1)
        def _():
            if mode == "mla":
                outs[0][:, :HEAD] = (dk_acc[...] * scale).astype(BF16)
                outs[0][:, HEAD:] = dv_acc[...].astype(BF16)
            else:
                outs[0][...] = (dk_acc[...] * scale).astype(BF16)
                outs[1][...] = dv_acc[...].astype(BF16)
            if mode == "fox":
                outs[2][...] = extra[...]

        if mode == "mla":
            @pl.when((qq == nsteps - 1) & (h == nh - 1))
            def _():
                outs[1][...] = extra[...] * scale

    return pl.pallas_call(
        body, name=name, grid=(nq, nh, nsteps), in_specs=specs, out_specs=out_specs, out_shape=out_shape,
        scratch_shapes=scratch,
        compiler_params=_params("parallel", "arbitrary", "arbitrary"))(*operands)


ROW_TILE_BYTES = 2 ** 20


def _row_tile(rows, cols):
    best = 0
    for t in range(16, rows + 1, 16):
        if rows % t == 0 and t * cols * 4 <= ROW_TILE_BYTES:
            best = t
    return best if best else rows


def _adamw(name, w, g, m, v):
    nl, r, c = w.shape
    tr = _row_tile(r, c)

    def body(w_ref, g_ref, m_ref, v_ref, d_ref, nm_ref, nv_ref):
        gv = g_ref[...]
        nm = ADAM_B1 * m_ref[...] + (1.0 - ADAM_B1) * gv
        nv = ADAM_B2 * v_ref[...] + (1.0 - ADAM_B2) * jnp.square(gv)
        m_hat = nm / (1.0 - ADAM_B1 ** ADAM_STEP)
        v_hat = nv / (1.0 - ADAM_B2 ** ADAM_STEP)
        d_ref[...] = -ADAM_LR * (m_hat / (jnp.sqrt(v_hat) + ADAM_EPS) + ADAM_WD * w_ref[...])
        nm_ref[...] = nm
        nv_ref[...] = nv

    blk = pl.BlockSpec((None, tr, c), lambda l, i: (l, i, 0))
    return pl.pallas_call(
        body, name=name, grid=(nl, r // tr), in_specs=[blk] * 4, out_specs=[blk] * 3,
        out_shape=[jax.ShapeDtypeStruct(w.shape, F32)] * 3,
        compiler_params=_params("parallel", "parallel"))(w, g, m, v)


def _add_half(name, g, recv, c_arr):
    nl, r, c = g.shape
    hl = nl // 2
    tr = _row_tile(r, c)

    def body(c_ref, g_ref, r_ref, o_ref):
        o_ref[...] = (g_ref[...] + r_ref[...]).astype(BF16)

    grid_spec = pltpu.PrefetchScalarGridSpec(
        num_scalar_prefetch=1, grid=(hl, r // tr),
        in_specs=[pl.BlockSpec((None, tr, c), lambda l, i, c_ref: (c_ref[0] * hl + l, i, 0)),
                  pl.BlockSpec((None, tr, c), lambda l, i, c_ref: (l, i, 0))],
        out_specs=pl.BlockSpec((None, tr, c), lambda l, i, c_ref: (l, i, 0)))
    return pl.pallas_call(
        body, name=name, grid_spec=grid_spec, out_shape=jax.ShapeDtypeStruct((hl, r, c), BF16),
        compiler_params=_params("parallel", "parallel"))(c_arr, g, recv)


def _sum_chips(name, pst, got, chip_arr):
    _, nl, r, c = pst.shape
    tr = _row_tile(r, c)

    def body(chip_ref, own_ref, g0_ref, g1_ref, g2_ref, o_ref):
        o_ref[...] = ((own_ref[...].astype(F32) + g0_ref[...].astype(F32)) + g1_ref[...].astype(F32)
                      ) + g2_ref[...].astype(F32)

    def slot(k):
        return pl.BlockSpec((None, None, tr, c), lambda l, i, chip_ref: (k, l, i, 0))

    grid_spec = pltpu.PrefetchScalarGridSpec(
        num_scalar_prefetch=1, grid=(nl, r // tr),
        in_specs=[pl.BlockSpec((None, None, tr, c), lambda l, i, chip_ref: (chip_ref[0], l, i, 0)),
                  slot(0), slot(1), slot(2)],
        out_specs=pl.BlockSpec((None, tr, c), lambda l, i, chip_ref: (l, i, 0)))
    return pl.pallas_call(
        body, name=name, grid_spec=grid_spec, out_shape=jax.ShapeDtypeStruct((nl, r, c), F32),
        compiler_params=_params("parallel", "parallel"))(chip_arr, pst, got, got, got)


def _adamw_layers(name, w, mines, others, m, v, c_arr):
    nl, r, c = w.shape
    hr = r // 2
    tr = _row_tile(hr, c)
    nb = hr // tr

    def body(c_ref, w_ref, m_ref, v_ref, *refs):
        halves, (g_ref, d_ref, nm_ref, nv_ref) = refs[:2 * nl], refs[2 * nl:]
        layer, is_mine = pl.program_id(0), pl.program_id(1) == c_ref[0]
        gv = jnp.where(is_mine, halves[0][...], halves[nl][...])
        for k in range(1, nl):
            gv = jnp.where(layer == k, jnp.where(is_mine, halves[k][...], halves[nl + k][...]), gv)
        nm = ADAM_B1 * m_ref[...] + (1.0 - ADAM_B1) * gv
        nv = ADAM_B2 * v_ref[...] + (1.0 - ADAM_B2) * jnp.square(gv)
        m_hat = nm / (1.0 - ADAM_B1 ** ADAM_STEP)
        v_hat = nv / (1.0 - ADAM_B2 ** ADAM_STEP)
        g_ref[...] = gv
        d_ref[...] = -ADAM_LR * (m_hat / (jnp.sqrt(v_hat) + ADAM_EPS) + ADAM_WD * w_ref[...])
        nm_ref[...] = nm
        nv_ref[...] = nv

    def half_of(k):
        return pl.BlockSpec((None, tr, c), lambda l, h, i, c_ref: (0, jnp.where(l == k, i, 0), 0))

    blk = pl.BlockSpec((None, tr, c), lambda l, h, i, c_ref: (l, h * nb + i, 0))
    grid_spec = pltpu.PrefetchScalarGridSpec(
        num_scalar_prefetch=1, grid=(nl, 2, nb),
        in_specs=[blk, blk, blk] + [half_of(k) for k in range(nl)] * 2, out_specs=[blk] * 4)
    return pl.pallas_call(
        body, name=name, grid_spec=grid_spec, out_shape=[jax.ShapeDtypeStruct(w.shape, F32)] * 4,
        compiler_params=_params("parallel", "parallel", "parallel"))(c_arr, w, m, v, *mines, *others)


def _sum_slots(name, rc, order, out_dtype=F32):
    _, nl, r, c = rc.shape
    tr = _row_tile(r, c)

    def body(*refs):
        acc = refs[0][...].astype(F32)
        for ref in refs[1:-1]:
            acc = acc + ref[...].astype(F32)
        refs[-1][...] = acc.astype(out_dtype)

    def slot(k):
        return pl.BlockSpec((None, None, tr, c), lambda l, i: (k, l, i, 0))

    return pl.pallas_call(
        body, name=name, grid=(nl, r // tr), in_specs=[slot(k) for k in order],
        out_specs=pl.BlockSpec((None, tr, c), lambda l, i: (l, i, 0)),
        out_shape=jax.ShapeDtypeStruct((nl, r, c), out_dtype),
        compiler_params=_params("parallel", "parallel"))(*([rc] * len(order)))


class _Comm:
    def __init__(self, name, ins, out_shapes, build, n_first, n_then=0, n_local=0):
        self.name, self.ins, self.out_shapes, self.build = name, list(ins), list(out_shapes), build
        self.n_first, self.n_then, self.n_local = n_first, n_then, n_local


def _comm_steps(comm, in_refs, out_refs, send_sems, recv_sems, local_sems=None):
    def descriptors(with_then):
        x, y, c = lax.axis_index("x"), lax.axis_index("y"), lax.axis_index("c")
        first, then, local = comm.build(in_refs, out_refs, x, y, c)
        assert (len(first), len(then), len(local)) == (comm.n_first, comm.n_then, comm.n_local)

        def remote(k, src, dst, flips):
            fx, fy, fc = flips
            peer = (1 - x if fx else x, 1 - y if fy else y, 1 - c if fc else c)
            return pltpu.make_async_remote_copy(src_ref=src, dst_ref=dst, send_sem=send_sems.at[k],
                                                recv_sem=recv_sems.at[k], device_id=peer, device_id_type=MESH)

        f = [remote(k, s, d, fl) for k, (s, d, fl) in enumerate(first)]
        t = [(remote(len(first) + k, s, d, fl), dep) for k, (s, d, fl, dep) in enumerate(then)] if with_then else []
        lc = [pltpu.make_async_copy(s, d, local_sems.at[k]) for k, (s, d) in enumerate(local)]
        return f, t, lc

    def start():
        f, _, lc = descriptors(False)
        for cp in f + lc:
            cp.start()

    def finish():
        f, t, lc = descriptors(True)
        for k, cp in enumerate(f):
            cp.wait_recv()
            for cp2, dep in t:
                if dep == k:
                    cp2.start()
        for cp2, _ in t:
            cp2.wait_recv()
        for cp in f + [cp2 for cp2, _ in t]:
            cp.wait_send()
        for cp in lc:
            cp.wait()

    return start, finish


def _run_comm(comm):
    n_in, n_out = len(comm.ins), len(comm.out_shapes)

    def body(*refs):
        start, finish = _comm_steps(comm, refs[:n_in], refs[n_in:n_in + n_out], *refs[n_in + n_out:])
        start()
        finish()

    n_sem = comm.n_first + comm.n_then
    return pl.pallas_call(
        body, name=comm.name, in_specs=[ANY] * n_in, out_specs=[ANY] * n_out, out_shape=comm.out_shapes,
        scratch_shapes=[pltpu.SemaphoreType.DMA((n_sem,)), pltpu.SemaphoreType.DMA((n_sem,)),
                        pltpu.SemaphoreType.DMA((max(comm.n_local, 1),))])(*comm.ins)


def _call(body, name, grid, in_specs, out_specs, out_shape, scratch, semantics, operands, queue=None):
    job = queue.take() if queue is not None else None
    if job is None:
        return pl.pallas_call(body, name=name, grid=grid, in_specs=in_specs, out_specs=out_specs,
                              out_shape=out_shape, scratch_shapes=scratch,
                              compiler_params=_params(*semantics))(*operands)
    comm = job[1]
    assert comm.n_local == 0
    n_in, n_out, n_cin, n_cout = len(in_specs), len(out_specs), len(comm.ins), len(comm.out_shapes)
    n_sem = comm.n_first + comm.n_then

    def hosted(*refs):
        a, b = n_in, n_in + n_cin
        c, d = b + n_out, b + n_out + n_cout
        start, finish = _comm_steps(comm, refs[a:b], refs[c:d], refs[-2], refs[-1])
        ids = [pl.program_id(k) for k in range(len(grid))]
        is_first = functools.reduce(jnp.logical_and, [i == 0 for i in ids])
        is_last = functools.reduce(jnp.logical_and, [i == g - 1 for i, g in zip(ids, grid)])
        pl.when(is_first)(start)
        body(*refs[:a], *refs[b:c], *refs[d:len(refs) - 2])
        pl.when(is_last)(finish)

    outs = pl.pallas_call(
        hosted, name=name + "_and_" + comm.name, grid=grid, in_specs=list(in_specs) + [ANY] * n_cin,
        out_specs=list(out_specs) + [ANY] * n_cout, out_shape=list(out_shape) + comm.out_shapes,
        scratch_shapes=list(scratch) + [pltpu.SemaphoreType.DMA((n_sem,)), pltpu.SemaphoreType.DMA((n_sem,))],
        compiler_params=_params(*(("arbitrary",) * len(grid))))(*operands, *comm.ins)
    queue.done(job, outs[n_out:])
    return outs[:n_out]


class _CommQueue:
    def __init__(self):
        self.jobs = []

    def add(self, task):
        self.done([task, None], None, first=True)

    def take(self):
        return self.jobs.pop(0) if self.jobs else None

    def done(self, job, results, first=False):
        try:
            comm = next(job[0]) if first else job[0].send(results)
        except StopIteration:
            return
        self.jobs.append([job[0], comm])

    def drain(self):
        while self.jobs:
            job = self.take()
            self.done(job, _run_comm(job[1]))


CHIP_FLIPS = ((1, 0), (0, 1), (1, 1))


def _flip(v, f):
    return 1 - v if f else v


def _gather_comm(name, shards):
    def build(ins, outs, x, y, c):
        first, then = [], []
        me = 2 * x + y
        for w_ref, out_ref in zip(ins, outs):
            base = len(first)
            for j, (fx, fy) in enumerate(CHIP_FLIPS):
                theirs = out_ref.at[2 * _flip(x, fx) + _flip(y, fy), pl.ds(c, 1)]
                first.append((w_ref.at[pl.ds(c, 1)], out_ref.at[me, pl.ds(c, 1)], (fx, fy, 0)))
                then.append((theirs, theirs, (0, 0, 1), base + j))
            first.append((w_ref, out_ref.at[me], (0, 0, 1)))
        return first, then, []

    n = len(shards)
    return _Comm(name, shards, [jax.ShapeDtypeStruct((4,) + w.shape, w.dtype) for w in shards], build, 4 * n, 3 * n)


def _sibling_comm(name, gs):
    def build(ins, outs, x, y, c):
        return [(g.at[pl.ds(1 - c, 1)], o, (0, 0, 1)) for g, o in zip(ins, outs)], [], []

    return _Comm(name, gs, [jax.ShapeDtypeStruct((1,) + g.shape[1:], g.dtype) for g in gs], build, len(gs))


def _chips_comm(name, psts):
    def build(ins, outs, x, y, c):
        first = []
        for p_ref, o_ref in zip(ins, outs):
            for k, (fx, fy) in enumerate(CHIP_FLIPS):
                first.append((p_ref.at[2 * _flip(x, fx) + _flip(y, fy)], o_ref.at[k], (fx, fy, 0)))
        return first, [], []

    return _Comm(name, psts, [jax.ShapeDtypeStruct((3,) + p.shape[1:], p.dtype) for p in psts], build,
                 3 * len(psts))


def _swap_comm(name, ss):
    def build(ins, outs, x, y, c):
        return [(s, o, (0, 0, 1)) for s, o in zip(ins, outs)], [], []

    return _Comm(name, ss, [jax.ShapeDtypeStruct(s.shape, s.dtype) for s in ss], build, len(ss))


def _gather_all(name, blk):
    def build(ins, outs, x, y, c):
        mine = outs[0].at[4 * x + 2 * y + c]
        first = [(ins[0], mine, (m >> 2 & 1, m >> 1 & 1, m & 1)) for m in range(1, 8)]
        return first, [], [(ins[0], mine)]

    return _run_comm(_Comm(name, [blk], [jax.ShapeDtypeStruct((8,) + blk.shape, blk.dtype)], build, 7, 0, 1))[0]


class _Dims:
    def __init__(self, d, in_w):
        self.d = d
        self.fox_h = (d // 4) // HEAD
        self.mla_h = (d // 2) // HEAD
        self.dil_h = (d // 4) // HEAD
        self.gw = d // 4
        self.rank = d // 4
        gw, fh = self.gw, self.fox_h
        sizes = (gw, gw, gw, fh, self.rank, self.rank, MLA_ROPE, gw, gw, gw)
        assert sum(sizes) == in_w
        offs = [0]
        for z in sizes:
            offs.append(offs[-1] + z)
        self.nat = dict(zip(("fq", "fk", "fv", "fl", "cq", "ckv", "kr", "dq", "dk", "dv"), zip(offs[:-1], sizes)))
        self.main_order = ("fq", "fk", "fv", "cq", "ckv", "dq", "dk", "dv")
        self.main_w = 8 * gw
        self.col = {n: i * gw for i, n in enumerate(self.main_order)}

    def align_w_in(self, w):
        parts = [w[:, self.nat[n][0]:self.nat[n][0] + self.nat[n][1]] for n in self.main_order]
        parts += [w[:, self.nat["kr"][0]:self.nat["kr"][0] + MLA_ROPE],
                  w[:, self.nat["fl"][0]:self.nat["fl"][0] + self.fox_h],
                  jnp.zeros((w.shape[0], LANES - MLA_ROPE - self.fox_h), w.dtype)]
        return jnp.concatenate(parts, axis=1)

    def unalign_w_in(self, g):
        gw, t0 = self.gw, self.main_w
        src = {n: g[:, self.col[n]:self.col[n] + gw] for n in self.main_order}
        src["kr"] = g[:, t0:t0 + MLA_ROPE]
        src["fl"] = g[:, t0 + FORGET_LANE:t0 + FORGET_LANE + self.fox_h]
        return jnp.concatenate([src[n] for n in ("fq", "fk", "fv", "fl", "cq", "ckv", "kr", "dq", "dk", "dv")], axis=1)

    def pad_w_uq(self, w):
        r = w.shape[0]
        w3 = w.reshape(r, self.mla_h, HEAD + MLA_ROPE)
        return jnp.pad(w3, ((0, 0), (0, 0), (0, HEAD - MLA_ROPE))).reshape(r, self.mla_h * 2 * HEAD)

    def unpad_w_uq(self, g):
        r = g.shape[0]
        return g.reshape(r, self.mla_h, 2 * HEAD)[:, :, :HEAD + MLA_ROPE].reshape(r, self.mla_h * (HEAD + MLA_ROPE))


def _rope_tables(s):
    def tables(dim):
        inv = 1.0 / (ROPE_THETA ** (jnp.arange(0, dim, 2, dtype=F32) / dim))
        ang = jnp.arange(s, dtype=F32)[:, None] * inv[None, :]
        return jnp.cos(ang), jnp.sin(ang)

    def build(cos, sin, lead, trail_one, trail_zero):
        z = jnp.zeros_like(sin)
        ones = lambda n: jnp.ones((s, n), F32)
        zeros = lambda n: jnp.zeros((s, n), F32)
        c = jnp.concatenate([ones(lead), cos, cos, ones(trail_one), zeros(trail_zero)], axis=1)
        s1 = jnp.concatenate([zeros(lead), -sin, z, zeros(trail_one + trail_zero)], axis=1)
        s2 = jnp.concatenate([zeros(lead), z, sin, zeros(trail_one + trail_zero)], axis=1)
        return c, s1, s2

    cm, sm = tables(MLA_ROPE)
    cp, sp = tables(PARTIAL_ROPE)
    return {"mla_q": build(cm, sm, HEAD, 0, HEAD - MLA_ROPE),
            "mla_k": build(cm, sm, 0, 0, LANES - MLA_ROPE),
            "dil": build(cp, sp, 0, HEAD - PARTIAL_ROPE, 0)}


def _ffn_forward(tag, x, g, wg, wu, wd, queue):
    h = _rms_fwd(tag + "_norm", x, g)
    a, b, z = _ffn_up(tag + "_up", h, wg, wu, queue)
    y = _mm(tag + "_down", z, wd, res=x, alpha=0.5, out_dtype=F32, queue=queue)
    return y, (x, h, a, b, z)


def _ffn_backward(tag, dx, dxb, saved, g, wg, wu, wd, queue):
    x, h, a, b, z = saved
    da, db = _ffn_bwd_mid(tag + "_bwd_mid", dxb, wd, a, b, 0.5, queue)
    g_wd = _mm(tag + "_dwd", z, dxb, ta=True, alpha=0.5, out_dtype=F32, queue=queue)
    g_wg = _mm(tag + "_dwg", h, da, ta=True, out_dtype=F32, queue=queue)
    g_wu = _mm(tag + "_dwu", h, db, ta=True, out_dtype=F32, queue=queue)
    dh = _mm(tag + "_dh", da, wg, tb=True, a2=db, b2=wu, out_dtype=F32, queue=queue)
    dx, dxb, dg = _rms_bwd(tag + "_norm_bwd", dh, x, g, res=dx)
    return dx, dxb, dg[0], g_wg, g_wu, g_wd


def _mix_forward(dm, tabs, x, lw, s, queue):
    d, gw = dm.d, dm.gw
    h = _rms_fwd("mix_norm", x, lw["mix_norm"])
    pm = _mm("mix_in_main", h, lw["w_in_main"], queue=queue)
    tail = _mm("mix_in_tail", h, lw["w_in_tail"], out_dtype=F32)
    cum = _gate_fwd("fox_gate", tail, lw["gate_bias"], lw["gate_mask"])
    cum_h = cum[:, FORGET_LANE:FORGET_LANE + dm.fox_h].T
    ccol, crow = cum_h[:, :, None], cum_h[:, None, :]
    fox_ops = (pm, ccol, crow)
    out_a, lse_a = _attn_fwd("fox_fwd", "fox", dm.fox_h, s, fox_ops, HEAD ** -0.5)

    cqn = _rms_fwd("mla_q_norm", pm, lw["mla_q_norm"], width=gw, col=dm.col["cq"])
    ckvn = _rms_fwd("mla_kv_norm", pm, lw["mla_kv_norm"], width=gw, col=dm.col["ckv"])
    qb_raw = _mm("mla_uq", cqn, lw["mla_w_uq"])
    qb = _rope("mla_q_rope", qb_raw, tabs["mla_q"], MLA_ROPE // 2, width=2 * HEAD, nblk=dm.mla_h)
    kv = _mm("mla_ukv", ckvn, lw["mla_w_ukv"])
    kr = _rope("mla_k_rope", tail, tabs["mla_k"], MLA_ROPE // 2, width=LANES)
    mla_ops = (qb, kv, kr)
    out_b, lse_b = _attn_fwd("mla_fwd", "mla", dm.mla_h, s, mla_ops, (HEAD + MLA_ROPE) ** -0.5)

    dqr = _rope("dil_q_rope", pm, tabs["dil"], PARTIAL_ROPE // 2, width=HEAD, col=dm.col["dq"], nblk=dm.dil_h)
    dkr = _rope("dil_k_rope", pm, tabs["dil"], PARTIAL_ROPE // 2, width=HEAD, col=dm.col["dk"], nblk=dm.dil_h)
    dil_ops = (dqr, dkr, pm, dm.col["dv"] // HEAD)
    out_c, lse_c = _attn_fwd("dil_fwd", "dil", dm.dil_h, s, dil_ops, HEAD ** -0.5)

    mixed = jnp.concatenate([out_a, out_b, out_c], axis=1)
    y = _mm("mix_out", mixed, lw["w_out"], res=x, out_dtype=F32, queue=queue)
    saved = (x, h, pm, tail, fox_ops, lse_a, cqn, ckvn, mla_ops, lse_b, dil_ops, lse_c, mixed)
    return y, saved


def _mix_backward(dm, tabs, dx, dxb, saved, lw, s, queue):
    x, h, pm, tail, fox_ops, lse_a, cqn, ckvn, mla_ops, lse_b, dil_ops, lse_c, mixed = saved
    gw = dm.gw
    grads = {}
    grads["w_out"] = _mm("mix_dwout", mixed, dxb, ta=True, out_dtype=F32)
    dmixed = _mm("mix_dmixed", dxb, lw["w_out"], tb=True)
    nha, nhb, nhc = dm.fox_h, dm.mla_h, dm.dil_h

    sa = HEAD ** -0.5
    dfq, dcum_q = _attn_bwd_q("fox_bwd_q", "fox", nha, s, fox_ops, sa, mixed, dmixed, 0, lse_a)
    dfk, dfv, dcum = _attn_bwd_kv("fox_bwd_kv", "fox", nha, s, fox_ops, sa, mixed, dmixed, 0, lse_a)

    sb = (HEAD + MLA_ROPE) ** -0.5
    dqb = _attn_bwd_q("mla_bwd_q", "mla", nhb, s, mla_ops, sb, mixed, dmixed, nha, lse_b)
    dkv, dkr = _attn_bwd_kv("mla_bwd_kv", "mla", nhb, s, mla_ops, sb, mixed, dmixed, nha, lse_b)
    dqb_raw = _rope("mla_q_rope_bwd", dqb, tabs["mla_q"], MLA_ROPE // 2, width=2 * HEAD, nblk=nhb, transpose=True)
    grads["mla_w_uq"] = _mm("mla_dwuq", cqn, dqb_raw, ta=True, out_dtype=F32)
    dcqn = _mm("mla_dcqn", dqb_raw, lw["mla_w_uq"], tb=True)
    grads["mla_w_ukv"] = _mm("mla_dwukv", ckvn, dkv, ta=True, out_dtype=F32)
    dckvn = _mm("mla_dckvn", dkv, lw["mla_w_ukv"], tb=True)
    dcq, dg_q = _rms_bwd("mla_q_norm_bwd", dcqn, pm, lw["mla_q_norm"], width=gw, col=dm.col["cq"])
    dckv, dg_kv = _rms_bwd("mla_kv_norm_bwd", dckvn, pm, lw["mla_kv_norm"], width=gw, col=dm.col["ckv"])
    dkr_raw = _rope("mla_k_rope_bwd", dkr, tabs["mla_k"], MLA_ROPE // 2, width=LANES, transpose=True, out_dtype=F32)

    sc = HEAD ** -0.5
    ddqr = _attn_bwd_q("dil_bwd_q", "dil", nhc, s, dil_ops, sc, mixed, dmixed, nha + nhb, lse_c)
    ddkr, ddv = _attn_bwd_kv("dil_bwd_kv", "dil", nhc, s, dil_ops, sc, mixed, dmixed, nha + nhb, lse_c)
    ddq = _rope("dil_q_rope_bwd", ddqr, tabs["dil"], PARTIAL_ROPE // 2, width=HEAD, nblk=nhc, transpose=True)
    ddk = _rope("dil_k_rope_bwd", ddkr, tabs["dil"], PARTIAL_ROPE // 2, width=HEAD, nblk=nhc, transpose=True)

    dcum_lanes = jnp.pad((dcum[:, 0, :] + dcum_q[:, :, 0]).T, ((0, 0), (FORGET_LANE, LANES - FORGET_LANE - nha)))
    dgate, dbias = _gate_bwd("fox_gate_bwd", dcum_lanes, tail, lw["gate_bias"], lw["gate_mask"])
    dtail = (dgate + dkr_raw).astype(BF16)
    dpm = jnp.concatenate([dfq, dfk, dfv, dcq, dckv, ddq, ddk, ddv], axis=1)

    g_main = _mm("mix_dwin_main", h, dpm, ta=True, out_dtype=F32, queue=queue)
    g_tail = _mm("mix_dwin_tail", h, dtail, ta=True, out_dtype=F32)
    grads["w_in"] = jnp.concatenate([g_main, g_tail], axis=1)
    dh = _mm("mix_dh_main", dpm, lw["w_in_main"], tb=True, out_dtype=F32, queue=queue)
    dh = _mm("mix_dh_tail", dtail, lw["w_in_tail"], tb=True, res=dh, out_dtype=F32)
    dx, dxb, dg = _rms_bwd("mix_norm_bwd", dh, x, lw["mix_norm"], res=dx)
    grads["mix_norm"] = dg[0]
    grads["mla_q_norm"] = dg_q[0]
    grads["mla_kv_norm"] = dg_kv[0]
    grads["fox_forget_bias"] = dbias[0, FORGET_LANE:FORGET_LANE + nha]
    return dx, dxb, grads


def kernel(x, ffn1_norm, ffn1_w_gate, ffn1_w_up, ffn1_w_down, mix_norm, w_in, fox_forget_bias, mla_q_norm, mla_kv_norm, mla_w_uq, mla_w_ukv, w_out, ffn2_norm, ffn2_w_gate, ffn2_w_up, ffn2_w_down, final_norm, loss_target, m_ffn1_norm, m_ffn1_w_gate, m_ffn1_w_up, m_ffn1_w_down, m_mix_norm, m_w_in, m_fox_forget_bias, m_mla_q_norm, m_mla_kv_norm, m_mla_w_uq, m_mla_w_ukv, m_w_out, m_ffn2_norm, m_ffn2_w_gate, m_ffn2_w_up, m_ffn2_w_down, m_final_norm, v_ffn1_norm, v_ffn1_w_gate, v_ffn1_w_up, v_ffn1_w_down, v_mix_norm, v_w_in, v_fox_forget_bias, v_mla_q_norm, v_mla_kv_norm, v_mla_w_uq, v_mla_w_ukv, v_w_out, v_ffn2_norm, v_ffn2_w_gate, v_ffn2_w_up, v_ffn2_w_down, v_final_norm):
    p = dict(locals())
    s, d = x.shape[1], x.shape[2]
    depth = ffn1_norm.shape[0]
    dm = _Dims(d, 4 * w_in.shape[2])
    tabs = _rope_tables(s)
    c_arr = lax.axis_index("c").astype(jnp.int32).reshape(1)
    chip_arr = (2 * lax.axis_index("x") + lax.axis_index("y")).astype(jnp.int32).reshape(1)

    shards = {n: p[n].astype(BF16) for n in BIG}
    stacked = [dict() for _ in range(depth)]

    def gather_task(names, l):
        halves = [shards[n][l].reshape(2, shards[n].shape[1] // 2, shards[n].shape[2]) for n in names]
        outs = yield _gather_comm("ag_" + names[0], halves)
        for n, o in zip(names, outs):
            stacked[l][n] = o.reshape(4, shards[n].shape[1], shards[n].shape[2])

    def launch_gathers(queue, l):
        for names in GATHER_BUNDLES:
            queue.add(gather_task(names, l))

    gate_lanes = ((0, 0), (FORGET_LANE, LANES - FORGET_LANE - dm.fox_h))
    gate_mask = jnp.pad(jnp.ones((1, dm.fox_h), F32), gate_lanes)

    def layer_weights(l):
        lw = {}
        for n in BIG:
            st = stacked[l][n]
            lw[n] = (st.reshape(-1, st.shape[-1]) if n in ROW_SHARDED
                     else jnp.transpose(st, (1, 0, 2)).reshape(st.shape[1], -1))
        w_in_al = dm.align_w_in(lw["w_in"])
        lw["w_in_main"] = w_in_al[:, :dm.main_w]
        lw["w_in_tail"] = w_in_al[:, dm.main_w:]
        lw["mla_w_uq"] = dm.pad_w_uq(lw["mla_w_uq"])
        for n in ("ffn1_norm", "mix_norm", "mla_q_norm", "mla_kv_norm", "ffn2_norm"):
            lw[n] = p[n][l][None, :]
        lw["gate_bias"] = jnp.pad(fox_forget_bias[l][None, :], gate_lanes)
        lw["gate_mask"] = gate_mask
        return lw

    fq = _CommQueue()
    launch_gathers(fq, 0)
    fq.drain()
    xs = x[0]
    saved, layers = [], []
    for l in range(depth):
        lw = layer_weights(l)
        layers.append(lw)
        if l + 1 < depth:
            launch_gathers(fq, l + 1)
        xs, s1 = _ffn_forward("ffn1", xs, lw["ffn1_norm"], lw["ffn1_w_gate"], lw["ffn1_w_up"], lw["ffn1_w_down"], fq)
        xs, s2 = _mix_forward(dm, tabs, xs, lw, s, fq)
        xs, s3 = _ffn_forward("ffn2", xs, lw["ffn2_norm"], lw["ffn2_w_gate"], lw["ffn2_w_up"], lw["ffn2_w_down"], fq)
        saved.append((s1, s2, s3))
        fq.drain()
    dx, dxb, dg_final, loss_blk = _loss_head("loss_head", xs, final_norm[None, :], loss_target[0])

    mines = {n: [None] * depth for n in BIG}
    others = {n: [None] * depth for n in BIG}

    def halves_view(n, g):
        rr, cc = g.shape
        if n in ROW_SHARDED:
            return jnp.transpose(g.reshape(4, 2, rr // 8, cc), (1, 0, 2, 3)).reshape(2, rr // 2, cc)
        return g.reshape(2, rr // 2, cc)

    def reduce_task(names, l, gl):
        views = [halves_view(n, gl[n]) for n in names]
        recvs = yield _sibling_comm("rs_sibling_" + names[0], views)
        for n, view, recv in zip(names, views, recvs):
            pair = _add_half("rs_pair_" + n, view, recv, c_arr)
            rr, cc = pair.shape[1], pair.shape[2]
            if n in ROW_SHARDED:
                pst = jnp.transpose(pair.reshape(1, 4, rr // 4, cc), (1, 0, 2, 3))
            else:
                pst = jnp.transpose(pair.reshape(1, rr, 4, cc // 4), (2, 0, 1, 3))
            (got,) = yield _chips_comm("rs_chips_" + n, [pst])
            mines[n][l] = _sum_chips("rs_sum_" + n, pst, got, chip_arr)

    def swap_layer(l):
        outs = _run_comm(_swap_comm("rs_swap", [mines[n][l] for n in BIG]))
        for n, o in zip(BIG, outs):
            others[n][l] = o

    bq = _CommQueue()
    g_layers = [None] * depth
    for l in reversed(range(depth)):
        lw = layers[l]
        s1, s2, s3 = saved[l]
        gl = {}
        dx, dxb, gl["ffn2_norm"], gl["ffn2_w_gate"], gl["ffn2_w_up"], gl["ffn2_w_down"] = _ffn_backward(
            "ffn2", dx, dxb, s3, lw["ffn2_norm"], lw["ffn2_w_gate"], lw["ffn2_w_up"], lw["ffn2_w_down"], bq)
        dx, dxb, gm = _mix_backward(dm, tabs, dx, dxb, s2, lw, s, bq)
        gl.update(gm)
        gl["w_in"] = dm.unalign_w_in(gl["w_in"])
        gl["mla_w_uq"] = dm.unpad_w_uq(gl["mla_w_uq"])
        dx, dxb, gl["ffn1_norm"], gl["ffn1_w_gate"], gl["ffn1_w_up"], gl["ffn1_w_down"] = _ffn_backward(
            "ffn1", dx, dxb, s1, lw["ffn1_norm"], lw["ffn1_w_gate"], lw["ffn1_w_up"], lw["ffn1_w_down"], bq)
        g_layers[l] = gl
        bq.drain()
        if l + 1 < depth:
            swap_layer(l + 1)
        for names in REDUCE_BUNDLES:
            bq.add(reduce_task(names, l, gl))
    bq.drain()
    swap_layer(0)
    grad_x = dx[None]

    grads, deltas, new_m, new_v = {}, {}, {}, {}
    for n in BIG:
        grads[n], deltas[n], new_m[n], new_v[n] = _adamw_layers(
            "adamw_" + n, p[n], mines[n], others[n], p["m_" + n], p["v_" + n], c_arr)

    def pack(get):
        flat = lambda n: jnp.pad(get(n).reshape(1, -1), ((0, 0), (0, d - get(n).size)))
        rows = [get("ffn1_norm"), get("mix_norm"), get("ffn2_norm"), get("final_norm")[None, :],
                flat("mla_q_norm"), flat("mla_kv_norm"), flat("fox_forget_bias")]
        return jnp.concatenate(rows, axis=0)

    def unpack(blk):
        o = 3 * depth
        return {"ffn1_norm": blk[0:depth], "mix_norm": blk[depth:2 * depth], "ffn2_norm": blk[2 * depth:o],
                "final_norm": blk[o], "mla_q_norm": blk[o + 1, :depth * dm.rank].reshape(depth, dm.rank),
                "mla_kv_norm": blk[o + 2, :depth * dm.rank].reshape(depth, dm.rank),
                "fox_forget_bias": blk[o + 3, :depth * dm.fox_h].reshape(depth, dm.fox_h)}

    assert depth * dm.rank <= d
    local_small = {n: (dg_final[0] if n == "final_norm" else jnp.stack([g_layers[l][n] for l in range(depth)]))
                   for n in SMALL}
    n_rows = 3 * depth + 4
    pad_rows = -(n_rows + 1) % 8
    blk = jnp.concatenate([pack(lambda n: local_small[n]), jnp.broadcast_to(loss_blk[0:1, 0:1], (1, d)),
                           jnp.zeros((pad_rows, d), F32)], axis=0)
    everyone = _gather_all("small_gather", blk)[:, None]
    total = _sum_slots("small_sum", everyone, tuple(range(8)))[0]
    loss = total[n_rows, 0]
    small_g = unpack(total)
    w_blk = jnp.concatenate([pack(lambda n: p[n]), jnp.zeros((pad_rows + 1, d), F32)], axis=0)
    m_blk = jnp.concatenate([pack(lambda n: p["m_" + n]), jnp.zeros((pad_rows + 1, d), F32)], axis=0)
    v_blk = jnp.concatenate([pack(lambda n: p["v_" + n]), jnp.zeros((pad_rows + 1, d), F32)], axis=0)
    g_blk = jnp.concatenate([total[:n_rows], jnp.zeros((pad_rows + 1, d), F32)], axis=0)
    d_blk, nm_blk, nv_blk = _adamw("adamw_small", w_blk[None], g_blk[None], m_blk[None], v_blk[None])
    small_d, small_m, small_v = unpack(d_blk[0]), unpack(nm_blk[0]), unpack(nv_blk[0])
    for n in SMALL:
        grads[n], deltas[n], new_m[n], new_v[n] = small_g[n], small_d[n], small_m[n], small_v[n]

    return (loss, grad_x, *[grads[n] for n in WEIGHTS], *[deltas[n] for n in WEIGHTS],
            *[new_m[n] for n in WEIGHTS], *[new_v[n] for n in WEIGHTS])
```

```python
import functools

import jax
import jax.numpy as jnp
from jax import lax
from jax.experimental import pallas as pl
from jax.experimental.pallas import tpu as pltpu

F32 = jnp.float32
BF16 = jnp.bfloat16
MESH = pl.DeviceIdType.MESH
ANY = pl.BlockSpec(memory_space=pl.ANY)

LANES = 128
VMEM_LIMIT_BYTES = 56 * 2 ** 20
HEAD = 128
EPS = 1e-6
NEG = -1e30
ATTN_TILE = 512
ROPE_THETA = 500000.0
MLA_ROPE = 64
PARTIAL_ROPE = HEAD // 4
DIL_BRANCHES = ((128, 1), (512, 4), (2048, 16))
DIL_REACH = max(w for w, _ in DIL_BRANCHES)
FORGET_LANE = MLA_ROPE
ADAM_LR, ADAM_B1, ADAM_B2, ADAM_EPS, ADAM_WD, ADAM_STEP = 0.001, 0.9, 0.999, 1e-08, 0.01, 10

BIG = ("ffn1_w_gate", "ffn1_w_up", "ffn1_w_down", "w_in", "mla_w_uq", "mla_w_ukv", "w_out",
       "ffn2_w_gate", "ffn2_w_up", "ffn2_w_down")
ROW_SHARDED = ("ffn1_w_down", "w_out", "ffn2_w_down")
GATHER_BUNDLES = (("ffn1_w_gate", "ffn1_w_up"), ("ffn1_w_down",), ("mla_w_uq", "mla_w_ukv", "w_out"), ("w_in",),
                  ("ffn2_w_gate", "ffn2_w_up"), ("ffn2_w_down",))
REDUCE_BUNDLES = (("ffn2_w_gate", "ffn2_w_up", "ffn2_w_down"), ("ffn1_w_gate", "ffn1_w_up", "ffn1_w_down"),
                  ("w_in", "w_out", "mla_w_uq", "mla_w_ukv"))
SMALL = ("ffn1_norm", "mix_norm", "fox_forget_bias", "mla_q_norm", "mla_kv_norm", "ffn2_norm", "final_norm")
WEIGHTS = ("ffn1_norm", "ffn1_w_gate", "ffn1_w_up", "ffn1_w_down", "mix_norm", "w_in", "fox_forget_bias",
           "mla_q_norm", "mla_kv_norm", "mla_w_uq", "mla_w_ukv", "w_out", "ffn2_norm", "ffn2_w_gate",
           "ffn2_w_up", "ffn2_w_down", "final_norm")


def _params(*sem):
    return pltpu.CompilerParams(dimension_semantics=sem, vmem_limit_bytes=VMEM_LIMIT_BYTES)


def _div_tile(n, cap):
    if n <= cap:
        return n
    best = 0
    for t in range(LANES, cap + 1, LANES):
        if n % t == 0:
            best = t
    assert best, (n, cap)
    return best


MM_TILE_BUDGET_BYTES = 34 * 2 ** 20


def _mm_tiles(m, n, k, pairs, out_bytes, has_res):
    def cands(x, cap):
        return [x] if x <= LANES else [t for t in range(LANES, min(x, cap) + 1, LANES) if x % t == 0]

    best = None
    for tm in cands(m, 1024):
        for tn in cands(n, 1408):
            for tk in cands(k, 2048):
                need = (2 * 2 * pairs * (tm * tk + tk * tn) + 4 * tm * tn + 2 * out_bytes * tm * tn
                        + (2 * 4 * tm * tn if has_res else 0))
                if need > MM_TILE_BUDGET_BYTES:
                    continue
                key = (-(1.0 / tm + 1.0 / tn), tk)
                if best is None or key > best[0]:
                    best = (key, (tm, tn, tk))
    assert best is not None, (m, n, k)
    return best[1]


def _mm(name, a, b, *, ta=False, tb=False, a2=None, b2=None, res=None, alpha=1.0, out_dtype=BF16, queue=None):
    m, k = (a.shape[1], a.shape[0]) if ta else a.shape
    n, kb = (b.shape[0], b.shape[1]) if tb else (b.shape[1], b.shape[0])
    assert k == kb, (name, a.shape, b.shape)
    pairs = 1 if a2 is None else 2
    if pairs == 2:
        assert a2.shape == a.shape and b2.shape == b.shape
    tm, tn, tk = _mm_tiles(m, n, k, pairs, jnp.dtype(out_dtype).itemsize, res is not None)
    nk = k // tk
    a_spec = (pl.BlockSpec((tk, tm), lambda i, j, kk: (kk, i)) if ta
              else pl.BlockSpec((tm, tk), lambda i, j, kk: (i, kk)))
    b_spec = (pl.BlockSpec((tn, tk), lambda i, j, kk: (j, kk)) if tb
              else pl.BlockSpec((tk, tn), lambda i, j, kk: (kk, j)))
    dims = (((0 if ta else 1,), (1 if tb else 0,)), ((), ()))
    operands, specs = [a, b], [a_spec, b_spec]
    if pairs == 2:
        operands += [a2, b2]
        specs += [a_spec, b_spec]
    n_mm = len(operands)
    if res is not None:
        operands.append(res)
        specs.append(pl.BlockSpec((tm, tn), lambda i, j, kk: (i, j)))

    def body(*refs):
        o_ref, acc_ref = refs[-2], refs[-1]
        kk = pl.program_id(2)

        @pl.when(kk == 0)
        def _():
            acc_ref[...] = jnp.zeros_like(acc_ref)

        acc_ref[...] += lax.dot_general(refs[0][...], refs[1][...], dims, preferred_element_type=F32)
        if pairs == 2:
            acc_ref[...] += lax.dot_general(refs[2][...], refs[3][...], dims, preferred_element_type=F32)

        @pl.when(kk == nk - 1)
        def _():
            out = acc_ref[...] * alpha
            if res is not None:
                out = out + refs[n_mm][...].astype(F32)
            o_ref[...] = out.astype(o_ref.dtype)

    return _call(body, name, (m // tm, n // tn, nk), specs,
                 [pl.BlockSpec((tm, tn), lambda i, j, kk: (i, j))], [jax.ShapeDtypeStruct((m, n), out_dtype)],
                 [pltpu.VMEM((tm, tn), F32)], ("parallel", "parallel", "arbitrary"), operands, queue)[0]


def _ffn_up(name, h, wg, wu, queue=None):
    s, d = h.shape
    f = wg.shape[1]
    tm, tn = _div_tile(s, 512), _div_tile(f, 512)

    def body(h_ref, wg_ref, wu_ref, a_ref, b_ref, z_ref):
        hv = h_ref[...]
        a = jnp.dot(hv, wg_ref[...], preferred_element_type=F32)
        b = jnp.dot(hv, wu_ref[...], preferred_element_type=F32)
        a_ref[...] = a.astype(BF16)
        b_ref[...] = b.astype(BF16)
        z_ref[...] = (a * jax.nn.sigmoid(a) * b).astype(BF16)

    tile = pl.BlockSpec((tm, tn), lambda i, j: (i, j))
    w_spec = pl.BlockSpec((d, tn), lambda i, j: (0, j))
    return _call(body, name, (s // tm, f // tn), [pl.BlockSpec((tm, d), lambda i, j: (i, 0)), w_spec, w_spec],
                 [tile, tile, tile], [jax.ShapeDtypeStruct((s, f), BF16)] * 3, [], ("parallel", "parallel"),
                 [h, wg, wu], queue)


def _ffn_bwd_mid(name, dy, wd, a, b, alpha, queue=None):
    s, d = dy.shape
    f = wd.shape[0]
    tm, tn = _div_tile(s, 512), _div_tile(f, 512)

    def body(dy_ref, wd_ref, a_ref, b_ref, da_ref, db_ref):
        dz = lax.dot_general(dy_ref[...], wd_ref[...], (((1,), (1,)), ((), ())),
                             preferred_element_type=F32) * alpha
        av = a_ref[...].astype(F32)
        sg = jax.nn.sigmoid(av)
        db_ref[...] = (dz * av * sg).astype(BF16)
        da_ref[...] = (dz * b_ref[...].astype(F32) * (sg * (1.0 + av * (1.0 - sg)))).astype(BF16)

    tile = pl.BlockSpec((tm, tn), lambda i, j: (i, j))
    return _call(body, name, (s // tm, f // tn),
                 [pl.BlockSpec((tm, d), lambda i, j: (i, 0)), pl.BlockSpec((tn, d), lambda i, j: (j, 0)), tile, tile],
                 [tile, tile], [jax.ShapeDtypeStruct((s, f), BF16)] * 2, [], ("parallel", "parallel"),
                 [dy, wd, a, b], queue)


def _rms_fwd(name, x, g, *, width=None, col=0):
    s = x.shape[0]
    w = x.shape[1] if width is None else width
    assert col % w == 0
    cb, tr = col // w, min(256, s)

    def body(x_ref, g_ref, h_ref):
        xf = x_ref[...].astype(F32)
        r = lax.rsqrt(jnp.mean(xf * xf, axis=-1, keepdims=True) + EPS)
        h_ref[...] = (xf * r * g_ref[...]).astype(BF16)

    return pl.pallas_call(
        body, name=name, grid=(s // tr,),
        in_specs=[pl.BlockSpec((tr, w), lambda i: (i, cb)), pl.BlockSpec((1, w), lambda i: (0, 0))],
        out_specs=pl.BlockSpec((tr, w), lambda i: (i, 0)),
        out_shape=jax.ShapeDtypeStruct((s, w), BF16),
        compiler_params=_params("parallel"))(x, g)


def _rms_bwd(name, dh, x, g, *, res=None, width=None, col=0):
    s = x.shape[0]
    w = x.shape[1] if width is None else width
    assert col % w == 0
    cb, tr = col // w, min(256, s)
    has_res = res is not None

    def body(*refs):
        dh_ref, x_ref, g_ref = refs[:3]
        dg_ref = refs[-1]
        i = pl.program_id(0)
        xf = x_ref[...].astype(F32)
        r = lax.rsqrt(jnp.mean(xf * xf, axis=-1, keepdims=True) + EPS)
        xh = xf * r
        d = dh_ref[...].astype(F32)
        dxh = d * g_ref[...]
        dx = r * (dxh - xh * jnp.mean(dxh * xh, axis=-1, keepdims=True))
        if has_res:
            dx = dx + refs[3][...]
            refs[4][...] = dx
            refs[5][...] = dx.astype(BF16)
        else:
            refs[3][...] = dx.astype(BF16)

        @pl.when(i == 0)
        def _():
            dg_ref[...] = jnp.zeros_like(dg_ref)

        dg_ref[0:1, :] += jnp.sum(d * xh, axis=0, keepdims=True)

    row = pl.BlockSpec((tr, w), lambda i: (i, 0))
    in_specs = [row, pl.BlockSpec((tr, w), lambda i: (i, cb)), pl.BlockSpec((1, w), lambda i: (0, 0))]
    operands = [dh, x, g]
    dg_spec = pl.BlockSpec((8, w), lambda i: (0, 0))
    dg_shape = jax.ShapeDtypeStruct((8, w), F32)
    if has_res:
        in_specs.append(row)
        operands.append(res)
        out_specs = [row, row, dg_spec]
        out_shape = [jax.ShapeDtypeStruct((s, w), F32), jax.ShapeDtypeStruct((s, w), BF16), dg_shape]
    else:
        out_specs = [row, dg_spec]
        out_shape = [jax.ShapeDtypeStruct((s, w), BF16), dg_shape]
    return pl.pallas_call(
        body, name=name, grid=(s // tr,), in_specs=in_specs, out_specs=out_specs, out_shape=out_shape,
        compiler_params=_params("arbitrary"))(*operands)


def _loss_head(name, x, g, target):
    s, d = x.shape
    tr = min(256, s)
    n = s // tr

    def body(x_ref, g_ref, t_ref, dx_ref, dxb_ref, dg_ref, loss_ref, sq_ref):
        i = pl.program_id(0)
        xf = x_ref[...]
        r = lax.rsqrt(jnp.mean(xf * xf, axis=-1, keepdims=True) + EPS)
        xh = xf * r
        gv = g_ref[...]
        err = xh * gv - t_ref[...]
        dy = err * (1.0 / d)
        dxh = dy * gv
        dx = r * (dxh - xh * jnp.mean(dxh * xh, axis=-1, keepdims=True))
        dx_ref[...] = dx
        dxb_ref[...] = dx.astype(BF16)

        @pl.when(i == 0)
        def _():
            dg_ref[...] = jnp.zeros_like(dg_ref)
            sq_ref[...] = jnp.zeros_like(sq_ref)

        dg_ref[0:1, :] += jnp.sum(dy * xh, axis=0, keepdims=True)
        sq_ref[...] += jnp.sum(err * err, axis=0, keepdims=True)

        @pl.when(i == n - 1)
        def _():
            total = jnp.sum(sq_ref[...], axis=1, keepdims=True) * (0.5 / d)
            loss_ref[...] = jnp.broadcast_to(total, loss_ref.shape)

    row = pl.BlockSpec((tr, d), lambda i: (i, 0))
    return pl.pallas_call(
        body, name=name, grid=(n,),
        in_specs=[row, pl.BlockSpec((1, d), lambda i: (0, 0)), row],
        out_specs=[row, row, pl.BlockSpec((8, d), lambda i: (0, 0)), pl.BlockSpec((8, LANES), lambda i: (0, 0))],
        out_shape=[jax.ShapeDtypeStruct((s, d), F32), jax.ShapeDtypeStruct((s, d), BF16),
                   jax.ShapeDtypeStruct((8, d), F32), jax.ShapeDtypeStruct((8, LANES), F32)],
        scratch_shapes=[pltpu.VMEM((1, d), F32)],
        compiler_params=_params("arbitrary"))(x, g, target)


def _rope(name, x, tabs, half, *, width, col=0, nblk=1, transpose=False, out_dtype=BF16):
    s = x.shape[0]
    assert col % width == 0
    cb0, tr = col // width, min(256, s)

    def body(x_ref, c_ref, s1_ref, s2_ref, y_ref):
        xf = x_ref[...].astype(F32)
        if transpose:
            y = (xf * c_ref[...] + pltpu.roll(xf * s1_ref[...], half, 1)
                 + pltpu.roll(xf * s2_ref[...], width - half, 1))
        else:
            y = (xf * c_ref[...] + pltpu.roll(xf, width - half, 1) * s1_ref[...]
                 + pltpu.roll(xf, half, 1) * s2_ref[...])
        y_ref[...] = y.astype(y_ref.dtype)

    tab = pl.BlockSpec((tr, width), lambda i, j: (i, 0))
    return pl.pallas_call(
        body, name=name, grid=(s // tr, nblk),
        in_specs=[pl.BlockSpec((tr, width), lambda i, j: (i, cb0 + j)), tab, tab, tab],
        out_specs=pl.BlockSpec((tr, width), lambda i, j: (i, j)),
        out_shape=jax.ShapeDtypeStruct((s, nblk * width), out_dtype),
        compiler_params=_params("parallel", "parallel"))(x, *tabs)


def _split3(v):
    hi = v.astype(BF16)
    r1 = v - hi.astype(F32)
    mid = r1.astype(BF16)
    lo = (r1 - mid.astype(F32)).astype(BF16)
    return hi, mid, lo


def _tri_matmul(tri, v):
    hi, mid, lo = _split3(v)
    return (jnp.dot(tri, hi, preferred_element_type=F32) + jnp.dot(tri, mid, preferred_element_type=F32)
            + jnp.dot(tri, lo, preferred_element_type=F32))


def _log_sigmoid(v):
    return jnp.minimum(v, 0.0) - jnp.log(1.0 + jnp.exp(-jnp.abs(v)))


def _gate_fwd(name, tail, bias, mask):
    s = tail.shape[0]
    tr = min(512, s)

    def body(t_ref, b_ref, m_ref, cum_ref, carry_ref):
        i = pl.program_id(0)

        @pl.when(i == 0)
        def _():
            carry_ref[...] = jnp.zeros_like(carry_ref)

        lf = _log_sigmoid(t_ref[...] + b_ref[...]) * m_ref[...]
        r = lax.broadcasted_iota(jnp.int32, (tr, tr), 0)
        c = lax.broadcasted_iota(jnp.int32, (tr, tr), 1)
        cum = _tri_matmul((r >= c).astype(BF16), lf) + carry_ref[...]
        cum_ref[...] = cum
        carry_ref[...] = cum[tr - 1:tr, :]

    vec = pl.BlockSpec((1, LANES), lambda i: (0, 0))
    return pl.pallas_call(
        body, name=name, grid=(s // tr,),
        in_specs=[pl.BlockSpec((tr, LANES), lambda i: (i, 0)), vec, vec],
        out_specs=pl.BlockSpec((tr, LANES), lambda i: (i, 0)),
        out_shape=jax.ShapeDtypeStruct((s, LANES), F32),
        scratch_shapes=[pltpu.VMEM((1, LANES), F32)],
        compiler_params=_params("arbitrary"))(tail, bias, mask)


def _gate_bwd(name, dcum, tail, bias, mask):
    s = tail.shape[0]
    tr = min(512, s)
    n = s // tr

    def body(dc_ref, t_ref, b_ref, m_ref, dt_ref, db_ref, carry_ref):
        i = pl.program_id(0)

        @pl.when(i == 0)
        def _():
            carry_ref[...] = jnp.zeros_like(carry_ref)
            db_ref[...] = jnp.zeros_like(db_ref)

        r = lax.broadcasted_iota(jnp.int32, (tr, tr), 0)
        c = lax.broadcasted_iota(jnp.int32, (tr, tr), 1)
        dlf = _tri_matmul((r <= c).astype(BF16), dc_ref[...]) + carry_ref[...]
        carry_ref[...] = dlf[0:1, :]
        v = t_ref[...] + b_ref[...]
        dt = dlf * jax.nn.sigmoid(-v) * m_ref[...]
        dt_ref[...] = dt
        db_ref[0:1, :] += jnp.sum(dt, axis=0, keepdims=True)

    vec = pl.BlockSpec((1, LANES), lambda i: (0, 0))
    rev = pl.BlockSpec((tr, LANES), lambda i: (n - 1 - i, 0))
    return pl.pallas_call(
        body, name=name, grid=(n,),
        in_specs=[rev, rev, vec, vec],
        out_specs=[rev, pl.BlockSpec((8, LANES), lambda i: (0, 0))],
        out_shape=[jax.ShapeDtypeStruct((s, LANES), F32), jax.ShapeDtypeStruct((8, LANES), F32)],
        scratch_shapes=[pltpu.VMEM((1, LANES), F32)],
        compiler_params=_params("arbitrary"))(dcum, tail, bias, mask)


NT = (((1,), (1,)), ((), ()))
TN = (((0,), (0,)), ((), ()))


def _weights_and_scores(mode, scale, t, qi, kb, q1, k1, q2, k2, cq, ck, masked=True):
    sc = lax.dot_general(q1, k1, NT, preferred_element_type=F32)
    if q2 is not None:
        sc = sc + lax.dot_general(q2, k2, NT, preferred_element_type=F32)
    sc = sc * scale
    if cq is not None:
        sc = sc + (cq - ck)
    if not masked:
        return sc, None
    row = lax.broadcasted_iota(jnp.int32, (t, t), 0)
    col = lax.broadcasted_iota(jnp.int32, (t, t), 1)
    dist = row - col + (qi - kb) * t
    back = dist >= 0
    if mode == "dil":
        w = jnp.zeros((t, t), F32)
        for window, dil in DIL_BRANCHES:
            w = w + (back & (dist <= window) & ((dist & (dil - 1)) == 0)).astype(F32)
    else:
        w = back.astype(F32)
    return jnp.where(w > 0.0, sc, NEG), w


def _causal_steps(mode, diagonal, step):
    if mode == "dil":
        step(True)
    else:
        pl.when(jnp.logical_not(diagonal))(functools.partial(step, False))
        pl.when(diagonal)(functools.partial(step, True))


def _attn_steps(mode, nh, s, key_major):
    t = min(ATTN_TILE, s)
    nq = s // t
    reach = DIL_REACH // t if mode == "dil" else nq
    rows = []
    if key_major:
        for j in range(nq):
            hi = min(nq - 1, j + reach)
            rows += [(h, qb, j, int(qb == hi)) for h in range(nh) for qb in range(j, hi + 1)]
    else:
        for h in range(nh):
            for i in range(nq):
                lo = max(0, i - reach)
                rows += [(h, i, kb, int(kb == lo)) for kb in range(lo, i + 1)]
    tables = [jnp.asarray([r[k] for r in rows], jnp.int32) for k in range(4)]
    return t, len(rows), tables


N_STEP_TABLES = 4


def _attn_operands(mode, nh, t, ops, hf, qf, kf):
    def cols(width, base, rows):
        return pl.BlockSpec((t, width), lambda *g: (rows(*g), base + hf(*g)))

    if mode == "fox":
        pm, ccol, crow = ops
        return ([pm, pm, pm, ccol, crow],
                [cols(HEAD, 0, qf), cols(HEAD, nh, kf), cols(HEAD, 2 * nh, kf),
                 pl.BlockSpec((None, t, 1), lambda *g: (hf(*g), qf(*g), 0)),
                 pl.BlockSpec((None, 1, t), lambda *g: (hf(*g), 0, kf(*g)))])
    if mode == "mla":
        qb, kv, kr = ops
        return ([qb, kv, kr],
                [cols(2 * HEAD, 0, qf), cols(2 * HEAD, 0, kf), pl.BlockSpec((t, HEAD), lambda *g: (kf(*g), 0))])
    q, k, pm, vbase = ops
    return [q, k, pm], [cols(HEAD, 0, qf), cols(HEAD, 0, kf), cols(HEAD, vbase, kf)]


def _unpack(mode, refs):
    if mode == "fox":
        return refs[0][...], refs[1][...], refs[2][...], None, None, refs[3][...], refs[4][...]
    if mode == "mla":
        q, kv = refs[0][...], refs[1][...]
        return q[:, :HEAD], kv[:, :HEAD], kv[:, HEAD:], q[:, HEAD:], refs[2][...], None, None
    return refs[0][...], refs[1][...], refs[2][...], None, None, None, None


def _attn_fwd(name, mode, nh, s, ops, scale):
    t, n_steps, tables = _attn_steps(mode, nh, s, key_major=False)
    hf = lambda p, ht, qt, kt, et: ht[p]
    qf = lambda p, ht, qt, kt, et: qt[p]
    kf = lambda p, ht, qt, kt, et: kt[p]
    operands, specs = _attn_operands(mode, nh, t, ops, hf, qf, kf)
    n_in = len(operands)

    def body(ht, qt, kt, et, *refs):
        o_ref, lse_ref, m_ref, l_ref, acc_ref = refs[n_in:]
        sid = pl.program_id(0)
        i, kb = qt[sid], kt[sid]

        @pl.when(et[sid] == 1)
        def _():
            m_ref[...] = jnp.full_like(m_ref, NEG)
            l_ref[...] = jnp.zeros_like(l_ref)
            acc_ref[...] = jnp.zeros_like(acc_ref)

        def step(masked):
            q1, k1, v, q2, k2, cq, ck = _unpack(mode, refs)
            sc, w = _weights_and_scores(mode, scale, t, i, kb, q1, k1, q2, k2, cq, ck, masked)
            m_old = m_ref[...]
            m_new = jnp.maximum(m_old, jnp.max(sc, axis=1, keepdims=True))
            p = jnp.exp(sc - m_new)
            if w is not None:
                p = w * p
            a = jnp.exp(m_old - m_new)
            l_ref[...] = a * l_ref[...] + jnp.sum(p, axis=1, keepdims=True)
            acc_ref[...] = a * acc_ref[...] + jnp.dot(p.astype(BF16), v, preferred_element_type=F32)
            m_ref[...] = m_new

        _causal_steps(mode, kb == i, step)

        @pl.when(kb == i)
        def _():
            o_ref[...] = (acc_ref[...] / l_ref[...]).astype(BF16)
            lse_ref[...] = m_ref[...] + jnp.log(l_ref[...])

    grid_spec = pltpu.PrefetchScalarGridSpec(
        num_scalar_prefetch=N_STEP_TABLES, grid=(n_steps,), in_specs=specs,
        out_specs=[pl.BlockSpec((t, HEAD), lambda *g: (qf(*g), hf(*g))),
                   pl.BlockSpec((None, t, 1), lambda *g: (hf(*g), qf(*g), 0))],
        scratch_shapes=[pltpu.VMEM((t, 1), F32), pltpu.VMEM((t, 1), F32), pltpu.VMEM((t, HEAD), F32)])
    return pl.pallas_call(
        body, name=name, grid_spec=grid_spec,
        out_shape=[jax.ShapeDtypeStruct((s, nh * HEAD), BF16), jax.ShapeDtypeStruct((nh, s, 1), F32)],
        compiler_params=_params("arbitrary"))(*tables, *operands)


def _attn_bwd_q(name, mode, nh, s, ops, scale, o, do, do_base, lse):
    t, n_steps, tables = _attn_steps(mode, nh, s, key_major=False)
    hf = lambda p, ht, qt, kt, et: ht[p]
    qf = lambda p, ht, qt, kt, et: qt[p]
    kf = lambda p, ht, qt, kt, et: kt[p]
    operands, specs = _attn_operands(mode, nh, t, ops, hf, qf, kf)
    operands += [o, do, lse]
    specs += [pl.BlockSpec((t, HEAD), lambda *g: (qf(*g), do_base + hf(*g))),
              pl.BlockSpec((t, HEAD), lambda *g: (qf(*g), do_base + hf(*g))),
              pl.BlockSpec((None, t, 1), lambda *g: (hf(*g), qf(*g), 0))]
    n_in = len(operands)
    wq = 2 * HEAD if mode == "mla" else HEAD

    fox = mode == "fox"

    def body(ht, qt, kt, et, *refs):
        o_ref, do_ref, lse_ref = refs[n_in - 3:n_in]
        dq_ref = refs[n_in]
        acc_ref, delta_ref, rows_ref = refs[-3:]
        sid = pl.program_id(0)
        i, kb = qt[sid], kt[sid]

        @pl.when(et[sid] == 1)
        def _():
            acc_ref[...] = jnp.zeros_like(acc_ref)
            rows_ref[...] = jnp.zeros_like(rows_ref)
            delta_ref[...] = jnp.sum(o_ref[...].astype(F32) * do_ref[...].astype(F32), axis=1, keepdims=True)

        def step(masked):
            q1, k1, v, q2, k2, cq, ck = _unpack(mode, refs)
            sc, w = _weights_and_scores(mode, scale, t, i, kb, q1, k1, q2, k2, cq, ck, masked)
            p = jnp.exp(sc - lse_ref[...])
            if w is not None:
                p = w * p
            dp = lax.dot_general(do_ref[...], v, NT, preferred_element_type=F32)
            ds32 = p * (dp - delta_ref[...])
            ds = ds32.astype(BF16)
            if fox:
                rows_ref[...] += jnp.sum(ds32, axis=1, keepdims=True)
            if mode == "mla":
                acc_ref[:, :HEAD] += jnp.dot(ds, k1, preferred_element_type=F32)
                acc_ref[:, HEAD:] += jnp.dot(ds, k2, preferred_element_type=F32)
            else:
                acc_ref[...] += jnp.dot(ds, k1, preferred_element_type=F32)

        _causal_steps(mode, kb == i, step)

        @pl.when(kb == i)
        def _():
            dq_ref[...] = (acc_ref[...] * scale).astype(BF16)
            if fox:
                refs[n_in + 1][...] = rows_ref[...]

    out_specs = [pl.BlockSpec((t, wq), lambda *g: (qf(*g), hf(*g)))]
    out_shape = [jax.ShapeDtypeStruct((s, nh * wq), BF16)]
    if fox:
        out_specs.append(pl.BlockSpec((None, t, 1), lambda *g: (hf(*g), qf(*g), 0)))
        out_shape.append(jax.ShapeDtypeStruct((nh, s, 1), F32))
    grid_spec = pltpu.PrefetchScalarGridSpec(
        num_scalar_prefetch=N_STEP_TABLES, grid=(n_steps,), in_specs=specs, out_specs=out_specs,
        scratch_shapes=[pltpu.VMEM((t, wq), F32), pltpu.VMEM((t, 1), F32), pltpu.VMEM((t, 1), F32)])
    out = pl.pallas_call(body, name=name, grid_spec=grid_spec, out_shape=out_shape,
                         compiler_params=_params("arbitrary"))(*tables, *operands)
    return out if fox else out[0]


def _attn_bwd_kv(name, mode, nh, s, ops, scale, o, do, do_base, lse):
    t, n_steps, tables = _attn_steps(mode, nh, s, key_major=True)
    hf = lambda p, ht, qt, kt, et: ht[p]
    qf = lambda p, ht, qt, kt, et: qt[p]
    kf = lambda p, ht, qt, kt, et: kt[p]
    operands, specs = _attn_operands(mode, nh, t, ops, hf, qf, kf)
    operands += [o, do, lse]
    specs += [pl.BlockSpec((t, HEAD), lambda *g: (qf(*g), do_base + hf(*g))),
              pl.BlockSpec((t, HEAD), lambda *g: (qf(*g), do_base + hf(*g))),
              pl.BlockSpec((None, t, 1), lambda *g: (hf(*g), qf(*g), 0))]
    n_in = len(operands)
    head_tile = pl.BlockSpec((t, HEAD), lambda *g: (kf(*g), hf(*g)))
    if mode == "fox":
        out_specs = [head_tile, head_tile, pl.BlockSpec((None, 1, t), lambda *g: (hf(*g), 0, kf(*g)))]
        out_shape = [jax.ShapeDtypeStruct((s, nh * HEAD), BF16)] * 2 + [jax.ShapeDtypeStruct((nh, 1, s), F32)]
        scratch = [pltpu.VMEM((t, HEAD), F32), pltpu.VMEM((t, HEAD), F32), pltpu.VMEM((1, t), F32)]
    elif mode == "mla":
        out_specs = [pl.BlockSpec((t, 2 * HEAD), lambda *g: (kf(*g), hf(*g))),
                     pl.BlockSpec((t, HEAD), lambda *g: (kf(*g), 0))]
        out_shape = [jax.ShapeDtypeStruct((s, nh * 2 * HEAD), BF16), jax.ShapeDtypeStruct((s, HEAD), F32)]
        scratch = [pltpu.VMEM((t, HEAD), F32), pltpu.VMEM((t, HEAD), F32), pltpu.VMEM((t, HEAD), F32)]
    else:
        out_specs = [head_tile, head_tile]
        out_shape = [jax.ShapeDtypeStruct((s, nh * HEAD), BF16)] * 2
        scratch = [pltpu.VMEM((t, HEAD), F32), pltpu.VMEM((t, HEAD), F32)]
    n_out = len(out_specs)

    def body(ht, qt, kt, et, *refs):
        o_ref, do_ref, lse_ref = refs[n_in - 3:n_in]
        outs = refs[n_in:n_in + n_out]
        dk_acc, dv_acc = refs[n_in + n_out], refs[n_in + n_out + 1]
        extra = refs[n_in + n_out + 2] if mode != "dil" else None
        sid = pl.program_id(0)
        j, h, qb = kt[sid], ht[sid], qt[sid]
        is_first, is_last = qb == j, et[sid] == 1

        @pl.when(is_first)
        def _():
            dk_acc[...] = jnp.zeros_like(dk_acc)
            dv_acc[...] = jnp.zeros_like(dv_acc)
            if mode == "fox":
                extra[...] = jnp.zeros_like(extra)

        if mode == "mla":
            @pl.when(is_first & (h == 0))
            def _():
                extra[...] = jnp.zeros_like(extra)

        def step(masked):
            q1, k1, v, q2, k2, cq, ck = _unpack(mode, refs)
            sc, w = _weights_and_scores(mode, scale, t, qb, j, q1, k1, q2, k2, cq, ck, masked)
            p = jnp.exp(sc - lse_ref[...])
            if w is not None:
                p = w * p
            dov = do_ref[...]
            delta = jnp.sum(o_ref[...].astype(F32) * dov.astype(F32), axis=1, keepdims=True)
            dv_acc[...] += lax.dot_general(p.astype(BF16), dov, TN, preferred_element_type=F32)
            dp = lax.dot_general(dov, v, NT, preferred_element_type=F32)
            ds = p * (dp - delta)
            dsb = ds.astype(BF16)
            dk_acc[...] += lax.dot_general(dsb, q1, TN, preferred_element_type=F32)
            if mode == "mla":
                extra[...] += lax.dot_general(dsb, q2, TN, preferred_element_type=F32)
            if mode == "fox":
                extra[...] -= jnp.sum(ds, axis=0, keepdims=True)

        _causal_steps(mode, is_first, step)

        @pl.when(is_last)
        def _():
            if mode == "mla":
                outs[0][:, :HEAD] = (dk_acc[...] * scale).astype(BF16)
                outs[0][:, HEAD:] = dv_acc[...].astype(BF16)
            else:
                outs[0][...] = (dk_acc[...] * scale).astype(BF16)
                outs[1][...] = dv_acc[...].astype(BF16)
            if mode == "fox":
                outs[2][...] = extra[...]

        if mode == "mla":
            @pl.when(is_last & (h == nh - 1))
            def _():
                outs[1][...] = extra[...] * scale

    grid_spec = pltpu.PrefetchScalarGridSpec(
        num_scalar_prefetch=N_STEP_TABLES, grid=(n_steps,), in_specs=specs, out_specs=out_specs,
        scratch_shapes=scratch)
    return pl.pallas_call(body, name=name, grid_spec=grid_spec, out_shape=out_shape,
                          compiler_params=_params("arbitrary"))(*tables, *operands)


ROW_TILE_BYTES = 2 ** 20


def _row_tile(rows, cols):
    best = 0
    for t in range(16, rows + 1, 16):
        if rows % t == 0 and t * cols * 4 <= ROW_TILE_BYTES:
            best = t
    return best if best else rows


def _adamw(name, w, g, m, v):
    nl, r, c = w.shape
    tr = _row_tile(r, c)

    def body(w_ref, g_ref, m_ref, v_ref, d_ref, nm_ref, nv_ref):
        gv = g_ref[...]
        nm = ADAM_B1 * m_ref[...] + (1.0 - ADAM_B1) * gv
        nv = ADAM_B2 * v_ref[...] + (1.0 - ADAM_B2) * jnp.square(gv)
        m_hat = nm / (1.0 - ADAM_B1 ** ADAM_STEP)
        v_hat = nv / (1.0 - ADAM_B2 ** ADAM_STEP)
        d_ref[...] = -ADAM_LR * (m_hat / (jnp.sqrt(v_hat) + ADAM_EPS) + ADAM_WD * w_ref[...])
        nm_ref[...] = nm
        nv_ref[...] = nv

    blk = pl.BlockSpec((None, tr, c), lambda l, i: (l, i, 0))
    return pl.pallas_call(
        body, name=name, grid=(nl, r // tr), in_specs=[blk] * 4, out_specs=[blk] * 3,
        out_shape=[jax.ShapeDtypeStruct(w.shape, F32)] * 3,
        compiler_params=_params("parallel", "parallel"))(w, g, m, v)


def _add_half(name, g, recv, c_arr):
    nl, r, c = g.shape
    hl = nl // 2
    tr = _row_tile(r, c)

    def body(c_ref, g_ref, r_ref, o_ref):
        o_ref[...] = (g_ref[...] + r_ref[...]).astype(BF16)

    grid_spec = pltpu.PrefetchScalarGridSpec(
        num_scalar_prefetch=1, grid=(hl, r // tr),
        in_specs=[pl.BlockSpec((None, tr, c), lambda l, i, c_ref: (c_ref[0] * hl + l, i, 0)),
                  pl.BlockSpec((None, tr, c), lambda l, i, c_ref: (l, i, 0))],
        out_specs=pl.BlockSpec((None, tr, c), lambda l, i, c_ref: (l, i, 0)))
    return pl.pallas_call(
        body, name=name, grid_spec=grid_spec, out_shape=jax.ShapeDtypeStruct((hl, r, c), BF16),
        compiler_params=_params("parallel", "parallel"))(c_arr, g, recv)


def _sum_chips(name, pst, got, chip_arr):
    _, nl, r, c = pst.shape
    tr = _row_tile(r, c)

    def body(chip_ref, own_ref, g0_ref, g1_ref, g2_ref, o_ref):
        o_ref[...] = ((own_ref[...].astype(F32) + g0_ref[...].astype(F32)) + g1_ref[...].astype(F32)
                      ) + g2_ref[...].astype(F32)

    def slot(k):
        return pl.BlockSpec((None, None, tr, c), lambda l, i, chip_ref: (k, l, i, 0))

    grid_spec = pltpu.PrefetchScalarGridSpec(
        num_scalar_prefetch=1, grid=(nl, r // tr),
        in_specs=[pl.BlockSpec((None, None, tr, c), lambda l, i, chip_ref: (chip_ref[0], l, i, 0)),
                  slot(0), slot(1), slot(2)],
        out_specs=pl.BlockSpec((None, tr, c), lambda l, i, chip_ref: (l, i, 0)))
    return pl.pallas_call(
        body, name=name, grid_spec=grid_spec, out_shape=jax.ShapeDtypeStruct((nl, r, c), F32),
        compiler_params=_params("parallel", "parallel"))(chip_arr, pst, got, got, got)


def _adamw_layers(name, w, mines, others, m, v, c_arr):
    nl, r, c = w.shape
    hr = r // 2
    tr = _row_tile(hr, c)
    nb = hr // tr

    def body(c_ref, w_ref, m_ref, v_ref, *refs):
        halves, (g_ref, d_ref, nm_ref, nv_ref) = refs[:2 * nl], refs[2 * nl:]
        layer, is_mine = pl.program_id(0), pl.program_id(1) == c_ref[0]
        gv = jnp.where(is_mine, halves[0][...], halves[nl][...])
        for k in range(1, nl):
            gv = jnp.where(layer == k, jnp.where(is_mine, halves[k][...], halves[nl + k][...]), gv)
        nm = ADAM_B1 * m_ref[...] + (1.0 - ADAM_B1) * gv
        nv = ADAM_B2 * v_ref[...] + (1.0 - ADAM_B2) * jnp.square(gv)
        m_hat = nm / (1.0 - ADAM_B1 ** ADAM_STEP)
        v_hat = nv / (1.0 - ADAM_B2 ** ADAM_STEP)
        g_ref[...] = gv
        d_ref[...] = -ADAM_LR * (m_hat / (jnp.sqrt(v_hat) + ADAM_EPS) + ADAM_WD * w_ref[...])
        nm_ref[...] = nm
        nv_ref[...] = nv

    def half_of(k):
        return pl.BlockSpec((None, tr, c), lambda l, h, i, c_ref: (0, jnp.where(l == k, i, 0), 0))

    blk = pl.BlockSpec((None, tr, c), lambda l, h, i, c_ref: (l, h * nb + i, 0))
    grid_spec = pltpu.PrefetchScalarGridSpec(
        num_scalar_prefetch=1, grid=(nl, 2, nb),
        in_specs=[blk, blk, blk] + [half_of(k) for k in range(nl)] * 2, out_specs=[blk] * 4)
    return pl.pallas_call(
        body, name=name, grid_spec=grid_spec, out_shape=[jax.ShapeDtypeStruct(w.shape, F32)] * 4,
        compiler_params=_params("parallel", "parallel", "parallel"))(c_arr, w, m, v, *mines, *others)


def _sum_slots(name, rc, order, out_dtype=F32):
    _, nl, r, c = rc.shape
    tr = _row_tile(r, c)

    def body(*refs):
        acc = refs[0][...].astype(F32)
        for ref in refs[1:-1]:
            acc = acc + ref[...].astype(F32)
        refs[-1][...] = acc.astype(out_dtype)

    def slot(k):
        return pl.BlockSpec((None, None, tr, c), lambda l, i: (k, l, i, 0))

    return pl.pallas_call(
        body, name=name, grid=(nl, r // tr), in_specs=[slot(k) for k in order],
        out_specs=pl.BlockSpec((None, tr, c), lambda l, i: (l, i, 0)),
        out_shape=jax.ShapeDtypeStruct((nl, r, c), out_dtype),
        compiler_params=_params("parallel", "parallel"))(*([rc] * len(order)))


class _Comm:
    def __init__(self, name, ins, out_shapes, build, n_first, n_then=0, n_local=0):
        self.name, self.ins, self.out_shapes, self.build = name, list(ins), list(out_shapes), build
        self.n_first, self.n_then, self.n_local = n_first, n_then, n_local


def _comm_steps(comm, in_refs, out_refs, send_sems, recv_sems, local_sems=None):
    def descriptors(with_then):
        x, y, c = lax.axis_index("x"), lax.axis_index("y"), lax.axis_index("c")
        first, then, local = comm.build(in_refs, out_refs, x, y, c)
        assert (len(first), len(then), len(local)) == (comm.n_first, comm.n_then, comm.n_local)

        def remote(k, src, dst, flips):
            fx, fy, fc = flips
            peer = (1 - x if fx else x, 1 - y if fy else y, 1 - c if fc else c)
            return pltpu.make_async_remote_copy(src_ref=src, dst_ref=dst, send_sem=send_sems.at[k],
                                                recv_sem=recv_sems.at[k], device_id=peer, device_id_type=MESH)

        f = [remote(k, s, d, fl) for k, (s, d, fl) in enumerate(first)]
        t = [(remote(len(first) + k, s, d, fl), dep) for k, (s, d, fl, dep) in enumerate(then)] if with_then else []
        lc = [pltpu.make_async_copy(s, d, local_sems.at[k]) for k, (s, d) in enumerate(local)]
        return f, t, lc

    def start():
        f, _, lc = descriptors(False)
        for cp in f + lc:
            cp.start()

    def finish():
        f, t, lc = descriptors(True)
        for k, cp in enumerate(f):
            cp.wait_recv()
            for cp2, dep in t:
                if dep == k:
                    cp2.start()
        for cp2, _ in t:
            cp2.wait_recv()
        for cp in f + [cp2 for cp2, _ in t]:
            cp.wait_send()
        for cp in lc:
            cp.wait()

    return start, finish


def _run_comm(comm):
    n_in, n_out = len(comm.ins), len(comm.out_shapes)

    def body(*refs):
        start, finish = _comm_steps(comm, refs[:n_in], refs[n_in:n_in + n_out], *refs[n_in + n_out:])
        start()
        finish()

    n_sem = comm.n_first + comm.n_then
    return pl.pallas_call(
        body, name=comm.name, in_specs=[ANY] * n_in, out_specs=[ANY] * n_out, out_shape=comm.out_shapes,
        scratch_shapes=[pltpu.SemaphoreType.DMA((n_sem,)), pltpu.SemaphoreType.DMA((n_sem,)),
                        pltpu.SemaphoreType.DMA((max(comm.n_local, 1),))])(*comm.ins)


def _call(body, name, grid, in_specs, out_specs, out_shape, scratch, semantics, operands, queue=None):
    job = queue.take() if queue is not None else None
    if job is None:
        return pl.pallas_call(body, name=name, grid=grid, in_specs=in_specs, out_specs=out_specs,
                              out_shape=out_shape, scratch_shapes=scratch,
                              compiler_params=_params(*semantics))(*operands)
    comm = job[1]
    assert comm.n_local == 0
    n_in, n_out, n_cin, n_cout = len(in_specs), len(out_specs), len(comm.ins), len(comm.out_shapes)
    n_sem = comm.n_first + comm.n_then

    def hosted(*refs):
        a, b = n_in, n_in + n_cin
        c, d = b + n_out, b + n_out + n_cout
        start, finish = _comm_steps(comm, refs[a:b], refs[c:d], refs[-2], refs[-1])
        ids = [pl.program_id(k) for k in range(len(grid))]
        is_first = functools.reduce(jnp.logical_and, [i == 0 for i in ids])
        is_last = functools.reduce(jnp.logical_and, [i == g - 1 for i, g in zip(ids, grid)])
        pl.when(is_first)(start)
        body(*refs[:a], *refs[b:c], *refs[d:len(refs) - 2])
        pl.when(is_last)(finish)

    outs = pl.pallas_call(
        hosted, name=name + "_and_" + comm.name, grid=grid, in_specs=list(in_specs) + [ANY] * n_cin,
        out_specs=list(out_specs) + [ANY] * n_cout, out_shape=list(out_shape) + comm.out_shapes,
        scratch_shapes=list(scratch) + [pltpu.SemaphoreType.DMA((n_sem,)), pltpu.SemaphoreType.DMA((n_sem,))],
        compiler_params=_params(*(("arbitrary",) * len(grid))))(*operands, *comm.ins)
    queue.done(job, outs[n_out:])
    return outs[:n_out]


class _CommQueue:
    def __init__(self):
        self.jobs = []

    def add(self, task):
        self.done([task, None], None, first=True)

    def take(self):
        return self.jobs.pop(0) if self.jobs else None

    def done(self, job, results, first=False):
        try:
            comm = next(job[0]) if first else job[0].send(results)
        except StopIteration:
            return
        self.jobs.append([job[0], comm])

    def drain(self):
        while self.jobs:
            job = self.take()
            self.done(job, _run_comm(job[1]))


CHIP_FLIPS = ((1, 0), (0, 1), (1, 1))


def _flip(v, f):
    return 1 - v if f else v


def _gather_comm(name, shards):
    def build(ins, outs, x, y, c):
        first, then = [], []
        me = 2 * x + y
        for w_ref, out_ref in zip(ins, outs):
            base = len(first)
            for j, (fx, fy) in enumerate(CHIP_FLIPS):
                theirs = out_ref.at[2 * _flip(x, fx) + _flip(y, fy), pl.ds(c, 1)]
                first.append((w_ref.at[pl.ds(c, 1)], out_ref.at[me, pl.ds(c, 1)], (fx, fy, 0)))
                then.append((theirs, theirs, (0, 0, 1), base + j))
            first.append((w_ref, out_ref.at[me], (0, 0, 1)))
        return first, then, []

    n = len(shards)
    return _Comm(name, shards, [jax.ShapeDtypeStruct((4,) + w.shape, w.dtype) for w in shards], build, 4 * n, 3 * n)


def _sibling_comm(name, gs):
    def build(ins, outs, x, y, c):
        return [(g.at[pl.ds(1 - c, 1)], o, (0, 0, 1)) for g, o in zip(ins, outs)], [], []

    return _Comm(name, gs, [jax.ShapeDtypeStruct((1,) + g.shape[1:], g.dtype) for g in gs], build, len(gs))


def _chips_comm(name, psts):
    def build(ins, outs, x, y, c):
        first = []
        for p_ref, o_ref in zip(ins, outs):
            for k, (fx, fy) in enumerate(CHIP_FLIPS):
                first.append((p_ref.at[2 * _flip(x, fx) + _flip(y, fy)], o_ref.at[k], (fx, fy, 0)))
        return first, [], []

    return _Comm(name, psts, [jax.ShapeDtypeStruct((3,) + p.shape[1:], p.dtype) for p in psts], build,
                 3 * len(psts))


def _swap_comm(name, ss):
    def build(ins, outs, x, y, c):
        return [(s, o, (0, 0, 1)) for s, o in zip(ins, outs)], [], []

    return _Comm(name, ss, [jax.ShapeDtypeStruct(s.shape, s.dtype) for s in ss], build, len(ss))


def _gather_all(name, blk):
    def build(ins, outs, x, y, c):
        mine = outs[0].at[4 * x + 2 * y + c]
        first = [(ins[0], mine, (m >> 2 & 1, m >> 1 & 1, m & 1)) for m in range(1, 8)]
        return first, [], [(ins[0], mine)]

    return _run_comm(_Comm(name, [blk], [jax.ShapeDtypeStruct((8,) + blk.shape, blk.dtype)], build, 7, 0, 1))[0]


class _Dims:
    def __init__(self, d, in_w):
        self.d = d
        self.fox_h = (d // 4) // HEAD
        self.mla_h = (d // 2) // HEAD
        self.dil_h = (d // 4) // HEAD
        self.gw = d // 4
        self.rank = d // 4
        gw, fh = self.gw, self.fox_h
        sizes = (gw, gw, gw, fh, self.rank, self.rank, MLA_ROPE, gw, gw, gw)
        assert sum(sizes) == in_w
        offs = [0]
        for z in sizes:
            offs.append(offs[-1] + z)
        self.nat = dict(zip(("fq", "fk", "fv", "fl", "cq", "ckv", "kr", "dq", "dk", "dv"), zip(offs[:-1], sizes)))
        self.main_order = ("fq", "fk", "fv", "cq", "ckv", "dq", "dk", "dv")
        self.main_w = 8 * gw
        self.col = {n: i * gw for i, n in enumerate(self.main_order)}

    def align_w_in(self, w):
        parts = [w[:, self.nat[n][0]:self.nat[n][0] + self.nat[n][1]] for n in self.main_order]
        parts += [w[:, self.nat["kr"][0]:self.nat["kr"][0] + MLA_ROPE],
                  w[:, self.nat["fl"][0]:self.nat["fl"][0] + self.fox_h],
                  jnp.zeros((w.shape[0], LANES - MLA_ROPE - self.fox_h), w.dtype)]
        return jnp.concatenate(parts, axis=1)

    def unalign_w_in(self, g):
        gw, t0 = self.gw, self.main_w
        src = {n: g[:, self.col[n]:self.col[n] + gw] for n in self.main_order}
        src["kr"] = g[:, t0:t0 + MLA_ROPE]
        src["fl"] = g[:, t0 + FORGET_LANE:t0 + FORGET_LANE + self.fox_h]
        return jnp.concatenate([src[n] for n in ("fq", "fk", "fv", "fl", "cq", "ckv", "kr", "dq", "dk", "dv")], axis=1)

    def pad_w_uq(self, w):
        r = w.shape[0]
        w3 = w.reshape(r, self.mla_h, HEAD + MLA_ROPE)
        return jnp.pad(w3, ((0, 0), (0, 0), (0, HEAD - MLA_ROPE))).reshape(r, self.mla_h * 2 * HEAD)

    def unpad_w_uq(self, g):
        r = g.shape[0]
        return g.reshape(r, self.mla_h, 2 * HEAD)[:, :, :HEAD + MLA_ROPE].reshape(r, self.mla_h * (HEAD + MLA_ROPE))


def _rope_tables(s):
    def tables(dim):
        inv = 1.0 / (ROPE_THETA ** (jnp.arange(0, dim, 2, dtype=F32) / dim))
        ang = jnp.arange(s, dtype=F32)[:, None] * inv[None, :]
        return jnp.cos(ang), jnp.sin(ang)

    def build(cos, sin, lead, trail_one, trail_zero):
        z = jnp.zeros_like(sin)
        ones = lambda n: jnp.ones((s, n), F32)
        zeros = lambda n: jnp.zeros((s, n), F32)
        c = jnp.concatenate([ones(lead), cos, cos, ones(trail_one), zeros(trail_zero)], axis=1)
        s1 = jnp.concatenate([zeros(lead), -sin, z, zeros(trail_one + trail_zero)], axis=1)
        s2 = jnp.concatenate([zeros(lead), z, sin, zeros(trail_one + trail_zero)], axis=1)
        return c, s1, s2

    cm, sm = tables(MLA_ROPE)
    cp, sp = tables(PARTIAL_ROPE)
    return {"mla_q": build(cm, sm, HEAD, 0, HEAD - MLA_ROPE),
            "mla_k": build(cm, sm, 0, 0, LANES - MLA_ROPE),
            "dil": build(cp, sp, 0, HEAD - PARTIAL_ROPE, 0)}


def _ffn_forward(tag, x, g, wg, wu, wd, queue):
    h = _rms_fwd(tag + "_norm", x, g)
    a, b, z = _ffn_up(tag + "_up", h, wg, wu, queue)
    y = _mm(tag + "_down", z, wd, res=x, alpha=0.5, out_dtype=F32, queue=queue)
    return y, (x, h, a, b, z)


def _ffn_backward(tag, dx, dxb, saved, g, wg, wu, wd, queue):
    x, h, a, b, z = saved
    da, db = _ffn_bwd_mid(tag + "_bwd_mid", dxb, wd, a, b, 0.5, queue)
    g_wd = _mm(tag + "_dwd", z, dxb, ta=True, alpha=0.5, out_dtype=F32, queue=queue)
    g_wg = _mm(tag + "_dwg", h, da, ta=True, out_dtype=F32, queue=queue)
    g_wu = _mm(tag + "_dwu", h, db, ta=True, out_dtype=F32, queue=queue)
    dh = _mm(tag + "_dh", da, wg, tb=True, a2=db, b2=wu, out_dtype=F32, queue=queue)
    dx, dxb, dg = _rms_bwd(tag + "_norm_bwd", dh, x, g, res=dx)
    return dx, dxb, dg[0], g_wg, g_wu, g_wd


def _mix_forward(dm, tabs, x, lw, s, queue):
    d, gw = dm.d, dm.gw
    h = _rms_fwd("mix_norm", x, lw["mix_norm"])
    pm = _mm("mix_in_main", h, lw["w_in_main"], queue=queue)
    tail = _mm("mix_in_tail", h, lw["w_in_tail"], out_dtype=F32)
    cum = _gate_fwd("fox_gate", tail, lw["gate_bias"], lw["gate_mask"])
    cum_h = cum[:, FORGET_LANE:FORGET_LANE + dm.fox_h].T
    ccol, crow = cum_h[:, :, None], cum_h[:, None, :]
    fox_ops = (pm, ccol, crow)
    out_a, lse_a = _attn_fwd("fox_fwd", "fox", dm.fox_h, s, fox_ops, HEAD ** -0.5)

    cqn = _rms_fwd("mla_q_norm", pm, lw["mla_q_norm"], width=gw, col=dm.col["cq"])
    ckvn = _rms_fwd("mla_kv_norm", pm, lw["mla_kv_norm"], width=gw, col=dm.col["ckv"])
    qb_raw = _mm("mla_uq", cqn, lw["mla_w_uq"])
    qb = _rope("mla_q_rope", qb_raw, tabs["mla_q"], MLA_ROPE // 2, width=2 * HEAD, nblk=dm.mla_h)
    kv = _mm("mla_ukv", ckvn, lw["mla_w_ukv"])
    kr = _rope("mla_k_rope", tail, tabs["mla_k"], MLA_ROPE // 2, width=LANES)
    mla_ops = (qb, kv, kr)
    out_b, lse_b = _attn_fwd("mla_fwd", "mla", dm.mla_h, s, mla_ops, (HEAD + MLA_ROPE) ** -0.5)

    dqr = _rope("dil_q_rope", pm, tabs["dil"], PARTIAL_ROPE // 2, width=HEAD, col=dm.col["dq"], nblk=dm.dil_h)
    dkr = _rope("dil_k_rope", pm, tabs["dil"], PARTIAL_ROPE // 2, width=HEAD, col=dm.col["dk"], nblk=dm.dil_h)
    dil_ops = (dqr, dkr, pm, dm.col["dv"] // HEAD)
    out_c, lse_c = _attn_fwd("dil_fwd", "dil", dm.dil_h, s, dil_ops, HEAD ** -0.5)

    mixed = jnp.concatenate([out_a, out_b, out_c], axis=1)
    y = _mm("mix_out", mixed, lw["w_out"], res=x, out_dtype=F32, queue=queue)
    saved = (x, h, pm, tail, fox_ops, lse_a, cqn, ckvn, mla_ops, lse_b, dil_ops, lse_c, mixed)
    return y, saved


def _mix_backward(dm, tabs, dx, dxb, saved, lw, s, queue):
    x, h, pm, tail, fox_ops, lse_a, cqn, ckvn, mla_ops, lse_b, dil_ops, lse_c, mixed = saved
    gw = dm.gw
    grads = {}
    grads["w_out"] = _mm("mix_dwout", mixed, dxb, ta=True, out_dtype=F32)
    dmixed = _mm("mix_dmixed", dxb, lw["w_out"], tb=True)
    nha, nhb, nhc = dm.fox_h, dm.mla_h, dm.dil_h

    sa = HEAD ** -0.5
    dfq, dcum_q = _attn_bwd_q("fox_bwd_q", "fox", nha, s, fox_ops, sa, mixed, dmixed, 0, lse_a)
    dfk, dfv, dcum = _attn_bwd_kv("fox_bwd_kv", "fox", nha, s, fox_ops, sa, mixed, dmixed, 0, lse_a)

    sb = (HEAD + MLA_ROPE) ** -0.5
    dqb = _attn_bwd_q("mla_bwd_q", "mla", nhb, s, mla_ops, sb, mixed, dmixed, nha, lse_b)
    dkv, dkr = _attn_bwd_kv("mla_bwd_kv", "mla", nhb, s, mla_ops, sb, mixed, dmixed, nha, lse_b)
    dqb_raw = _rope("mla_q_rope_bwd", dqb, tabs["mla_q"], MLA_ROPE // 2, width=2 * HEAD, nblk=nhb, transpose=True)
    grads["mla_w_uq"] = _mm("mla_dwuq", cqn, dqb_raw, ta=True, out_dtype=F32)
    dcqn = _mm("mla_dcqn", dqb_raw, lw["mla_w_uq"], tb=True)
    grads["mla_w_ukv"] = _mm("mla_dwukv", ckvn, dkv, ta=True, out_dtype=F32)
    dckvn = _mm("mla_dckvn", dkv, lw["mla_w_ukv"], tb=True)
    dcq, dg_q = _rms_bwd("mla_q_norm_bwd", dcqn, pm, lw["mla_q_norm"], width=gw, col=dm.col["cq"])
    dckv, dg_kv = _rms_bwd("mla_kv_norm_bwd", dckvn, pm, lw["mla_kv_norm"], width=gw, col=dm.col["ckv"])
    dkr_raw = _rope("mla_k_rope_bwd", dkr, tabs["mla_k"], MLA_ROPE // 2, width=LANES, transpose=True, out_dtype=F32)

    sc = HEAD ** -0.5
    ddqr = _attn_bwd_q("dil_bwd_q", "dil", nhc, s, dil_ops, sc, mixed, dmixed, nha + nhb, lse_c)
    ddkr, ddv = _attn_bwd_kv("dil_bwd_kv", "dil", nhc, s, dil_ops, sc, mixed, dmixed, nha + nhb, lse_c)
    ddq = _rope("dil_q_rope_bwd", ddqr, tabs["dil"], PARTIAL_ROPE // 2, width=HEAD, nblk=nhc, transpose=True)
    ddk = _rope("dil_k_rope_bwd", ddkr, tabs["dil"], PARTIAL_ROPE // 2, width=HEAD, nblk=nhc, transpose=True)

    dcum_lanes = jnp.pad((dcum[:, 0, :] + dcum_q[:, :, 0]).T, ((0, 0), (FORGET_LANE, LANES - FORGET_LANE - nha)))
    dgate, dbias = _gate_bwd("fox_gate_bwd", dcum_lanes, tail, lw["gate_bias"], lw["gate_mask"])
    dtail = (dgate + dkr_raw).astype(BF16)
    dpm = jnp.concatenate([dfq, dfk, dfv, dcq, dckv, ddq, ddk, ddv], axis=1)

    g_main = _mm("mix_dwin_main", h, dpm, ta=True, out_dtype=F32, queue=queue)
    g_tail = _mm("mix_dwin_tail", h, dtail, ta=True, out_dtype=F32)
    grads["w_in"] = jnp.concatenate([g_main, g_tail], axis=1)
    dh = _mm("mix_dh_main", dpm, lw["w_in_main"], tb=True, out_dtype=F32, queue=queue)
    dh = _mm("mix_dh_tail", dtail, lw["w_in_tail"], tb=True, res=dh, out_dtype=F32)
    dx, dxb, dg = _rms_bwd("mix_norm_bwd", dh, x, lw["mix_norm"], res=dx)
    grads["mix_norm"] = dg[0]
    grads["mla_q_norm"] = dg_q[0]
    grads["mla_kv_norm"] = dg_kv[0]
    grads["fox_forget_bias"] = dbias[0, FORGET_LANE:FORGET_LANE + nha]
    return dx, dxb, grads


def kernel(x, ffn1_norm, ffn1_w_gate, ffn1_w_up, ffn1_w_down, mix_norm, w_in, fox_forget_bias, mla_q_norm, mla_kv_norm, mla_w_uq, mla_w_ukv, w_out, ffn2_norm, ffn2_w_gate, ffn2_w_up, ffn2_w_down, final_norm, loss_target, m_ffn1_norm, m_ffn1_w_gate, m_ffn1_w_up, m_ffn1_w_down, m_mix_norm, m_w_in, m_fox_forget_bias, m_mla_q_norm, m_mla_kv_norm, m_mla_w_uq, m_mla_w_ukv, m_w_out, m_ffn2_norm, m_ffn2_w_gate, m_ffn2_w_up, m_ffn2_w_down, m_final_norm, v_ffn1_norm, v_ffn1_w_gate, v_ffn1_w_up, v_ffn1_w_down, v_mix_norm, v_w_in, v_fox_forget_bias, v_mla_q_norm, v_mla_kv_norm, v_mla_w_uq, v_mla_w_ukv, v_w_out, v_ffn2_norm, v_ffn2_w_gate, v_ffn2_w_up, v_ffn2_w_down, v_final_norm):
    p = dict(locals())
    s, d = x.shape[1], x.shape[2]
    depth = ffn1_norm.shape[0]
    dm = _Dims(d, 4 * w_in.shape[2])
    tabs = _rope_tables(s)
    c_arr = lax.axis_index("c").astype(jnp.int32).reshape(1)
    chip_arr = (2 * lax.axis_index("x") + lax.axis_index("y")).astype(jnp.int32).reshape(1)

    shards = {n: p[n].astype(BF16) for n in BIG}
    stacked = [dict() for _ in range(depth)]

    def gather_task(names, l):
        halves = [shards[n][l].reshape(2, shards[n].shape[1] // 2, shards[n].shape[2]) for n in names]
        outs = yield _gather_comm("ag_" + names[0], halves)
        for n, o in zip(names, outs):
            stacked[l][n] = o.reshape(4, shards[n].shape[1], shards[n].shape[2])

    def launch_gathers(queue, l):
        for names in GATHER_BUNDLES:
            queue.add(gather_task(names, l))

    gate_lanes = ((0, 0), (FORGET_LANE, LANES - FORGET_LANE - dm.fox_h))
    gate_mask = jnp.pad(jnp.ones((1, dm.fox_h), F32), gate_lanes)

    def layer_weights(l):
        lw = {}
        for n in BIG:
            st = stacked[l][n]
            lw[n] = (st.reshape(-1, st.shape[-1]) if n in ROW_SHARDED
                     else jnp.transpose(st, (1, 0, 2)).reshape(st.shape[1], -1))
        w_in_al = dm.align_w_in(lw["w_in"])
        lw["w_in_main"] = w_in_al[:, :dm.main_w]
        lw["w_in_tail"] = w_in_al[:, dm.main_w:]
        lw["mla_w_uq"] = dm.pad_w_uq(lw["mla_w_uq"])
        for n in ("ffn1_norm", "mix_norm", "mla_q_norm", "mla_kv_norm", "ffn2_norm"):
            lw[n] = p[n][l][None, :]
        lw["gate_bias"] = jnp.pad(fox_forget_bias[l][None, :], gate_lanes)
        lw["gate_mask"] = gate_mask
        return lw

    fq = _CommQueue()
    launch_gathers(fq, 0)
    fq.drain()
    xs = x[0]
    saved, layers = [], []
    for l in range(depth):
        lw = layer_weights(l)
        layers.append(lw)
        if l + 1 < depth:
            launch_gathers(fq, l + 1)
        xs, s1 = _ffn_forward("ffn1", xs, lw["ffn1_norm"], lw["ffn1_w_gate"], lw["ffn1_w_up"], lw["ffn1_w_down"], fq)
        xs, s2 = _mix_forward(dm, tabs, xs, lw, s, fq)
        xs, s3 = _ffn_forward("ffn2", xs, lw["ffn2_norm"], lw["ffn2_w_gate"], lw["ffn2_w_up"], lw["ffn2_w_down"], fq)
        saved.append((s1, s2, s3))
        fq.drain()
    dx, dxb, dg_final, loss_blk = _loss_head("loss_head", xs, final_norm[None, :], loss_target[0])

    mines = {n: [None] * depth for n in BIG}
    others = {n: [None] * depth for n in BIG}

    def halves_view(n, g):
        rr, cc = g.shape
        if n in ROW_SHARDED:
            return jnp.transpose(g.reshape(4, 2, rr // 8, cc), (1, 0, 2, 3)).reshape(2, rr // 2, cc)
        return g.reshape(2, rr // 2, cc)

    def reduce_task(names, l, gl):
        views = [halves_view(n, gl[n]) for n in names]
        recvs = yield _sibling_comm("rs_sibling_" + names[0], views)
        for n, view, recv in zip(names, views, recvs):
            pair = _add_half("rs_pair_" + n, view, recv, c_arr)
            rr, cc = pair.shape[1], pair.shape[2]
            if n in ROW_SHARDED:
                pst = jnp.transpose(pair.reshape(1, 4, rr // 4, cc), (1, 0, 2, 3))
            else:
                pst = jnp.transpose(pair.reshape(1, rr, 4, cc // 4), (2, 0, 1, 3))
            (got,) = yield _chips_comm("rs_chips_" + n, [pst])
            mines[n][l] = _sum_chips("rs_sum_" + n, pst, got, chip_arr)

    bq = _CommQueue()
    g_layers = [None] * depth
    for l in reversed(range(depth)):
        lw = layers[l]
        s1, s2, s3 = saved[l]
        gl = {}
        dx, dxb, gl["ffn2_norm"], gl["ffn2_w_gate"], gl["ffn2_w_up"], gl["ffn2_w_down"] = _ffn_backward(
            "ffn2", dx, dxb, s3, lw["ffn2_norm"], lw["ffn2_w_gate"], lw["ffn2_w_up"], lw["ffn2_w_down"], bq)
        bq.add(reduce_task(REDUCE_BUNDLES[0], l, gl))
        dx, dxb, gm = _mix_backward(dm, tabs, dx, dxb, s2, lw, s, bq)
        gl.update(gm)
        gl["w_in"] = dm.unalign_w_in(gl["w_in"])
        gl["mla_w_uq"] = dm.unpad_w_uq(gl["mla_w_uq"])
        bq.add(reduce_task(REDUCE_BUNDLES[2], l, gl))
        dx, dxb, gl["ffn1_norm"], gl["ffn1_w_gate"], gl["ffn1_w_up"], gl["ffn1_w_down"] = _ffn_backward(
            "ffn1", dx, dxb, s1, lw["ffn1_norm"], lw["ffn1_w_gate"], lw["ffn1_w_up"], lw["ffn1_w_down"], bq)
        bq.add(reduce_task(REDUCE_BUNDLES[1], l, gl))
        g_layers[l] = gl
    bq.drain()
    swapped = _run_comm(_swap_comm("rs_swap", [mines[n][l] for n in BIG for l in range(depth)]))
    for k, n in enumerate(BIG):
        others[n] = list(swapped[k * depth:(k + 1) * depth])
    grad_x = dx[None]

    grads, deltas, new_m, new_v = {}, {}, {}, {}
    for n in BIG:
        grads[n], deltas[n], new_m[n], new_v[n] = _adamw_layers(
            "adamw_" + n, p[n], mines[n], others[n], p["m_" + n], p["v_" + n], c_arr)

    def pack(get):
        flat = lambda n: jnp.pad(get(n).reshape(1, -1), ((0, 0), (0, d - get(n).size)))
        rows = [get("ffn1_norm"), get("mix_norm"), get("ffn2_norm"), get("final_norm")[None, :],
                flat("mla_q_norm"), flat("mla_kv_norm"), flat("fox_forget_bias")]
        return jnp.concatenate(rows, axis=0)

    def unpack(blk):
        o = 3 * depth
        return {"ffn1_norm": blk[0:depth], "mix_norm": blk[depth:2 * depth], "ffn2_norm": blk[2 * depth:o],
                "final_norm": blk[o], "mla_q_norm": blk[o + 1, :depth * dm.rank].reshape(depth, dm.rank),
                "mla_kv_norm": blk[o + 2, :depth * dm.rank].reshape(depth, dm.rank),
                "fox_forget_bias": blk[o + 3, :depth * dm.fox_h].reshape(depth, dm.fox_h)}

    assert depth * dm.rank <= d
    local_small = {n: (dg_final[0] if n == "final_norm" else jnp.stack([g_layers[l][n] for l in range(depth)]))
                   for n in SMALL}
    n_rows = 3 * depth + 4
    pad_rows = -(n_rows + 1) % 8
    blk = jnp.concatenate([pack(lambda n: local_small[n]), jnp.broadcast_to(loss_blk[0:1, 0:1], (1, d)),
                           jnp.zeros((pad_rows, d), F32)], axis=0)
    everyone = _gather_all("small_gather", blk)[:, None]
    total = _sum_slots("small_sum", everyone, tuple(range(8)))[0]
    loss = total[n_rows, 0]
    small_g = unpack(total)
    w_blk = jnp.concatenate([pack(lambda n: p[n]), jnp.zeros((pad_rows + 1, d), F32)], axis=0)
    m_blk = jnp.concatenate([pack(lambda n: p["m_" + n]), jnp.zeros((pad_rows + 1, d), F32)], axis=0)
    v_blk = jnp.concatenate([pack(lambda n: p["v_" + n]), jnp.zeros((pad_rows + 1, d), F32)], axis=0)
    g_blk = jnp.concatenate([total[:n_rows], jnp.zeros((pad_rows + 1, d), F32)], axis=0)
    d_blk, nm_blk, nv_blk = _adamw("adamw_small", w_blk[None], g_blk[None], m_blk[None], v_blk[None])
    small_d, small_m, small_v = unpack(d_blk[0]), unpack(nm_blk[0]), unpack(nv_blk[0])
    for n in SMALL:
        grads[n], deltas[n], new_m[n], new_v[n] = small_g[n], small_d[n], small_m[n], small_v[n]

    return (loss, grad_x, *[grads[n] for n in WEIGHTS], *[deltas[n] for n in WEIGHTS],
            *[new_m[n] for n in WEIGHTS], *[new_v[n] for n in WEIGHTS])
```

```python
import functools

import jax
import jax.numpy as jnp
from jax import lax
from jax.experimental import pallas as pl
from jax.experimental.pallas import tpu as pltpu

F32 = jnp.float32
BF16 = jnp.bfloat16
MESH = pl.DeviceIdType.MESH
ANY = pl.BlockSpec(memory_space=pl.ANY)

LANES = 128
VMEM_LIMIT_BYTES = 56 * 2 ** 20
HEAD = 128
EPS = 1e-6
NEG = -1e30
ATTN_TILE = 512
ROPE_THETA = 500000.0
MLA_ROPE = 64
PARTIAL_ROPE = HEAD // 4
DIL_BRANCHES = ((128, 1), (512, 4), (2048, 16))
DIL_REACH = max(w for w, _ in DIL_BRANCHES)
FORGET_LANE = MLA_ROPE
ADAM_LR, ADAM_B1, ADAM_B2, ADAM_EPS, ADAM_WD, ADAM_STEP = 0.001, 0.9, 0.999, 1e-08, 0.01, 10

BIG = ("ffn1_w_gate", "ffn1_w_up", "ffn1_w_down", "w_in", "mla_w_uq", "mla_w_ukv", "w_out",
       "ffn2_w_gate", "ffn2_w_up", "ffn2_w_down")
ROW_SHARDED = ("ffn1_w_down", "w_out", "ffn2_w_down")
GATHER_BUNDLES = (("ffn1_w_gate", "ffn1_w_up"), ("ffn1_w_down",), ("mla_w_uq", "mla_w_ukv", "w_out"), ("w_in",),
                  ("ffn2_w_gate", "ffn2_w_up"), ("ffn2_w_down",))
REDUCE_BUNDLES = (("ffn2_w_gate", "ffn2_w_up", "ffn2_w_down"), ("ffn1_w_gate", "ffn1_w_up", "ffn1_w_down"),
                  ("w_in", "w_out", "mla_w_uq", "mla_w_ukv"))
SMALL = ("ffn1_norm", "mix_norm", "fox_forget_bias", "mla_q_norm", "mla_kv_norm", "ffn2_norm", "final_norm")
WEIGHTS = ("ffn1_norm", "ffn1_w_gate", "ffn1_w_up", "ffn1_w_down", "mix_norm", "w_in", "fox_forget_bias",
           "mla_q_norm", "mla_kv_norm", "mla_w_uq", "mla_w_ukv", "w_out", "ffn2_norm", "ffn2_w_gate",
           "ffn2_w_up", "ffn2_w_down", "final_norm")


def _params(*sem):
    return pltpu.CompilerParams(dimension_semantics=sem, vmem_limit_bytes=VMEM_LIMIT_BYTES)


def _div_tile(n, cap):
    if n <= cap:
        return n
    best = 0
    for t in range(LANES, cap + 1, LANES):
        if n % t == 0:
            best = t
    assert best, (n, cap)
    return best


MM_TILE_BUDGET_BYTES = 34 * 2 ** 20


def _mm_tiles(m, n, k, pairs, out_bytes, has_res):
    def cands(x, cap):
        return [x] if x <= LANES else [t for t in range(LANES, min(x, cap) + 1, LANES) if x % t == 0]

    best = None
    for tm in cands(m, 1024):
        for tn in cands(n, 1408):
            for tk in cands(k, 2048):
                need = (2 * 2 * pairs * (tm * tk + tk * tn) + 4 * tm * tn + 2 * out_bytes * tm * tn
                        + (2 * 4 * tm * tn if has_res else 0))
                if need > MM_TILE_BUDGET_BYTES:
                    continue
                key = (-(1.0 / tm + 1.0 / tn), tk)
                if best is None or key > best[0]:
                    best = (key, (tm, tn, tk))
    assert best is not None, (m, n, k)
    return best[1]


def _mm(name, a, b, *, ta=False, tb=False, a2=None, b2=None, res=None, alpha=1.0, out_dtype=BF16, queue=None):
    m, k = (a.shape[1], a.shape[0]) if ta else a.shape
    n, kb = (b.shape[0], b.shape[1]) if tb else (b.shape[1], b.shape[0])
    assert k == kb, (name, a.shape, b.shape)
    pairs = 1 if a2 is None else 2
    if pairs == 2:
        assert a2.shape == a.shape and b2.shape == b.shape
    tm, tn, tk = _mm_tiles(m, n, k, pairs, jnp.dtype(out_dtype).itemsize, res is not None)
    nk = k // tk
    a_spec = (pl.BlockSpec((tk, tm), lambda i, j, kk: (kk, i)) if ta
              else pl.BlockSpec((tm, tk), lambda i, j, kk: (i, kk)))
    b_spec = (pl.BlockSpec((tn, tk), lambda i, j, kk: (j, kk)) if tb
              else pl.BlockSpec((tk, tn), lambda i, j, kk: (kk, j)))
    dims = (((0 if ta else 1,), (1 if tb else 0,)), ((), ()))
    operands, specs = [a, b], [a_spec, b_spec]
    if pairs == 2:
        operands += [a2, b2]
        specs += [a_spec, b_spec]
    n_mm = len(operands)
    if res is not None:
        operands.append(res)
        specs.append(pl.BlockSpec((tm, tn), lambda i, j, kk: (i, j)))

    def body(*refs):
        o_ref, acc_ref = refs[-2], refs[-1]
        kk = pl.program_id(2)

        @pl.when(kk == 0)
        def _():
            acc_ref[...] = jnp.zeros_like(acc_ref)

        acc_ref[...] += lax.dot_general(refs[0][...], refs[1][...], dims, preferred_element_type=F32)
        if pairs == 2:
            acc_ref[...] += lax.dot_general(refs[2][...], refs[3][...], dims, preferred_element_type=F32)

        @pl.when(kk == nk - 1)
        def _():
            out = acc_ref[...] * alpha
            if res is not None:
                out = out + refs[n_mm][...].astype(F32)
            o_ref[...] = out.astype(o_ref.dtype)

    return _call(body, name, (m // tm, n // tn, nk), specs,
                 [pl.BlockSpec((tm, tn), lambda i, j, kk: (i, j))], [jax.ShapeDtypeStruct((m, n), out_dtype)],
                 [pltpu.VMEM((tm, tn), F32)], ("parallel", "parallel", "arbitrary"), operands, queue)[0]


def _ffn_up(name, h, wg, wu, queue=None):
    s, d = h.shape
    f = wg.shape[1]
    tm, tn = _div_tile(s, 512), _div_tile(f, 512)

    def body(h_ref, wg_ref, wu_ref, a_ref, b_ref, z_ref):
        hv = h_ref[...]
        a = jnp.dot(hv, wg_ref[...], preferred_element_type=F32)
        b = jnp.dot(hv, wu_ref[...], preferred_element_type=F32)
        a_ref[...] = a.astype(BF16)
        b_ref[...] = b.astype(BF16)
        z_ref[...] = (a * jax.nn.sigmoid(a) * b).astype(BF16)

    tile = pl.BlockSpec((tm, tn), lambda i, j: (i, j))
    w_spec = pl.BlockSpec((d, tn), lambda i, j: (0, j))
    return _call(body, name, (s // tm, f // tn), [pl.BlockSpec((tm, d), lambda i, j: (i, 0)), w_spec, w_spec],
                 [tile, tile, tile], [jax.ShapeDtypeStruct((s, f), BF16)] * 3, [], ("parallel", "parallel"),
                 [h, wg, wu], queue)


def _ffn_bwd_mid(name, dy, wd, a, b, alpha, queue=None):
    s, d = dy.shape
    f = wd.shape[0]
    tm, tn = _div_tile(s, 512), _div_tile(f, 512)

    def body(dy_ref, wd_ref, a_ref, b_ref, da_ref, db_ref):
        dz = lax.dot_general(dy_ref[...], wd_ref[...], (((1,), (1,)), ((), ())),
                             preferred_element_type=F32) * alpha
        av = a_ref[...].astype(F32)
        sg = jax.nn.sigmoid(av)
        db_ref[...] = (dz * av * sg).astype(BF16)
        da_ref[...] = (dz * b_ref[...].astype(F32) * (sg * (1.0 + av * (1.0 - sg)))).astype(BF16)

    tile = pl.BlockSpec((tm, tn), lambda i, j: (i, j))
    return _call(body, name, (s // tm, f // tn),
                 [pl.BlockSpec((tm, d), lambda i, j: (i, 0)), pl.BlockSpec((tn, d), lambda i, j: (j, 0)), tile, tile],
                 [tile, tile], [jax.ShapeDtypeStruct((s, f), BF16)] * 2, [], ("parallel", "parallel"),
                 [dy, wd, a, b], queue)


def _rms_fwd(name, x, g, *, width=None, col=0):
    s = x.shape[0]
    w = x.shape[1] if width is None else width
    assert col % w == 0
    cb, tr = col // w, min(256, s)

    def body(x_ref, g_ref, h_ref):
        xf = x_ref[...].astype(F32)
        r = lax.rsqrt(jnp.mean(xf * xf, axis=-1, keepdims=True) + EPS)
        h_ref[...] = (xf * r * g_ref[...]).astype(BF16)

    return pl.pallas_call(
        body, name=name, grid=(s // tr,),
        in_specs=[pl.BlockSpec((tr, w), lambda i: (i, cb)), pl.BlockSpec((1, w), lambda i: (0, 0))],
        out_specs=pl.BlockSpec((tr, w), lambda i: (i, 0)),
        out_shape=jax.ShapeDtypeStruct((s, w), BF16),
        compiler_params=_params("parallel"))(x, g)


def _rms_bwd(name, dh, x, g, *, res=None, width=None, col=0):
    s = x.shape[0]
    w = x.shape[1] if width is None else width
    assert col % w == 0
    cb, tr = col // w, min(256, s)
    has_res = res is not None

    def body(*refs):
        dh_ref, x_ref, g_ref = refs[:3]
        dg_ref = refs[-1]
        i = pl.program_id(0)
        xf = x_ref[...].astype(F32)
        r = lax.rsqrt(jnp.mean(xf * xf, axis=-1, keepdims=True) + EPS)
        xh = xf * r
        d = dh_ref[...].astype(F32)
        dxh = d * g_ref[...]
        dx = r * (dxh - xh * jnp.mean(dxh * xh, axis=-1, keepdims=True))
        if has_res:
            dx = dx + refs[3][...]
            refs[4][...] = dx
            refs[5][...] = dx.astype(BF16)
        else:
            refs[3][...] = dx.astype(BF16)

        @pl.when(i == 0)
        def _():
            dg_ref[...] = jnp.zeros_like(dg_ref)

        dg_ref[0:1, :] += jnp.sum(d * xh, axis=0, keepdims=True)

    row = pl.BlockSpec((tr, w), lambda i: (i, 0))
    in_specs = [row, pl.BlockSpec((tr, w), lambda i: (i, cb)), pl.BlockSpec((1, w), lambda i: (0, 0))]
    operands = [dh, x, g]
    dg_spec = pl.BlockSpec((8, w), lambda i: (0, 0))
    dg_shape = jax.ShapeDtypeStruct((8, w), F32)
    if has_res:
        in_specs.append(row)
        operands.append(res)
        out_specs = [row, row, dg_spec]
        out_shape = [jax.ShapeDtypeStruct((s, w), F32), jax.ShapeDtypeStruct((s, w), BF16), dg_shape]
    else:
        out_specs = [row, dg_spec]
        out_shape = [jax.ShapeDtypeStruct((s, w), BF16), dg_shape]
    return pl.pallas_call(
        body, name=name, grid=(s // tr,), in_specs=in_specs, out_specs=out_specs, out_shape=out_shape,
        compiler_params=_params("arbitrary"))(*operands)


def _loss_head(name, x, g, target):
    s, d = x.shape
    tr = min(256, s)
    n = s // tr

    def body(x_ref, g_ref, t_ref, dx_ref, dxb_ref, dg_ref, loss_ref, sq_ref):
        i = pl.program_id(0)
        xf = x_ref[...]
        r = lax.rsqrt(jnp.mean(xf * xf, axis=-1, keepdims=True) + EPS)
        xh = xf * r
        gv = g_ref[...]
        err = xh * gv - t_ref[...]
        dy = err * (1.0 / d)
        dxh = dy * gv
        dx = r * (dxh - xh * jnp.mean(dxh * xh, axis=-1, keepdims=True))
        dx_ref[...] = dx
        dxb_ref[...] = dx.astype(BF16)

        @pl.when(i == 0)
        def _():
            dg_ref[...] = jnp.zeros_like(dg_ref)
            sq_ref[...] = jnp.zeros_like(sq_ref)

        dg_ref[0:1, :] += jnp.sum(dy * xh, axis=0, keepdims=True)
        sq_ref[...] += jnp.sum(err * err, axis=0, keepdims=True)

        @pl.when(i == n - 1)
        def _():
            total = jnp.sum(sq_ref[...], axis=1, keepdims=True) * (0.5 / d)
            loss_ref[...] = jnp.broadcast_to(total, loss_ref.shape)

    row = pl.BlockSpec((tr, d), lambda i: (i, 0))
    return pl.pallas_call(
        body, name=name, grid=(n,),
        in_specs=[row, pl.BlockSpec((1, d), lambda i: (0, 0)), row],
        out_specs=[row, row, pl.BlockSpec((8, d), lambda i: (0, 0)), pl.BlockSpec((8, LANES), lambda i: (0, 0))],
        out_shape=[jax.ShapeDtypeStruct((s, d), F32), jax.ShapeDtypeStruct((s, d), BF16),
                   jax.ShapeDtypeStruct((8, d), F32), jax.ShapeDtypeStruct((8, LANES), F32)],
        scratch_shapes=[pltpu.VMEM((1, d), F32)],
        compiler_params=_params("arbitrary"))(x, g, target)


def _rope(name, x, tabs, half, *, width, col=0, nblk=1, transpose=False, out_dtype=BF16):
    s = x.shape[0]
    assert col % width == 0
    cb0, tr = col // width, min(256, s)

    def body(x_ref, c_ref, s1_ref, s2_ref, y_ref):
        xf = x_ref[...].astype(F32)
        if transpose:
            y = (xf * c_ref[...] + pltpu.roll(xf * s1_ref[...], half, 1)
                 + pltpu.roll(xf * s2_ref[...], width - half, 1))
        else:
            y = (xf * c_ref[...] + pltpu.roll(xf, width - half, 1) * s1_ref[...]
                 + pltpu.roll(xf, half, 1) * s2_ref[...])
        y_ref[...] = y.astype(y_ref.dtype)

    tab = pl.BlockSpec((tr, width), lambda i, j: (i, 0))
    return pl.pallas_call(
        body, name=name, grid=(s // tr, nblk),
        in_specs=[pl.BlockSpec((tr, width), lambda i, j: (i, cb0 + j)), tab, tab, tab],
        out_specs=pl.BlockSpec((tr, width), lambda i, j: (i, j)),
        out_shape=jax.ShapeDtypeStruct((s, nblk * width), out_dtype),
        compiler_params=_params("parallel", "parallel"))(x, *tabs)


def _split3(v):
    hi = v.astype(BF16)
    r1 = v - hi.astype(F32)
    mid = r1.astype(BF16)
    lo = (r1 - mid.astype(F32)).astype(BF16)
    return hi, mid, lo


def _tri_matmul(tri, v):
    hi, mid, lo = _split3(v)
    return (jnp.dot(tri, hi, preferred_element_type=F32) + jnp.dot(tri, mid, preferred_element_type=F32)
            + jnp.dot(tri, lo, preferred_element_type=F32))


def _log_sigmoid(v):
    return jnp.minimum(v, 0.0) - jnp.log(1.0 + jnp.exp(-jnp.abs(v)))


def _gate_fwd(name, tail, bias, mask):
    s = tail.shape[0]
    tr = min(512, s)

    def body(t_ref, b_ref, m_ref, cum_ref, carry_ref):
        i = pl.program_id(0)

        @pl.when(i == 0)
        def _():
            carry_ref[...] = jnp.zeros_like(carry_ref)

        lf = _log_sigmoid(t_ref[...] + b_ref[...]) * m_ref[...]
        r = lax.broadcasted_iota(jnp.int32, (tr, tr), 0)
        c = lax.broadcasted_iota(jnp.int32, (tr, tr), 1)
        cum = _tri_matmul((r >= c).astype(BF16), lf) + carry_ref[...]
        cum_ref[...] = cum
        carry_ref[...] = cum[tr - 1:tr, :]

    vec = pl.BlockSpec((1, LANES), lambda i: (0, 0))
    return pl.pallas_call(
        body, name=name, grid=(s // tr,),
        in_specs=[pl.BlockSpec((tr, LANES), lambda i: (i, 0)), vec, vec],
        out_specs=pl.BlockSpec((tr, LANES), lambda i: (i, 0)),
        out_shape=jax.ShapeDtypeStruct((s, LANES), F32),
        scratch_shapes=[pltpu.VMEM((1, LANES), F32)],
        compiler_params=_params("arbitrary"))(tail, bias, mask)


def _gate_bwd(name, dcum, tail, bias, mask):
    s = tail.shape[0]
    tr = min(512, s)
    n = s // tr

    def body(dc_ref, t_ref, b_ref, m_ref, dt_ref, db_ref, carry_ref):
        i = pl.program_id(0)

        @pl.when(i == 0)
        def _():
            carry_ref[...] = jnp.zeros_like(carry_ref)
            db_ref[...] = jnp.zeros_like(db_ref)

        r = lax.broadcasted_iota(jnp.int32, (tr, tr), 0)
        c = lax.broadcasted_iota(jnp.int32, (tr, tr), 1)
        dlf = _tri_matmul((r <= c).astype(BF16), dc_ref[...]) + carry_ref[...]
        carry_ref[...] = dlf[0:1, :]
        v = t_ref[...] + b_ref[...]
        dt = dlf * jax.nn.sigmoid(-v) * m_ref[...]
        dt_ref[...] = dt
        db_ref[0:1, :] += jnp.sum(dt, axis=0, keepdims=True)

    vec = pl.BlockSpec((1, LANES), lambda i: (0, 0))
    rev = pl.BlockSpec((tr, LANES), lambda i: (n - 1 - i, 0))
    return pl.pallas_call(
        body, name=name, grid=(n,),
        in_specs=[rev, rev, vec, vec],
        out_specs=[rev, pl.BlockSpec((8, LANES), lambda i: (0, 0))],
        out_shape=[jax.ShapeDtypeStruct((s, LANES), F32), jax.ShapeDtypeStruct((8, LANES), F32)],
        scratch_shapes=[pltpu.VMEM((1, LANES), F32)],
        compiler_params=_params("arbitrary"))(dcum, tail, bias, mask)


NT = (((1,), (1,)), ((), ()))
TN = (((0,), (0,)), ((), ()))


def _weights_and_scores(mode, scale, t, qi, kb, q1, k1, q2, k2, cq, ck, masked=True):
    sc = lax.dot_general(q1, k1, NT, preferred_element_type=F32)
    if q2 is not None:
        sc = sc + lax.dot_general(q2, k2, NT, preferred_element_type=F32)
    sc = sc * scale
    if cq is not None:
        sc = sc + (cq - ck)
    if not masked:
        return sc, None
    row = lax.broadcasted_iota(jnp.int32, (t, t), 0)
    col = lax.broadcasted_iota(jnp.int32, (t, t), 1)
    dist = row - col + (qi - kb) * t
    back = dist >= 0
    if mode == "dil":
        w = jnp.zeros((t, t), F32)
        for window, dil in DIL_BRANCHES:
            w = w + (back & (dist <= window) & ((dist & (dil - 1)) == 0)).astype(F32)
    else:
        w = back.astype(F32)
    return jnp.where(w > 0.0, sc, NEG), w


def _causal_steps(mode, diagonal, step):
    if mode == "dil":
        step(True)
    else:
        pl.when(jnp.logical_not(diagonal))(functools.partial(step, False))
        pl.when(diagonal)(functools.partial(step, True))


def _attn_steps(mode, nh, s, key_major):
    t = min(ATTN_TILE, s)
    nq = s // t
    reach = DIL_REACH // t if mode == "dil" else nq
    rows = []
    if key_major:
        for j in range(nq):
            hi = min(nq - 1, j + reach)
            rows += [(h, qb, j, int(qb == hi)) for h in range(nh) for qb in range(j, hi + 1)]
    else:
        for h in range(nh):
            for i in range(nq):
                lo = max(0, i - reach)
                rows += [(h, i, kb, int(kb == lo)) for kb in range(lo, i + 1)]
    tables = [jnp.asarray([r[k] for r in rows], jnp.int32) for k in range(4)]
    return t, len(rows), tables


N_STEP_TABLES = 4


def _attn_operands(mode, nh, t, ops, hf, qf, kf):
    def cols(width, base, rows):
        return pl.BlockSpec((t, width), lambda *g: (rows(*g), base + hf(*g)))

    if mode == "fox":
        pm, ccol, crow = ops
        return ([pm, pm, pm, ccol, crow],
                [cols(HEAD, 0, qf), cols(HEAD, nh, kf), cols(HEAD, 2 * nh, kf),
                 pl.BlockSpec((None, t, 1), lambda *g: (hf(*g), qf(*g), 0)),
                 pl.BlockSpec((None, 1, t), lambda *g: (hf(*g), 0, kf(*g)))])
    if mode == "mla":
        qb, kv, kr = ops
        return ([qb, kv, kr],
                [cols(2 * HEAD, 0, qf), cols(2 * HEAD, 0, kf), pl.BlockSpec((t, HEAD), lambda *g: (kf(*g), 0))])
    q, k, pm, vbase = ops
    return [q, k, pm], [cols(HEAD, 0, qf), cols(HEAD, 0, kf), cols(HEAD, vbase, kf)]


def _unpack(mode, refs):
    if mode == "fox":
        return refs[0][...], refs[1][...], refs[2][...], None, None, refs[3][...], refs[4][...]
    if mode == "mla":
        q, kv = refs[0][...], refs[1][...]
        return q[:, :HEAD], kv[:, :HEAD], kv[:, HEAD:], q[:, HEAD:], refs[2][...], None, None
    return refs[0][...], refs[1][...], refs[2][...], None, None, None, None


def _attn_fwd(name, mode, nh, s, ops, scale, queue=None):
    t, n_steps, tables = _attn_steps(mode, nh, s, key_major=False)
    hf = lambda p, ht, qt, kt, et: ht[p]
    qf = lambda p, ht, qt, kt, et: qt[p]
    kf = lambda p, ht, qt, kt, et: kt[p]
    operands, specs = _attn_operands(mode, nh, t, ops, hf, qf, kf)
    n_in = len(operands)

    def body(ht, qt, kt, et, *refs):
        o_ref, lse_ref, m_ref, l_ref, acc_ref = refs[n_in:]
        sid = pl.program_id(0)
        i, kb = qt[sid], kt[sid]

        @pl.when(et[sid] == 1)
        def _():
            m_ref[...] = jnp.full_like(m_ref, NEG)
            l_ref[...] = jnp.zeros_like(l_ref)
            acc_ref[...] = jnp.zeros_like(acc_ref)

        def step(masked):
            q1, k1, v, q2, k2, cq, ck = _unpack(mode, refs)
            sc, w = _weights_and_scores(mode, scale, t, i, kb, q1, k1, q2, k2, cq, ck, masked)
            m_old = m_ref[...]
            m_new = jnp.maximum(m_old, jnp.max(sc, axis=1, keepdims=True))
            p = jnp.exp(sc - m_new)
            if w is not None:
                p = w * p
            a = jnp.exp(m_old - m_new)
            l_ref[...] = a * l_ref[...] + jnp.sum(p, axis=1, keepdims=True)
            acc_ref[...] = a * acc_ref[...] + jnp.dot(p.astype(BF16), v, preferred_element_type=F32)
            m_ref[...] = m_new

        _causal_steps(mode, kb == i, step)

        @pl.when(kb == i)
        def _():
            o_ref[...] = (acc_ref[...] / l_ref[...]).astype(BF16)
            lse_ref[...] = m_ref[...] + jnp.log(l_ref[...])

    return _call(body, name, (n_steps,), specs,
                 [pl.BlockSpec((t, HEAD), lambda *g: (qf(*g), hf(*g))),
                  pl.BlockSpec((None, t, 1), lambda *g: (hf(*g), qf(*g), 0))],
                 [jax.ShapeDtypeStruct((s, nh * HEAD), BF16), jax.ShapeDtypeStruct((nh, s, 1), F32)],
                 [pltpu.VMEM((t, 1), F32), pltpu.VMEM((t, 1), F32), pltpu.VMEM((t, HEAD), F32)],
                 ("arbitrary",), operands, queue, prefetch=tables)


def _attn_bwd_q(name, mode, nh, s, ops, scale, o, do, do_base, lse):
    t, n_steps, tables = _attn_steps(mode, nh, s, key_major=False)
    hf = lambda p, ht, qt, kt, et: ht[p]
    qf = lambda p, ht, qt, kt, et: qt[p]
    kf = lambda p, ht, qt, kt, et: kt[p]
    operands, specs = _attn_operands(mode, nh, t, ops, hf, qf, kf)
    operands += [o, do, lse]
    specs += [pl.BlockSpec((t, HEAD), lambda *g: (qf(*g), do_base + hf(*g))),
              pl.BlockSpec((t, HEAD), lambda *g: (qf(*g), do_base + hf(*g))),
              pl.BlockSpec((None, t, 1), lambda *g: (hf(*g), qf(*g), 0))]
    n_in = len(operands)
    wq = 2 * HEAD if mode == "mla" else HEAD

    fox = mode == "fox"

    def body(ht, qt, kt, et, *refs):
        o_ref, do_ref, lse_ref = refs[n_in - 3:n_in]
        dq_ref = refs[n_in]
        acc_ref, delta_ref, rows_ref = refs[-3:]
        sid = pl.program_id(0)
        i, kb = qt[sid], kt[sid]

        @pl.when(et[sid] == 1)
        def _():
            acc_ref[...] = jnp.zeros_like(acc_ref)
            rows_ref[...] = jnp.zeros_like(rows_ref)
            delta_ref[...] = jnp.sum(o_ref[...].astype(F32) * do_ref[...].astype(F32), axis=1, keepdims=True)

        def step(masked):
            q1, k1, v, q2, k2, cq, ck = _unpack(mode, refs)
            sc, w = _weights_and_scores(mode, scale, t, i, kb, q1, k1, q2, k2, cq, ck, masked)
            p = jnp.exp(sc - lse_ref[...])
            if w is not None:
                p = w * p
            dp = lax.dot_general(do_ref[...], v, NT, preferred_element_type=F32)
            ds32 = p * (dp - delta_ref[...])
            ds = ds32.astype(BF16)
            if fox:
                rows_ref[...] += jnp.sum(ds32, axis=1, keepdims=True)
            if mode == "mla":
                acc_ref[:, :HEAD] += jnp.dot(ds, k1, preferred_element_type=F32)
                acc_ref[:, HEAD:] += jnp.dot(ds, k2, preferred_element_type=F32)
            else:
                acc_ref[...] += jnp.dot(ds, k1, preferred_element_type=F32)

        _causal_steps(mode, kb == i, step)

        @pl.when(kb == i)
        def _():
            dq_ref[...] = (acc_ref[...] * scale).astype(BF16)
            if fox:
                refs[n_in + 1][...] = rows_ref[...]

    out_specs = [pl.BlockSpec((t, wq), lambda *g: (qf(*g), hf(*g)))]
    out_shape = [jax.ShapeDtypeStruct((s, nh * wq), BF16)]
    if fox:
        out_specs.append(pl.BlockSpec((None, t, 1), lambda *g: (hf(*g), qf(*g), 0)))
        out_shape.append(jax.ShapeDtypeStruct((nh, s, 1), F32))
    grid_spec = pltpu.PrefetchScalarGridSpec(
        num_scalar_prefetch=N_STEP_TABLES, grid=(n_steps,), in_specs=specs, out_specs=out_specs,
        scratch_shapes=[pltpu.VMEM((t, wq), F32), pltpu.VMEM((t, 1), F32), pltpu.VMEM((t, 1), F32)])
    out = pl.pallas_call(body, name=name, grid_spec=grid_spec, out_shape=out_shape,
                         compiler_params=_params("arbitrary"))(*tables, *operands)
    return out if fox else out[0]


def _attn_bwd_kv(name, mode, nh, s, ops, scale, o, do, do_base, lse):
    t, n_steps, tables = _attn_steps(mode, nh, s, key_major=True)
    hf = lambda p, ht, qt, kt, et: ht[p]
    qf = lambda p, ht, qt, kt, et: qt[p]
    kf = lambda p, ht, qt, kt, et: kt[p]
    operands, specs = _attn_operands(mode, nh, t, ops, hf, qf, kf)
    operands += [o, do, lse]
    specs += [pl.BlockSpec((t, HEAD), lambda *g: (qf(*g), do_base + hf(*g))),
              pl.BlockSpec((t, HEAD), lambda *g: (qf(*g), do_base + hf(*g))),
              pl.BlockSpec((None, t, 1), lambda *g: (hf(*g), qf(*g), 0))]
    n_in = len(operands)
    head_tile = pl.BlockSpec((t, HEAD), lambda *g: (kf(*g), hf(*g)))
    if mode == "fox":
        out_specs = [head_tile, head_tile, pl.BlockSpec((None, 1, t), lambda *g: (hf(*g), 0, kf(*g)))]
        out_shape = [jax.ShapeDtypeStruct((s, nh * HEAD), BF16)] * 2 + [jax.ShapeDtypeStruct((nh, 1, s), F32)]
        scratch = [pltpu.VMEM((t, HEAD), F32), pltpu.VMEM((t, HEAD), F32), pltpu.VMEM((1, t), F32)]
    elif mode == "mla":
        out_specs = [pl.BlockSpec((t, 2 * HEAD), lambda *g: (kf(*g), hf(*g))),
                     pl.BlockSpec((t, HEAD), lambda *g: (kf(*g), 0))]
        out_shape = [jax.ShapeDtypeStruct((s, nh * 2 * HEAD), BF16), jax.ShapeDtypeStruct((s, HEAD), F32)]
        scratch = [pltpu.VMEM((t, HEAD), F32), pltpu.VMEM((t, HEAD), F32), pltpu.VMEM((t, HEAD), F32)]
    else:
        out_specs = [head_tile, head_tile]
        out_shape = [jax.ShapeDtypeStruct((s, nh * HEAD), BF16)] * 2
        scratch = [pltpu.VMEM((t, HEAD), F32), pltpu.VMEM((t, HEAD), F32)]
    n_out = len(out_specs)

    def body(ht, qt, kt, et, *refs):
        o_ref, do_ref, lse_ref = refs[n_in - 3:n_in]
        outs = refs[n_in:n_in + n_out]
        dk_acc, dv_acc = refs[n_in + n_out], refs[n_in + n_out + 1]
        extra = refs[n_in + n_out + 2] if mode != "dil" else None
        sid = pl.program_id(0)
        j, h, qb = kt[sid], ht[sid], qt[sid]
        is_first, is_last = qb == j, et[sid] == 1

        @pl.when(is_first)
        def _():
            dk_acc[...] = jnp.zeros_like(dk_acc)
            dv_acc[...] = jnp.zeros_like(dv_acc)
            if mode == "fox":
                extra[...] = jnp.zeros_like(extra)

        if mode == "mla":
            @pl.when(is_first & (h == 0))
            def _():
                extra[...] = jnp.zeros_like(extra)

        def step(masked):
            q1, k1, v, q2, k2, cq, ck = _unpack(mode, refs)
            sc, w = _weights_and_scores(mode, scale, t, qb, j, q1, k1, q2, k2, cq, ck, masked)
            p = jnp.exp(sc - lse_ref[...])
            if w is not None:
                p = w * p
            dov = do_ref[...]
            delta = jnp.sum(o_ref[...].astype(F32) * dov.astype(F32), axis=1, keepdims=True)
            dv_acc[...] += lax.dot_general(p.astype(BF16), dov, TN, preferred_element_type=F32)
            dp = lax.dot_general(dov, v, NT, preferred_element_type=F32)
            ds = p * (dp - delta)
            dsb = ds.astype(BF16)
            dk_acc[...] += lax.dot_general(dsb, q1, TN, preferred_element_type=F32)
            if mode == "mla":
                extra[...] += lax.dot_general(dsb, q2, TN, preferred_element_type=F32)
            if mode == "fox":
                extra[...] -= jnp.sum(ds, axis=0, keepdims=True)

        _causal_steps(mode, is_first, step)

        @pl.when(is_last)
        def _():
            if mode == "mla":
                outs[0][:, :HEAD] = (dk_acc[...] * scale).astype(BF16)
                outs[0][:, HEAD:] = dv_acc[...].astype(BF16)
            else:
                outs[0][...] = (dk_acc[...] * scale).astype(BF16)
                outs[1][...] = dv_acc[...].astype(BF16)
            if mode == "fox":
                outs[2][...] = extra[...]

        if mode == "mla":
            @pl.when(is_last & (h == nh - 1))
            def _():
                outs[1][...] = extra[...] * scale

    grid_spec = pltpu.PrefetchScalarGridSpec(
        num_scalar_prefetch=N_STEP_TABLES, grid=(n_steps,), in_specs=specs, out_specs=out_specs,
        scratch_shapes=scratch)
    return pl.pallas_call(body, name=name, grid_spec=grid_spec, out_shape=out_shape,
                          compiler_params=_params("arbitrary"))(*tables, *operands)


ROW_TILE_BYTES = 2 ** 20


def _row_tile(rows, cols):
    best = 0
    for t in range(16, rows + 1, 16):
        if rows % t == 0 and t * cols * 4 <= ROW_TILE_BYTES:
            best = t
    return best if best else rows


def _adamw(name, w, g, m, v):
    nl, r, c = w.shape
    tr = _row_tile(r, c)

    def body(w_ref, g_ref, m_ref, v_ref, d_ref, nm_ref, nv_ref):
        gv = g_ref[...]
        nm = ADAM_B1 * m_ref[...] + (1.0 - ADAM_B1) * gv
        nv = ADAM_B2 * v_ref[...] + (1.0 - ADAM_B2) * jnp.square(gv)
        m_hat = nm / (1.0 - ADAM_B1 ** ADAM_STEP)
        v_hat = nv / (1.0 - ADAM_B2 ** ADAM_STEP)
        d_ref[...] = -ADAM_LR * (m_hat / (jnp.sqrt(v_hat) + ADAM_EPS) + ADAM_WD * w_ref[...])
        nm_ref[...] = nm
        nv_ref[...] = nv

    blk = pl.BlockSpec((None, tr, c), lambda l, i: (l, i, 0))
    return pl.pallas_call(
        body, name=name, grid=(nl, r // tr), in_specs=[blk] * 4, out_specs=[blk] * 3,
        out_shape=[jax.ShapeDtypeStruct(w.shape, F32)] * 3,
        compiler_params=_params("parallel", "parallel"))(w, g, m, v)


def _add_half(name, g, recv, c_arr):
    nl, r, c = g.shape
    hl = nl // 2
    tr = _row_tile(r, c)

    def body(c_ref, g_ref, r_ref, o_ref):
        o_ref[...] = (g_ref[...] + r_ref[...]).astype(BF16)

    grid_spec = pltpu.PrefetchScalarGridSpec(
        num_scalar_prefetch=1, grid=(hl, r // tr),
        in_specs=[pl.BlockSpec((None, tr, c), lambda l, i, c_ref: (c_ref[0] * hl + l, i, 0)),
                  pl.BlockSpec((None, tr, c), lambda l, i, c_ref: (l, i, 0))],
        out_specs=pl.BlockSpec((None, tr, c), lambda l, i, c_ref: (l, i, 0)))
    return pl.pallas_call(
        body, name=name, grid_spec=grid_spec, out_shape=jax.ShapeDtypeStruct((hl, r, c), BF16),
        compiler_params=_params("parallel", "parallel"))(c_arr, g, recv)


def _sum_chips(name, pst, got, chip_arr):
    _, nl, r, c = pst.shape
    tr = _row_tile(r, c)

    def body(chip_ref, own_ref, g0_ref, g1_ref, g2_ref, o_ref):
        o_ref[...] = ((own_ref[...].astype(F32) + g0_ref[...].astype(F32)) + g1_ref[...].astype(F32)
                      ) + g2_ref[...].astype(F32)

    def slot(k):
        return pl.BlockSpec((None, None, tr, c), lambda l, i, chip_ref: (k, l, i, 0))

    grid_spec = pltpu.PrefetchScalarGridSpec(
        num_scalar_prefetch=1, grid=(nl, r // tr),
        in_specs=[pl.BlockSpec((None, None, tr, c), lambda l, i, chip_ref: (chip_ref[0], l, i, 0)),
                  slot(0), slot(1), slot(2)],
        out_specs=pl.BlockSpec((None, tr, c), lambda l, i, chip_ref: (l, i, 0)))
    return pl.pallas_call(
        body, name=name, grid_spec=grid_spec, out_shape=jax.ShapeDtypeStruct((nl, r, c), F32),
        compiler_params=_params("parallel", "parallel"))(chip_arr, pst, got, got, got)


def _adamw_layers(name, w, mines, others, m, v, c_arr):
    nl, r, c = w.shape
    hr = r // 2
    tr = _row_tile(hr, c)
    nb = hr // tr

    def body(c_ref, w_ref, m_ref, v_ref, *refs):
        halves, (g_ref, d_ref, nm_ref, nv_ref) = refs[:2 * nl], refs[2 * nl:]
        layer, is_mine = pl.program_id(0), pl.program_id(1) == c_ref[0]
        gv = jnp.where(is_mine, halves[0][...], halves[nl][...])
        for k in range(1, nl):
            gv = jnp.where(layer == k, jnp.where(is_mine, halves[k][...], halves[nl + k][...]), gv)
        nm = ADAM_B1 * m_ref[...] + (1.0 - ADAM_B1) * gv
        nv = ADAM_B2 * v_ref[...] + (1.0 - ADAM_B2) * jnp.square(gv)
        m_hat = nm / (1.0 - ADAM_B1 ** ADAM_STEP)
        v_hat = nv / (1.0 - ADAM_B2 ** ADAM_STEP)
        g_ref[...] = gv
        d_ref[...] = -ADAM_LR * (m_hat / (jnp.sqrt(v_hat) + ADAM_EPS) + ADAM_WD * w_ref[...])
        nm_ref[...] = nm
        nv_ref[...] = nv

    def half_of(k):
        return pl.BlockSpec((None, tr, c), lambda l, h, i, c_ref: (0, jnp.where(l == k, i, 0), 0))

    blk = pl.BlockSpec((None, tr, c), lambda l, h, i, c_ref: (l, h * nb + i, 0))
    grid_spec = pltpu.PrefetchScalarGridSpec(
        num_scalar_prefetch=1, grid=(nl, 2, nb),
        in_specs=[blk, blk, blk] + [half_of(k) for k in range(nl)] * 2, out_specs=[blk] * 4)
    return pl.pallas_call(
        body, name=name, grid_spec=grid_spec, out_shape=[jax.ShapeDtypeStruct(w.shape, F32)] * 4,
        compiler_params=_params("parallel", "parallel", "parallel"))(c_arr, w, m, v, *mines, *others)


def _sum_slots(name, rc, order, out_dtype=F32):
    _, nl, r, c = rc.shape
    tr = _row_tile(r, c)

    def body(*refs):
        acc = refs[0][...].astype(F32)
        for ref in refs[1:-1]:
            acc = acc + ref[...].astype(F32)
        refs[-1][...] = acc.astype(out_dtype)

    def slot(k):
        return pl.BlockSpec((None, None, tr, c), lambda l, i: (k, l, i, 0))

    return pl.pallas_call(
        body, name=name, grid=(nl, r // tr), in_specs=[slot(k) for k in order],
        out_specs=pl.BlockSpec((None, tr, c), lambda l, i: (l, i, 0)),
        out_shape=jax.ShapeDtypeStruct((nl, r, c), out_dtype),
        compiler_params=_params("parallel", "parallel"))(*([rc] * len(order)))


class _Comm:
    def __init__(self, name, ins, out_shapes, build, n_first, n_then=0, n_local=0):
        self.name, self.ins, self.out_shapes, self.build = name, list(ins), list(out_shapes), build
        self.n_first, self.n_then, self.n_local = n_first, n_then, n_local


def _comm_steps(comm, in_refs, out_refs, send_sems, recv_sems, local_sems=None):
    def descriptors(with_then):
        x, y, c = lax.axis_index("x"), lax.axis_index("y"), lax.axis_index("c")
        first, then, local = comm.build(in_refs, out_refs, x, y, c)
        assert (len(first), len(then), len(local)) == (comm.n_first, comm.n_then, comm.n_local)

        def remote(k, src, dst, flips):
            fx, fy, fc = flips
            peer = (1 - x if fx else x, 1 - y if fy else y, 1 - c if fc else c)
            return pltpu.make_async_remote_copy(src_ref=src, dst_ref=dst, send_sem=send_sems.at[k],
                                                recv_sem=recv_sems.at[k], device_id=peer, device_id_type=MESH)

        f = [remote(k, s, d, fl) for k, (s, d, fl) in enumerate(first)]
        t = [(remote(len(first) + k, s, d, fl), dep) for k, (s, d, fl, dep) in enumerate(then)] if with_then else []
        lc = [pltpu.make_async_copy(s, d, local_sems.at[k]) for k, (s, d) in enumerate(local)]
        return f, t, lc

    def start():
        f, _, lc = descriptors(False)
        for cp in f + lc:
            cp.start()

    def finish():
        f, t, lc = descriptors(True)
        for k, cp in enumerate(f):
            cp.wait_recv()
            for cp2, dep in t:
                if dep == k:
                    cp2.start()
        for cp2, _ in t:
            cp2.wait_recv()
        for cp in f + [cp2 for cp2, _ in t]:
            cp.wait_send()
        for cp in lc:
            cp.wait()

    return start, finish


def _run_comm(comm):
    n_in, n_out = len(comm.ins), len(comm.out_shapes)

    def body(*refs):
        start, finish = _comm_steps(comm, refs[:n_in], refs[n_in:n_in + n_out], *refs[n_in + n_out:])
        start()
        finish()

    n_sem = comm.n_first + comm.n_then
    return pl.pallas_call(
        body, name=comm.name, in_specs=[ANY] * n_in, out_specs=[ANY] * n_out, out_shape=comm.out_shapes,
        scratch_shapes=[pltpu.SemaphoreType.DMA((n_sem,)), pltpu.SemaphoreType.DMA((n_sem,)),
                        pltpu.SemaphoreType.DMA((max(comm.n_local, 1),))])(*comm.ins)


def _call(body, name, grid, in_specs, out_specs, out_shape, scratch, semantics, operands, queue=None, prefetch=()):
    n_pre = len(prefetch)

    def run(kernel_body, ins, outs, shapes, scratches, sems, args):
        grid_spec = pltpu.PrefetchScalarGridSpec(num_scalar_prefetch=n_pre, grid=grid, in_specs=ins,
                                                 out_specs=outs, scratch_shapes=scratches)
        return pl.pallas_call(kernel_body, name=name, grid_spec=grid_spec, out_shape=shapes,
                              compiler_params=_params(*sems))(*prefetch, *args)

    job = queue.take() if queue is not None else None
    if job is None:
        return run(body, list(in_specs), list(out_specs), list(out_shape), list(scratch), semantics, operands)
    comm = job[1]
    assert comm.n_local == 0
    n_in, n_out, n_cin, n_cout = len(in_specs), len(out_specs), len(comm.ins), len(comm.out_shapes)
    n_sem = comm.n_first + comm.n_then

    def hosted(*refs):
        pre, refs = refs[:n_pre], refs[n_pre:]
        a, b = n_in, n_in + n_cin
        c, d = b + n_out, b + n_out + n_cout
        start, finish = _comm_steps(comm, refs[a:b], refs[c:d], refs[-2], refs[-1])
        ids = [pl.program_id(k) for k in range(len(grid))]
        is_first = functools.reduce(jnp.logical_and, [i == 0 for i in ids])
        is_last = functools.reduce(jnp.logical_and, [i == g - 1 for i, g in zip(ids, grid)])
        pl.when(is_first)(start)
        body(*pre, *refs[:a], *refs[b:c], *refs[d:len(refs) - 2])
        pl.when(is_last)(finish)

    name = name + "_and_" + comm.name
    outs = run(hosted, list(in_specs) + [ANY] * n_cin, list(out_specs) + [ANY] * n_cout,
               list(out_shape) + comm.out_shapes,
               list(scratch) + [pltpu.SemaphoreType.DMA((n_sem,)), pltpu.SemaphoreType.DMA((n_sem,))],
               ("arbitrary",) * len(grid), list(operands) + comm.ins)
    queue.done(job, outs[n_out:])
    return outs[:n_out]


class _CommQueue:
    def __init__(self):
        self.jobs = []

    def add(self, task):
        self.done([task, None], None, first=True)

    def take(self):
        return self.jobs.pop(0) if self.jobs else None

    def done(self, job, results, first=False):
        try:
            comm = next(job[0]) if first else job[0].send(results)
        except StopIteration:
            return
        self.jobs.append([job[0], comm])

    def drain(self, until=lambda: False):
        while self.jobs and not until():
            job = self.take()
            self.done(job, _run_comm(job[1]))


CHIP_FLIPS = ((1, 0), (0, 1), (1, 1))


def _flip(v, f):
    return 1 - v if f else v


def _gather_comm(name, shards):
    def build(ins, outs, x, y, c):
        first, then = [], []
        me = 2 * x + y
        for w_ref, out_ref in zip(ins, outs):
            base = len(first)
            for j, (fx, fy) in enumerate(CHIP_FLIPS):
                theirs = out_ref.at[2 * _flip(x, fx) + _flip(y, fy), pl.ds(c, 1)]
                first.append((w_ref.at[pl.ds(c, 1)], out_ref.at[me, pl.ds(c, 1)], (fx, fy, 0)))
                then.append((theirs, theirs, (0, 0, 1), base + j))
            first.append((w_ref, out_ref.at[me], (0, 0, 1)))
        return first, then, []

    n = len(shards)
    return _Comm(name, shards, [jax.ShapeDtypeStruct((4,) + w.shape, w.dtype) for w in shards], build, 4 * n, 3 * n)


def _sibling_comm(name, gs):
    def build(ins, outs, x, y, c):
        return [(g.at[pl.ds(1 - c, 1)], o, (0, 0, 1)) for g, o in zip(ins, outs)], [], []

    return _Comm(name, gs, [jax.ShapeDtypeStruct((1,) + g.shape[1:], g.dtype) for g in gs], build, len(gs))


def _chips_comm(name, psts):
    def build(ins, outs, x, y, c):
        first = []
        for p_ref, o_ref in zip(ins, outs):
            for k, (fx, fy) in enumerate(CHIP_FLIPS):
                first.append((p_ref.at[2 * _flip(x, fx) + _flip(y, fy)], o_ref.at[k], (fx, fy, 0)))
        return first, [], []

    return _Comm(name, psts, [jax.ShapeDtypeStruct((3,) + p.shape[1:], p.dtype) for p in psts], build,
                 3 * len(psts))


def _swap_comm(name, ss):
    def build(ins, outs, x, y, c):
        return [(s, o, (0, 0, 1)) for s, o in zip(ins, outs)], [], []

    return _Comm(name, ss, [jax.ShapeDtypeStruct(s.shape, s.dtype) for s in ss], build, len(ss))


def _gather_all(name, blk):
    def build(ins, outs, x, y, c):
        mine = outs[0].at[4 * x + 2 * y + c]
        first = [(ins[0], mine, (m >> 2 & 1, m >> 1 & 1, m & 1)) for m in range(1, 8)]
        return first, [], [(ins[0], mine)]

    return _run_comm(_Comm(name, [blk], [jax.ShapeDtypeStruct((8,) + blk.shape, blk.dtype)], build, 7, 0, 1))[0]


class _Dims:
    def __init__(self, d, in_w):
        self.d = d
        self.fox_h = (d // 4) // HEAD
        self.mla_h = (d // 2) // HEAD
        self.dil_h = (d // 4) // HEAD
        self.gw = d // 4
        self.rank = d // 4
        gw, fh = self.gw, self.fox_h
        sizes = (gw, gw, gw, fh, self.rank, self.rank, MLA_ROPE, gw, gw, gw)
        assert sum(sizes) == in_w
        offs = [0]
        for z in sizes:
            offs.append(offs[-1] + z)
        self.nat = dict(zip(("fq", "fk", "fv", "fl", "cq", "ckv", "kr", "dq", "dk", "dv"), zip(offs[:-1], sizes)))
        self.main_order = ("fq", "fk", "fv", "cq", "ckv", "dq", "dk", "dv")
        self.main_w = 8 * gw
        self.col = {n: i * gw for i, n in enumerate(self.main_order)}

    def align_w_in(self, w):
        parts = [w[:, self.nat[n][0]:self.nat[n][0] + self.nat[n][1]] for n in self.main_order]
        parts += [w[:, self.nat["kr"][0]:self.nat["kr"][0] + MLA_ROPE],
                  w[:, self.nat["fl"][0]:self.nat["fl"][0] + self.fox_h],
                  jnp.zeros((w.shape[0], LANES - MLA_ROPE - self.fox_h), w.dtype)]
        return jnp.concatenate(parts, axis=1)

    def unalign_w_in(self, g):
        gw, t0 = self.gw, self.main_w
        src = {n: g[:, self.col[n]:self.col[n] + gw] for n in self.main_order}
        src["kr"] = g[:, t0:t0 + MLA_ROPE]
        src["fl"] = g[:, t0 + FORGET_LANE:t0 + FORGET_LANE + self.fox_h]
        return jnp.concatenate([src[n] for n in ("fq", "fk", "fv", "fl", "cq", "ckv", "kr", "dq", "dk", "dv")], axis=1)

    def pad_w_uq(self, w):
        r = w.shape[0]
        w3 = w.reshape(r, self.mla_h, HEAD + MLA_ROPE)
        return jnp.pad(w3, ((0, 0), (0, 0), (0, HEAD - MLA_ROPE))).reshape(r, self.mla_h * 2 * HEAD)

    def unpad_w_uq(self, g):
        r = g.shape[0]
        return g.reshape(r, self.mla_h, 2 * HEAD)[:, :, :HEAD + MLA_ROPE].reshape(r, self.mla_h * (HEAD + MLA_ROPE))


def _rope_tables(s):
    def tables(dim):
        inv = 1.0 / (ROPE_THETA ** (jnp.arange(0, dim, 2, dtype=F32) / dim))
        ang = jnp.arange(s, dtype=F32)[:, None] * inv[None, :]
        return jnp.cos(ang), jnp.sin(ang)

    def build(cos, sin, lead, trail_one, trail_zero):
        z = jnp.zeros_like(sin)
        ones = lambda n: jnp.ones((s, n), F32)
        zeros = lambda n: jnp.zeros((s, n), F32)
        c = jnp.concatenate([ones(lead), cos, cos, ones(trail_one), zeros(trail_zero)], axis=1)
        s1 = jnp.concatenate([zeros(lead), -sin, z, zeros(trail_one + trail_zero)], axis=1)
        s2 = jnp.concatenate([zeros(lead), z, sin, zeros(trail_one + trail_zero)], axis=1)
        return c, s1, s2

    cm, sm = tables(MLA_ROPE)
    cp, sp = tables(PARTIAL_ROPE)
    return {"mla_q": build(cm, sm, HEAD, 0, HEAD - MLA_ROPE),
            "mla_k": build(cm, sm, 0, 0, LANES - MLA_ROPE),
            "dil": build(cp, sp, 0, HEAD - PARTIAL_ROPE, 0)}


def _ffn_forward(tag, x, g, wg, wu, wd, queue):
    h = _rms_fwd(tag + "_norm", x, g)
    a, b, z = _ffn_up(tag + "_up", h, wg, wu, queue)
    y = _mm(tag + "_down", z, wd, res=x, alpha=0.5, out_dtype=F32, queue=queue)
    return y, (x, h, a, b, z)


def _ffn_backward(tag, dx, dxb, saved, g, wg, wu, wd, queue):
    x, h, a, b, z = saved
    da, db = _ffn_bwd_mid(tag + "_bwd_mid", dxb, wd, a, b, 0.5, queue)
    g_wd = _mm(tag + "_dwd", z, dxb, ta=True, alpha=0.5, out_dtype=F32, queue=queue)
    g_wg = _mm(tag + "_dwg", h, da, ta=True, out_dtype=F32, queue=queue)
    g_wu = _mm(tag + "_dwu", h, db, ta=True, out_dtype=F32, queue=queue)
    dh = _mm(tag + "_dh", da, wg, tb=True, a2=db, b2=wu, out_dtype=F32, queue=queue)
    dx, dxb, dg = _rms_bwd(tag + "_norm_bwd", dh, x, g, res=dx)
    return dx, dxb, dg[0], g_wg, g_wu, g_wd


def _mix_forward(dm, tabs, x, lw, s, queue):
    d, gw = dm.d, dm.gw
    h = _rms_fwd("mix_norm", x, lw["mix_norm"])
    pm = _mm("mix_in_main", h, lw["w_in_main"], queue=queue)
    tail = _mm("mix_in_tail", h, lw["w_in_tail"], out_dtype=F32)
    cum = _gate_fwd("fox_gate", tail, lw["gate_bias"], lw["gate_mask"])
    cum_h = cum[:, FORGET_LANE:FORGET_LANE + dm.fox_h].T
    ccol, crow = cum_h[:, :, None], cum_h[:, None, :]
    fox_ops = (pm, ccol, crow)
    out_a, lse_a = _attn_fwd("fox_fwd", "fox", dm.fox_h, s, fox_ops, HEAD ** -0.5, queue)

    cqn = _rms_fwd("mla_q_norm", pm, lw["mla_q_norm"], width=gw, col=dm.col["cq"])
    ckvn = _rms_fwd("mla_kv_norm", pm, lw["mla_kv_norm"], width=gw, col=dm.col["ckv"])
    qb_raw = _mm("mla_uq", cqn, lw["mla_w_uq"])
    qb = _rope("mla_q_rope", qb_raw, tabs["mla_q"], MLA_ROPE // 2, width=2 * HEAD, nblk=dm.mla_h)
    kv = _mm("mla_ukv", ckvn, lw["mla_w_ukv"])
    kr = _rope("mla_k_rope", tail, tabs["mla_k"], MLA_ROPE // 2, width=LANES)
    mla_ops = (qb, kv, kr)
    out_b, lse_b = _attn_fwd("mla_fwd", "mla", dm.mla_h, s, mla_ops, (HEAD + MLA_ROPE) ** -0.5, queue)

    dqr = _rope("dil_q_rope", pm, tabs["dil"], PARTIAL_ROPE // 2, width=HEAD, col=dm.col["dq"], nblk=dm.dil_h)
    dkr = _rope("dil_k_rope", pm, tabs["dil"], PARTIAL_ROPE // 2, width=HEAD, col=dm.col["dk"], nblk=dm.dil_h)
    dil_ops = (dqr, dkr, pm, dm.col["dv"] // HEAD)
    out_c, lse_c = _attn_fwd("dil_fwd", "dil", dm.dil_h, s, dil_ops, HEAD ** -0.5, queue)

    mixed = jnp.concatenate([out_a, out_b, out_c], axis=1)
    y = _mm("mix_out", mixed, lw["w_out"], res=x, out_dtype=F32, queue=queue)
    saved = (x, h, pm, tail, fox_ops, lse_a, cqn, ckvn, mla_ops, lse_b, dil_ops, lse_c, mixed)
    return y, saved


def _mix_backward(dm, tabs, dx, dxb, saved, lw, s, queue):
    x, h, pm, tail, fox_ops, lse_a, cqn, ckvn, mla_ops, lse_b, dil_ops, lse_c, mixed = saved
    gw = dm.gw
    grads = {}
    grads["w_out"] = _mm("mix_dwout", mixed, dxb, ta=True, out_dtype=F32)
    dmixed = _mm("mix_dmixed", dxb, lw["w_out"], tb=True)
    nha, nhb, nhc = dm.fox_h, dm.mla_h, dm.dil_h

    sa = HEAD ** -0.5
    dfq, dcum_q = _attn_bwd_q("fox_bwd_q", "fox", nha, s, fox_ops, sa, mixed, dmixed, 0, lse_a)
    dfk, dfv, dcum = _attn_bwd_kv("fox_bwd_kv", "fox", nha, s, fox_ops, sa, mixed, dmixed, 0, lse_a)

    sb = (HEAD + MLA_ROPE) ** -0.5
    dqb = _attn_bwd_q("mla_bwd_q", "mla", nhb, s, mla_ops, sb, mixed, dmixed, nha, lse_b)
    dkv, dkr = _attn_bwd_kv("mla_bwd_kv", "mla", nhb, s, mla_ops, sb, mixed, dmixed, nha, lse_b)
    dqb_raw = _rope("mla_q_rope_bwd", dqb, tabs["mla_q"], MLA_ROPE // 2, width=2 * HEAD, nblk=nhb, transpose=True)
    grads["mla_w_uq"] = _mm("mla_dwuq", cqn, dqb_raw, ta=True, out_dtype=F32)
    dcqn = _mm("mla_dcqn", dqb_raw, lw["mla_w_uq"], tb=True)
    grads["mla_w_ukv"] = _mm("mla_dwukv", ckvn, dkv, ta=True, out_dtype=F32)
    dckvn = _mm("mla_dckvn", dkv, lw["mla_w_ukv"], tb=True)
    dcq, dg_q = _rms_bwd("mla_q_norm_bwd", dcqn, pm, lw["mla_q_norm"], width=gw, col=dm.col["cq"])
    dckv, dg_kv = _rms_bwd("mla_kv_norm_bwd", dckvn, pm, lw["mla_kv_norm"], width=gw, col=dm.col["ckv"])
    dkr_raw = _rope("mla_k_rope_bwd", dkr, tabs["mla_k"], MLA_ROPE // 2, width=LANES, transpose=True, out_dtype=F32)

    sc = HEAD ** -0.5
    ddqr = _attn_bwd_q("dil_bwd_q", "dil", nhc, s, dil_ops, sc, mixed, dmixed, nha + nhb, lse_c)
    ddkr, ddv = _attn_bwd_kv("dil_bwd_kv", "dil", nhc, s, dil_ops, sc, mixed, dmixed, nha + nhb, lse_c)
    ddq = _rope("dil_q_rope_bwd", ddqr, tabs["dil"], PARTIAL_ROPE // 2, width=HEAD, nblk=nhc, transpose=True)
    ddk = _rope("dil_k_rope_bwd", ddkr, tabs["dil"], PARTIAL_ROPE // 2, width=HEAD, nblk=nhc, transpose=True)

    dcum_lanes = jnp.pad((dcum[:, 0, :] + dcum_q[:, :, 0]).T, ((0, 0), (FORGET_LANE, LANES - FORGET_LANE - nha)))
    dgate, dbias = _gate_bwd("fox_gate_bwd", dcum_lanes, tail, lw["gate_bias"], lw["gate_mask"])
    dtail = (dgate + dkr_raw).astype(BF16)
    dpm = jnp.concatenate([dfq, dfk, dfv, dcq, dckv, ddq, ddk, ddv], axis=1)

    g_main = _mm("mix_dwin_main", h, dpm, ta=True, out_dtype=F32, queue=queue)
    g_tail = _mm("mix_dwin_tail", h, dtail, ta=True, out_dtype=F32)
    grads["w_in"] = jnp.concatenate([g_main, g_tail], axis=1)
    dh = _mm("mix_dh_main", dpm, lw["w_in_main"], tb=True, out_dtype=F32, queue=queue)
    dh = _mm("mix_dh_tail", dtail, lw["w_in_tail"], tb=True, res=dh, out_dtype=F32)
    dx, dxb, dg = _rms_bwd("mix_norm_bwd", dh, x, lw["mix_norm"], res=dx)
    grads["mix_norm"] = dg[0]
    grads["mla_q_norm"] = dg_q[0]
    grads["mla_kv_norm"] = dg_kv[0]
    grads["fox_forget_bias"] = dbias[0, FORGET_LANE:FORGET_LANE + nha]
    return dx, dxb, grads


def kernel(x, ffn1_norm, ffn1_w_gate, ffn1_w_up, ffn1_w_down, mix_norm, w_in, fox_forget_bias, mla_q_norm, mla_kv_norm, mla_w_uq, mla_w_ukv, w_out, ffn2_norm, ffn2_w_gate, ffn2_w_up, ffn2_w_down, final_norm, loss_target, m_ffn1_norm, m_ffn1_w_gate, m_ffn1_w_up, m_ffn1_w_down, m_mix_norm, m_w_in, m_fox_forget_bias, m_mla_q_norm, m_mla_kv_norm, m_mla_w_uq, m_mla_w_ukv, m_w_out, m_ffn2_norm, m_ffn2_w_gate, m_ffn2_w_up, m_ffn2_w_down, m_final_norm, v_ffn1_norm, v_ffn1_w_gate, v_ffn1_w_up, v_ffn1_w_down, v_mix_norm, v_w_in, v_fox_forget_bias, v_mla_q_norm, v_mla_kv_norm, v_mla_w_uq, v_mla_w_ukv, v_w_out, v_ffn2_norm, v_ffn2_w_gate, v_ffn2_w_up, v_ffn2_w_down, v_final_norm):
    p = dict(locals())
    s, d = x.shape[1], x.shape[2]
    depth = ffn1_norm.shape[0]
    dm = _Dims(d, 4 * w_in.shape[2])
    tabs = _rope_tables(s)
    c_arr = lax.axis_index("c").astype(jnp.int32).reshape(1)
    chip_arr = (2 * lax.axis_index("x") + lax.axis_index("y")).astype(jnp.int32).reshape(1)

    shards = {n: p[n].astype(BF16) for n in BIG}
    stacked = [dict() for _ in range(depth)]

    def gather_task(names, l):
        halves = [shards[n][l].reshape(2, shards[n].shape[1] // 2, shards[n].shape[2]) for n in names]
        outs = yield _gather_comm("ag_" + names[0], halves)
        for n, o in zip(names, outs):
            stacked[l][n] = o.reshape(4, shards[n].shape[1], shards[n].shape[2])

    def launch_gathers(queue, l):
        for names in GATHER_BUNDLES:
            queue.add(gather_task(names, l))

    gate_lanes = ((0, 0), (FORGET_LANE, LANES - FORGET_LANE - dm.fox_h))
    gate_mask = jnp.pad(jnp.ones((1, dm.fox_h), F32), gate_lanes)

    fq = _CommQueue()
    for l in range(depth):
        launch_gathers(fq, l)

    def sublayer_weights(lw, l, names):
        fq.drain(until=lambda: all(n in stacked[l] for n in names))
        for n in names:
            st = stacked[l][n]
            lw[n] = (st.reshape(-1, st.shape[-1]) if n in ROW_SHARDED
                     else jnp.transpose(st, (1, 0, 2)).reshape(st.shape[1], -1))

    def layer_constants(l):
        lw = {n: p[n][l][None, :] for n in ("ffn1_norm", "mix_norm", "mla_q_norm", "mla_kv_norm", "ffn2_norm")}
        lw["gate_bias"] = jnp.pad(fox_forget_bias[l][None, :], gate_lanes)
        lw["gate_mask"] = gate_mask
        return lw

    xs = x[0]
    saved, layers = [], []
    for l in range(depth):
        lw = layer_constants(l)
        layers.append(lw)
        sublayer_weights(lw, l, ("ffn1_w_gate", "ffn1_w_up", "ffn1_w_down"))
        xs, s1 = _ffn_forward("ffn1", xs, lw["ffn1_norm"], lw["ffn1_w_gate"], lw["ffn1_w_up"], lw["ffn1_w_down"], fq)
        sublayer_weights(lw, l, ("w_in", "mla_w_uq", "mla_w_ukv", "w_out"))
        w_in_al = dm.align_w_in(lw["w_in"])
        lw["w_in_main"] = w_in_al[:, :dm.main_w]
        lw["w_in_tail"] = w_in_al[:, dm.main_w:]
        lw["mla_w_uq"] = dm.pad_w_uq(lw["mla_w_uq"])
        xs, s2 = _mix_forward(dm, tabs, xs, lw, s, fq)
        sublayer_weights(lw, l, ("ffn2_w_gate", "ffn2_w_up", "ffn2_w_down"))
        xs, s3 = _ffn_forward("ffn2", xs, lw["ffn2_norm"], lw["ffn2_w_gate"], lw["ffn2_w_up"], lw["ffn2_w_down"], fq)
        saved.append((s1, s2, s3))
    fq.drain()
    dx, dxb, dg_final, loss_blk = _loss_head("loss_head", xs, final_norm[None, :], loss_target[0])

    mines = {n: [None] * depth for n in BIG}
    others = {n: [None] * depth for n in BIG}

    def halves_view(n, g):
        rr, cc = g.shape
        if n in ROW_SHARDED:
            return jnp.transpose(g.reshape(4, 2, rr // 8, cc), (1, 0, 2, 3)).reshape(2, rr // 2, cc)
        return g.reshape(2, rr // 2, cc)

    def reduce_task(names, l, gl):
        views = [halves_view(n, gl[n]) for n in names]
        recvs = yield _sibling_comm("rs_sibling_" + names[0], views)
        for n, view, recv in zip(names, views, recvs):
            pair = _add_half("rs_pair_" + n, view, recv, c_arr)
            rr, cc = pair.shape[1], pair.shape[2]
            if n in ROW_SHARDED:
                pst = jnp.transpose(pair.reshape(1, 4, rr // 4, cc), (1, 0, 2, 3))
            else:
                pst = jnp.transpose(pair.reshape(1, rr, 4, cc // 4), (2, 0, 1, 3))
            (got,) = yield _chips_comm("rs_chips_" + n, [pst])
            mines[n][l] = _sum_chips("rs_sum_" + n, pst, got, chip_arr)

    bq = _CommQueue()
    g_layers = [None] * depth
    for l in reversed(range(depth)):
        lw = layers[l]
        s1, s2, s3 = saved[l]
        gl = {}
        dx, dxb, gl["ffn2_norm"], gl["ffn2_w_gate"], gl["ffn2_w_up"], gl["ffn2_w_down"] = _ffn_backward(
            "ffn2", dx, dxb, s3, lw["ffn2_norm"], lw["ffn2_w_gate"], lw["ffn2_w_up"], lw["ffn2_w_down"], bq)
        bq.add(reduce_task(REDUCE_BUNDLES[0], l, gl))
        dx, dxb, gm = _mix_backward(dm, tabs, dx, dxb, s2, lw, s, bq)
        gl.update(gm)
        gl["w_in"] = dm.unalign_w_in(gl["w_in"])
        gl["mla_w_uq"] = dm.unpad_w_uq(gl["mla_w_uq"])
        bq.add(reduce_task(REDUCE_BUNDLES[2], l, gl))
        dx, dxb, gl["ffn1_norm"], gl["ffn1_w_gate"], gl["ffn1_w_up"], gl["ffn1_w_down"] = _ffn_backward(
            "ffn1", dx, dxb, s1, lw["ffn1_norm"], lw["ffn1_w_gate"], lw["ffn1_w_up"], lw["ffn1_w_down"], bq)
        bq.add(reduce_task(REDUCE_BUNDLES[1], l, gl))
        g_layers[l] = gl
    bq.drain()
    swapped = _run_comm(_swap_comm("rs_swap", [mines[n][l] for n in BIG for l in range(depth)]))
    for k, n in enumerate(BIG):
        others[n] = list(swapped[k * depth:(k + 1) * depth])
    grad_x = dx[None]

    grads, deltas, new_m, new_v = {}, {}, {}, {}
    for n in BIG:
        grads[n], deltas[n], new_m[n], new_v[n] = _adamw_layers(
            "adamw_" + n, p[n], mines[n], others[n], p["m_" + n], p["v_" + n], c_arr)

    def pack(get):
        flat = lambda n: jnp.pad(get(n).reshape(1, -1), ((0, 0), (0, d - get(n).size)))
        rows = [get("ffn1_norm"), get("mix_norm"), get("ffn2_norm"), get("final_norm")[None, :],
                flat("mla_q_norm"), flat("mla_kv_norm"), flat("fox_forget_bias")]
        return jnp.concatenate(rows, axis=0)

    def unpack(blk):
        o = 3 * depth
        return {"ffn1_norm": blk[0:depth], "mix_norm": blk[depth:2 * depth], "ffn2_norm": blk[2 * depth:o],
                "final_norm": blk[o], "mla_q_norm": blk[o + 1, :depth * dm.rank].reshape(depth, dm.rank),
                "mla_kv_norm": blk[o + 2, :depth * dm.rank].reshape(depth, dm.rank),
                "fox_forget_bias": blk[o + 3, :depth * dm.fox_h].reshape(depth, dm.fox_h)}

    assert depth * dm.rank <= d
    local_small = {n: (dg_final[0] if n == "final_norm" else jnp.stack([g_layers[l][n] for l in range(depth)]))
                   for n in SMALL}
    n_rows = 3 * depth + 4
    pad_rows = -(n_rows + 1) % 8
    blk = jnp.concatenate([pack(lambda n: local_small[n]), jnp.broadcast_to(loss_blk[0:1, 0:1], (1, d)),
                           jnp.zeros((pad_rows, d), F32)], axis=0)
    everyone = _gather_all("small_gather", blk)[:, None]
    total = _sum_slots("small_sum", everyone, tuple(range(8)))[0]
    loss = total[n_rows, 0]
    small_g = unpack(total)
    w_blk = jnp.concatenate([pack(lambda n: p[n]), jnp.zeros((pad_rows + 1, d), F32)], axis=0)
    m_blk = jnp.concatenate([pack(lambda n: p["m_" + n]), jnp.zeros((pad_rows + 1, d), F32)], axis=0)
    v_blk = jnp.concatenate([pack(lambda n: p["v_" + n]), jnp.zeros((pad_rows + 1, d), F32)], axis=0)
    g_blk = jnp.concatenate([total[:n_rows], jnp.zeros((pad_rows + 1, d), F32)], axis=0)
    d_blk, nm_blk, nv_blk = _adamw("adamw_small", w_blk[None], g_blk[None], m_blk[None], v_blk[None])
    small_d, small_m, small_v = unpack(d_blk[0]), unpack(nm_blk[0]), unpack(nv_blk[0])
    for n in SMALL:
        grads[n], deltas[n], new_m[n], new_v[n] = small_g[n], small_d[n], small_m[n], small_v[n]

    return (loss, grad_x, *[grads[n] for n in WEIGHTS], *[deltas[n] for n in WEIGHTS],
            *[new_m[n] for n in WEIGHTS], *[new_v[n] for n in WEIGHTS])
```

```python
import functools

import jax
import jax.numpy as jnp
from jax import lax
from jax.experimental import pallas as pl
from jax.experimental.pallas import tpu as pltpu

F32 = jnp.float32
BF16 = jnp.bfloat16
MESH = pl.DeviceIdType.MESH
ANY = pl.BlockSpec(memory_space=pl.ANY)

LANES = 128
VMEM_LIMIT_BYTES = 56 * 2 ** 20
HEAD = 128
EPS = 1e-6
NEG = -1e30
ATTN_TILE = 512
ROPE_THETA = 500000.0
MLA_ROPE = 64
PARTIAL_ROPE = HEAD // 4
DIL_BRANCHES = ((128, 1), (512, 4), (2048, 16))
DIL_REACH = max(w for w, _ in DIL_BRANCHES)
FORGET_LANE = MLA_ROPE
ADAM_LR, ADAM_B1, ADAM_B2, ADAM_EPS, ADAM_WD, ADAM_STEP = 0.001, 0.9, 0.999, 1e-08, 0.01, 10

BIG = ("ffn1_w_gate", "ffn1_w_up", "ffn1_w_down", "w_in", "mla_w_uq", "mla_w_ukv", "w_out",
       "ffn2_w_gate", "ffn2_w_up", "ffn2_w_down")
ROW_SHARDED = ("ffn1_w_down", "w_out", "ffn2_w_down")
GATHER_BUNDLES = (("ffn1_w_gate", "ffn1_w_up"), ("ffn1_w_down", "mla_w_uq", "mla_w_ukv"), ("w_in", "w_out"),
                  ("ffn2_w_gate", "ffn2_w_up"), ("ffn2_w_down",))
REDUCE_BUNDLES = (("ffn2_w_gate", "ffn2_w_up", "ffn2_w_down"), ("ffn1_w_gate", "ffn1_w_up", "ffn1_w_down"),
                  ("w_in", "w_out", "mla_w_uq", "mla_w_ukv"))
SMALL = ("ffn1_norm", "mix_norm", "fox_forget_bias", "mla_q_norm", "mla_kv_norm", "ffn2_norm", "final_norm")
WEIGHTS = ("ffn1_norm", "ffn1_w_gate", "ffn1_w_up", "ffn1_w_down", "mix_norm", "w_in", "fox_forget_bias",
           "mla_q_norm", "mla_kv_norm", "mla_w_uq", "mla_w_ukv", "w_out", "ffn2_norm", "ffn2_w_gate",
           "ffn2_w_up", "ffn2_w_down", "final_norm")


def _params(*sem):
    return pltpu.CompilerParams(dimension_semantics=sem, vmem_limit_bytes=VMEM_LIMIT_BYTES)


def _div_tile(n, cap):
    if n <= cap:
        return n
    best = 0
    for t in range(LANES, cap + 1, LANES):
        if n % t == 0:
            best = t
    assert best, (n, cap)
    return best


MM_TILE_BUDGET_BYTES = 34 * 2 ** 20


def _mm_tiles(m, n, k, pairs, out_bytes, has_res):
    def cands(x, cap):
        return [x] if x <= LANES else [t for t in range(LANES, min(x, cap) + 1, LANES) if x % t == 0]

    best = None
    for tm in cands(m, 1024):
        for tn in cands(n, 1408):
            for tk in cands(k, 2048):
                need = (2 * 2 * pairs * (tm * tk + tk * tn) + 4 * tm * tn + 2 * out_bytes * tm * tn
                        + (2 * 4 * tm * tn if has_res else 0))
                if need > MM_TILE_BUDGET_BYTES:
                    continue
                key = (-(1.0 / tm + 1.0 / tn), tk)
                if best is None or key > best[0]:
                    best = (key, (tm, tn, tk))
    assert best is not None, (m, n, k)
    return best[1]


def _mm(name, a, b, *, ta=False, tb=False, a2=None, b2=None, res=None, alpha=1.0, out_dtype=BF16, queue=None):
    m, k = (a.shape[1], a.shape[0]) if ta else a.shape
    n, kb = (b.shape[0], b.shape[1]) if tb else (b.shape[1], b.shape[0])
    assert k == kb, (name, a.shape, b.shape)
    pairs = 1 if a2 is None else 2
    if pairs == 2:
        assert a2.shape == a.shape and b2.shape == b.shape
    tm, tn, tk = _mm_tiles(m, n, k, pairs, jnp.dtype(out_dtype).itemsize, res is not None)
    nk = k // tk
    a_spec = (pl.BlockSpec((tk, tm), lambda i, j, kk: (kk, i)) if ta
              else pl.BlockSpec((tm, tk), lambda i, j, kk: (i, kk)))
    b_spec = (pl.BlockSpec((tn, tk), lambda i, j, kk: (j, kk)) if tb
              else pl.BlockSpec((tk, tn), lambda i, j, kk: (kk, j)))
    dims = (((0 if ta else 1,), (1 if tb else 0,)), ((), ()))
    operands, specs = [a, b], [a_spec, b_spec]
    if pairs == 2:
        operands += [a2, b2]
        specs += [a_spec, b_spec]
    n_mm = len(operands)
    if res is not None:
        operands.append(res)
        specs.append(pl.BlockSpec((tm, tn), lambda i, j, kk: (i, j)))

    def body(*refs):
        o_ref, acc_ref = refs[-2], refs[-1]
        kk = pl.program_id(2)

        @pl.when(kk == 0)
        def _():
            acc_ref[...] = jnp.zeros_like(acc_ref)

        acc_ref[...] += lax.dot_general(refs[0][...], refs[1][...], dims, preferred_element_type=F32)
        if pairs == 2:
            acc_ref[...] += lax.dot_general(refs[2][...], refs[3][...], dims, preferred_element_type=F32)

        @pl.when(kk == nk - 1)
        def _():
            out = acc_ref[...] * alpha
            if res is not None:
                out = out + refs[n_mm][...].astype(F32)
            o_ref[...] = out.astype(o_ref.dtype)

    return _call(body, name, (m // tm, n // tn, nk), specs,
                 [pl.BlockSpec((tm, tn), lambda i, j, kk: (i, j))], [jax.ShapeDtypeStruct((m, n), out_dtype)],
                 [pltpu.VMEM((tm, tn), F32)], ("parallel", "parallel", "arbitrary"), operands, queue)[0]


def _ffn_up(name, h, wg, wu, queue=None):
    s, d = h.shape
    f = wg.shape[1]
    tm, tn = _div_tile(s, 512), _div_tile(f, 512)

    def body(h_ref, wg_ref, wu_ref, a_ref, b_ref, z_ref):
        hv = h_ref[...]
        a = jnp.dot(hv, wg_ref[...], preferred_element_type=F32)
        b = jnp.dot(hv, wu_ref[...], preferred_element_type=F32)
        a_ref[...] = a.astype(BF16)
        b_ref[...] = b.astype(BF16)
        z_ref[...] = (a * jax.nn.sigmoid(a) * b).astype(BF16)

    tile = pl.BlockSpec((tm, tn), lambda i, j: (i, j))
    w_spec = pl.BlockSpec((d, tn), lambda i, j: (0, j))
    return _call(body, name, (s // tm, f // tn), [pl.BlockSpec((tm, d), lambda i, j: (i, 0)), w_spec, w_spec],
                 [tile, tile, tile], [jax.ShapeDtypeStruct((s, f), BF16)] * 3, [], ("parallel", "parallel"),
                 [h, wg, wu], queue)


def _ffn_bwd_mid(name, dy, wd, a, b, alpha, queue=None):
    s, d = dy.shape
    f = wd.shape[0]
    tm, tn = _div_tile(s, 512), _div_tile(f, 512)

    def body(dy_ref, wd_ref, a_ref, b_ref, da_ref, db_ref):
        dz = lax.dot_general(dy_ref[...], wd_ref[...], (((1,), (1,)), ((), ())),
                             preferred_element_type=F32) * alpha
        av = a_ref[...].astype(F32)
        sg = jax.nn.sigmoid(av)
        db_ref[...] = (dz * av * sg).astype(BF16)
        da_ref[...] = (dz * b_ref[...].astype(F32) * (sg * (1.0 + av * (1.0 - sg)))).astype(BF16)

    tile = pl.BlockSpec((tm, tn), lambda i, j: (i, j))
    return _call(body, name, (s // tm, f // tn),
                 [pl.BlockSpec((tm, d), lambda i, j: (i, 0)), pl.BlockSpec((tn, d), lambda i, j: (j, 0)), tile, tile],
                 [tile, tile], [jax.ShapeDtypeStruct((s, f), BF16)] * 2, [], ("parallel", "parallel"),
                 [dy, wd, a, b], queue)


def _rms_fwd(name, x, g, *, width=None, col=0):
    s = x.shape[0]
    w = x.shape[1] if width is None else width
    assert col % w == 0
    cb, tr = col // w, min(256, s)

    def body(x_ref, g_ref, h_ref):
        xf = x_ref[...].astype(F32)
        r = lax.rsqrt(jnp.mean(xf * xf, axis=-1, keepdims=True) + EPS)
        h_ref[...] = (xf * r * g_ref[...]).astype(BF16)

    return pl.pallas_call(
        body, name=name, grid=(s // tr,),
        in_specs=[pl.BlockSpec((tr, w), lambda i: (i, cb)), pl.BlockSpec((1, w), lambda i: (0, 0))],
        out_specs=pl.BlockSpec((tr, w), lambda i: (i, 0)),
        out_shape=jax.ShapeDtypeStruct((s, w), BF16),
        compiler_params=_params("parallel"))(x, g)


def _rms_bwd(name, dh, x, g, *, res=None, width=None, col=0):
    s = x.shape[0]
    w = x.shape[1] if width is None else width
    assert col % w == 0
    cb, tr = col // w, min(256, s)
    has_res = res is not None

    def body(*refs):
        dh_ref, x_ref, g_ref = refs[:3]
        dg_ref = refs[-1]
        i = pl.program_id(0)
        xf = x_ref[...].astype(F32)
        r = lax.rsqrt(jnp.mean(xf * xf, axis=-1, keepdims=True) + EPS)
        xh = xf * r
        d = dh_ref[...].astype(F32)
        dxh = d * g_ref[...]
        dx = r * (dxh - xh * jnp.mean(dxh * xh, axis=-1, keepdims=True))
        if has_res:
            dx = dx + refs[3][...]
            refs[4][...] = dx
            refs[5][...] = dx.astype(BF16)
        else:
            refs[3][...] = dx.astype(BF16)

        @pl.when(i == 0)
        def _():
            dg_ref[...] = jnp.zeros_like(dg_ref)

        dg_ref[0:1, :] += jnp.sum(d * xh, axis=0, keepdims=True)

    row = pl.BlockSpec((tr, w), lambda i: (i, 0))
    in_specs = [row, pl.BlockSpec((tr, w), lambda i: (i, cb)), pl.BlockSpec((1, w), lambda i: (0, 0))]
    operands = [dh, x, g]
    dg_spec = pl.BlockSpec((8, w), lambda i: (0, 0))
    dg_shape = jax.ShapeDtypeStruct((8, w), F32)
    if has_res:
        in_specs.append(row)
        operands.append(res)
        out_specs = [row, row, dg_spec]
        out_shape = [jax.ShapeDtypeStruct((s, w), F32), jax.ShapeDtypeStruct((s, w), BF16), dg_shape]
    else:
        out_specs = [row, dg_spec]
        out_shape = [jax.ShapeDtypeStruct((s, w), BF16), dg_shape]
    return pl.pallas_call(
        body, name=name, grid=(s // tr,), in_specs=in_specs, out_specs=out_specs, out_shape=out_shape,
        compiler_params=_params("arbitrary"))(*operands)


def _loss_head(name, x, g, target):
    s, d = x.shape
    tr = min(256, s)
    n = s // tr

    def body(x_ref, g_ref, t_ref, dx_ref, dxb_ref, dg_ref, loss_ref, sq_ref):
        i = pl.program_id(0)
        xf = x_ref[...]
        r = lax.rsqrt(jnp.mean(xf * xf, axis=-1, keepdims=True) + EPS)
        xh = xf * r
        gv = g_ref[...]
        err = xh * gv - t_ref[...]
        dy = err * (1.0 / d)
        dxh = dy * gv
        dx = r * (dxh - xh * jnp.mean(dxh * xh, axis=-1, keepdims=True))
        dx_ref[...] = dx
        dxb_ref[...] = dx.astype(BF16)

        @pl.when(i == 0)
        def _():
            dg_ref[...] = jnp.zeros_like(dg_ref)
            sq_ref[...] = jnp.zeros_like(sq_ref)

        dg_ref[0:1, :] += jnp.sum(dy * xh, axis=0, keepdims=True)
        sq_ref[...] += jnp.sum(err * err, axis=0, keepdims=True)

        @pl.when(i == n - 1)
        def _():
            total = jnp.sum(sq_ref[...], axis=1, keepdims=True) * (0.5 / d)
            loss_ref[...] = jnp.broadcast_to(total, loss_ref.shape)

    row = pl.BlockSpec((tr, d), lambda i: (i, 0))
    return pl.pallas_call(
        body, name=name, grid=(n,),
        in_specs=[row, pl.BlockSpec((1, d), lambda i: (0, 0)), row],
        out_specs=[row, row, pl.BlockSpec((8, d), lambda i: (0, 0)), pl.BlockSpec((8, LANES), lambda i: (0, 0))],
        out_shape=[jax.ShapeDtypeStruct((s, d), F32), jax.ShapeDtypeStruct((s, d), BF16),
                   jax.ShapeDtypeStruct((8, d), F32), jax.ShapeDtypeStruct((8, LANES), F32)],
        scratch_shapes=[pltpu.VMEM((1, d), F32)],
        compiler_params=_params("arbitrary"))(x, g, target)


def _rope(name, x, tabs, half, *, width, col=0, nblk=1, transpose=False, out_dtype=BF16):
    s = x.shape[0]
    assert col % width == 0
    cb0, tr = col // width, min(256, s)

    def body(x_ref, c_ref, s1_ref, s2_ref, y_ref):
        xf = x_ref[...].astype(F32)
        if transpose:
            y = (xf * c_ref[...] + pltpu.roll(xf * s1_ref[...], half, 1)
                 + pltpu.roll(xf * s2_ref[...], width - half, 1))
        else:
            y = (xf * c_ref[...] + pltpu.roll(xf, width - half, 1) * s1_ref[...]
                 + pltpu.roll(xf, half, 1) * s2_ref[...])
        y_ref[...] = y.astype(y_ref.dtype)

    tab = pl.BlockSpec((tr, width), lambda i, j: (i, 0))
    return pl.pallas_call(
        body, name=name, grid=(s // tr, nblk),
        in_specs=[pl.BlockSpec((tr, width), lambda i, j: (i, cb0 + j)), tab, tab, tab],
        out_specs=pl.BlockSpec((tr, width), lambda i, j: (i, j)),
        out_shape=jax.ShapeDtypeStruct((s, nblk * width), out_dtype),
        compiler_params=_params("parallel", "parallel"))(x, *tabs)


def _split3(v):
    hi = v.astype(BF16)
    r1 = v - hi.astype(F32)
    mid = r1.astype(BF16)
    lo = (r1 - mid.astype(F32)).astype(BF16)
    return hi, mid, lo


def _tri_matmul(tri, v):
    hi, mid, lo = _split3(v)
    return (jnp.dot(tri, hi, preferred_element_type=F32) + jnp.dot(tri, mid, preferred_element_type=F32)
            + jnp.dot(tri, lo, preferred_element_type=F32))


def _log_sigmoid(v):
    return jnp.minimum(v, 0.0) - jnp.log(1.0 + jnp.exp(-jnp.abs(v)))


def _gate_fwd(name, tail, bias, mask):
    s = tail.shape[0]
    tr = min(512, s)

    def body(t_ref, b_ref, m_ref, cum_ref, carry_ref):
        i = pl.program_id(0)

        @pl.when(i == 0)
        def _():
            carry_ref[...] = jnp.zeros_like(carry_ref)

        lf = _log_sigmoid(t_ref[...] + b_ref[...]) * m_ref[...]
        r = lax.broadcasted_iota(jnp.int32, (tr, tr), 0)
        c = lax.broadcasted_iota(jnp.int32, (tr, tr), 1)
        cum = _tri_matmul((r >= c).astype(BF16), lf) + carry_ref[...]
        cum_ref[...] = cum
        carry_ref[...] = cum[tr - 1:tr, :]

    vec = pl.BlockSpec((1, LANES), lambda i: (0, 0))
    return pl.pallas_call(
        body, name=name, grid=(s // tr,),
        in_specs=[pl.BlockSpec((tr, LANES), lambda i: (i, 0)), vec, vec],
        out_specs=pl.BlockSpec((tr, LANES), lambda i: (i, 0)),
        out_shape=jax.ShapeDtypeStruct((s, LANES), F32),
        scratch_shapes=[pltpu.VMEM((1, LANES), F32)],
        compiler_params=_params("arbitrary"))(tail, bias, mask)


def _gate_bwd(name, dcum, tail, bias, mask):
    s = tail.shape[0]
    tr = min(512, s)
    n = s // tr

    def body(dc_ref, t_ref, b_ref, m_ref, dt_ref, db_ref, carry_ref):
        i = pl.program_id(0)

        @pl.when(i == 0)
        def _():
            carry_ref[...] = jnp.zeros_like(carry_ref)
            db_ref[...] = jnp.zeros_like(db_ref)

        r = lax.broadcasted_iota(jnp.int32, (tr, tr), 0)
        c = lax.broadcasted_iota(jnp.int32, (tr, tr), 1)
        dlf = _tri_matmul((r <= c).astype(BF16), dc_ref[...]) + carry_ref[...]
        carry_ref[...] = dlf[0:1, :]
        v = t_ref[...] + b_ref[...]
        dt = dlf * jax.nn.sigmoid(-v) * m_ref[...]
        dt_ref[...] = dt
        db_ref[0:1, :] += jnp.sum(dt, axis=0, keepdims=True)

    vec = pl.BlockSpec((1, LANES), lambda i: (0, 0))
    rev = pl.BlockSpec((tr, LANES), lambda i: (n - 1 - i, 0))
    return pl.pallas_call(
        body, name=name, grid=(n,),
        in_specs=[rev, rev, vec, vec],
        out_specs=[rev, pl.BlockSpec((8, LANES), lambda i: (0, 0))],
        out_shape=[jax.ShapeDtypeStruct((s, LANES), F32), jax.ShapeDtypeStruct((8, LANES), F32)],
        scratch_shapes=[pltpu.VMEM((1, LANES), F32)],
        compiler_params=_params("arbitrary"))(dcum, tail, bias, mask)


NT = (((1,), (1,)), ((), ()))
TN = (((0,), (0,)), ((), ()))


def _weights_and_scores(mode, scale, t, qi, kb, q1, k1, q2, k2, cq, ck, masked=True):
    sc = lax.dot_general(q1, k1, NT, preferred_element_type=F32)
    if q2 is not None:
        sc = sc + lax.dot_general(q2, k2, NT, preferred_element_type=F32)
    sc = sc * scale
    if cq is not None:
        sc = sc + (cq - ck)
    if not masked:
        return sc, None
    row = lax.broadcasted_iota(jnp.int32, (t, t), 0)
    col = lax.broadcasted_iota(jnp.int32, (t, t), 1)
    dist = row - col + (qi - kb) * t
    back = dist >= 0
    if mode == "dil":
        w = jnp.zeros((t, t), F32)
        for window, dil in DIL_BRANCHES:
            w = w + (back & (dist <= window) & ((dist & (dil - 1)) == 0)).astype(F32)
    else:
        w = back.astype(F32)
    return jnp.where(w > 0.0, sc, NEG), w


def _causal_steps(mode, diagonal, step):
    if mode == "dil":
        step(True)
    else:
        pl.when(jnp.logical_not(diagonal))(functools.partial(step, False))
        pl.when(diagonal)(functools.partial(step, True))


def _attn_steps(mode, nh, s, key_major):
    t = min(ATTN_TILE, s)
    nq = s // t
    reach = DIL_REACH // t if mode == "dil" else nq
    rows = []
    if key_major:
        for j in range(nq):
            hi = min(nq - 1, j + reach)
            rows += [(h, qb, j, int(qb == hi)) for h in range(nh) for qb in range(j, hi + 1)]
    else:
        for h in range(nh):
            for i in range(nq):
                lo = max(0, i - reach)
                rows += [(h, i, kb, int(kb == lo)) for kb in range(lo, i + 1)]
    tables = [jnp.asarray([r[k] for r in rows], jnp.int32) for k in range(4)]
    return t, len(rows), tables


N_STEP_TABLES = 4


def _attn_operands(mode, nh, t, ops, hf, qf, kf):
    def cols(width, base, rows):
        return pl.BlockSpec((t, width), lambda *g: (rows(*g), base + hf(*g)))

    if mode == "fox":
        pm, ccol, crow = ops
        return ([pm, pm, pm, ccol, crow],
                [cols(HEAD, 0, qf), cols(HEAD, nh, kf), cols(HEAD, 2 * nh, kf),
                 pl.BlockSpec((None, t, 1), lambda *g: (hf(*g), qf(*g), 0)),
                 pl.BlockSpec((None, 1, t), lambda *g: (hf(*g), 0, kf(*g)))])
    if mode == "mla":
        qb, kv, kr = ops
        return ([qb, kv, kr],
                [cols(2 * HEAD, 0, qf), cols(2 * HEAD, 0, kf), pl.BlockSpec((t, HEAD), lambda *g: (kf(*g), 0))])
    q, k, pm, vbase = ops
    return [q, k, pm], [cols(HEAD, 0, qf), cols(HEAD, 0, kf), cols(HEAD, vbase, kf)]


def _unpack(mode, refs):
    if mode == "fox":
        return refs[0][...], refs[1][...], refs[2][...], None, None, refs[3][...], refs[4][...]
    if mode == "mla":
        q, kv = refs[0][...], refs[1][...]
        return q[:, :HEAD], kv[:, :HEAD], kv[:, HEAD:], q[:, HEAD:], refs[2][...], None, None
    return refs[0][...], refs[1][...], refs[2][...], None, None, None, None


def _attn_fwd(name, mode, nh, s, ops, scale, queue=None):
    t, n_steps, tables = _attn_steps(mode, nh, s, key_major=False)
    hf = lambda p, ht, qt, kt, et: ht[p]
    qf = lambda p, ht, qt, kt, et: qt[p]
    kf = lambda p, ht, qt, kt, et: kt[p]
    operands, specs = _attn_operands(mode, nh, t, ops, hf, qf, kf)
    n_in = len(operands)

    def body(ht, qt, kt, et, *refs):
        o_ref, lse_ref, m_ref, l_ref, acc_ref = refs[n_in:]
        sid = pl.program_id(0)
        i, kb = qt[sid], kt[sid]

        @pl.when(et[sid] == 1)
        def _():
            m_ref[...] = jnp.full_like(m_ref, NEG)
            l_ref[...] = jnp.zeros_like(l_ref)
            acc_ref[...] = jnp.zeros_like(acc_ref)

        def step(masked):
            q1, k1, v, q2, k2, cq, ck = _unpack(mode, refs)
            sc, w = _weights_and_scores(mode, scale, t, i, kb, q1, k1, q2, k2, cq, ck, masked)
            m_old = m_ref[...]
            m_new = jnp.maximum(m_old, jnp.max(sc, axis=1, keepdims=True))
            p = jnp.exp(sc - m_new)
            if w is not None:
                p = w * p
            a = jnp.exp(m_old - m_new)
            l_ref[...] = a * l_ref[...] + jnp.sum(p, axis=1, keepdims=True)
            acc_ref[...] = a * acc_ref[...] + jnp.dot(p.astype(BF16), v, preferred_element_type=F32)
            m_ref[...] = m_new

        _causal_steps(mode, kb == i, step)

        @pl.when(kb == i)
        def _():
            o_ref[...] = (acc_ref[...] / l_ref[...]).astype(BF16)
            lse_ref[...] = m_ref[...] + jnp.log(l_ref[...])

    return _call(body, name, (n_steps,), specs,
                 [pl.BlockSpec((t, HEAD), lambda *g: (qf(*g), hf(*g))),
                  pl.BlockSpec((None, t, 1), lambda *g: (hf(*g), qf(*g), 0))],
                 [jax.ShapeDtypeStruct((s, nh * HEAD), BF16), jax.ShapeDtypeStruct((nh, s, 1), F32)],
                 [pltpu.VMEM((t, 1), F32), pltpu.VMEM((t, 1), F32), pltpu.VMEM((t, HEAD), F32)],
                 ("arbitrary",), operands, queue, prefetch=tables)


def _attn_bwd_q(name, mode, nh, s, ops, scale, o, do, do_base, lse):
    t, n_steps, tables = _attn_steps(mode, nh, s, key_major=False)
    hf = lambda p, ht, qt, kt, et: ht[p]
    qf = lambda p, ht, qt, kt, et: qt[p]
    kf = lambda p, ht, qt, kt, et: kt[p]
    operands, specs = _attn_operands(mode, nh, t, ops, hf, qf, kf)
    operands += [o, do, lse]
    specs += [pl.BlockSpec((t, HEAD), lambda *g: (qf(*g), do_base + hf(*g))),
              pl.BlockSpec((t, HEAD), lambda *g: (qf(*g), do_base + hf(*g))),
              pl.BlockSpec((None, t, 1), lambda *g: (hf(*g), qf(*g), 0))]
    n_in = len(operands)
    wq = 2 * HEAD if mode == "mla" else HEAD

    fox = mode == "fox"

    def body(ht, qt, kt, et, *refs):
        o_ref, do_ref, lse_ref = refs[n_in - 3:n_in]
        dq_ref = refs[n_in]
        acc_ref, delta_ref, rows_ref = refs[-3:]
        sid = pl.program_id(0)
        i, kb = qt[sid], kt[sid]

        @pl.when(et[sid] == 1)
        def _():
            acc_ref[...] = jnp.zeros_like(acc_ref)
            rows_ref[...] = jnp.zeros_like(rows_ref)
            delta_ref[...] = jnp.sum(o_ref[...].astype(F32) * do_ref[...].astype(F32), axis=1, keepdims=True)

        def step(masked):
            q1, k1, v, q2, k2, cq, ck = _unpack(mode, refs)
            sc, w = _weights_and_scores(mode, scale, t, i, kb, q1, k1, q2, k2, cq, ck, masked)
            p = jnp.exp(sc - lse_ref[...])
            if w is not None:
                p = w * p
            dp = lax.dot_general(do_ref[...], v, NT, preferred_element_type=F32)
            ds32 = p * (dp - delta_ref[...])
            ds = ds32.astype(BF16)
            if fox:
                rows_ref[...] += jnp.sum(ds32, axis=1, keepdims=True)
            if mode == "mla":
                acc_ref[:, :HEAD] += jnp.dot(ds, k1, preferred_element_type=F32)
                acc_ref[:, HEAD:] += jnp.dot(ds, k2, preferred_element_type=F32)
            else:
                acc_ref[...] += jnp.dot(ds, k1, preferred_element_type=F32)

        _causal_steps(mode, kb == i, step)

        @pl.when(kb == i)
        def _():
            dq_ref[...] = (acc_ref[...] * scale).astype(BF16)
            if fox:
                refs[n_in + 1][...] = rows_ref[...]

    out_specs = [pl.BlockSpec((t, wq), lambda *g: (qf(*g), hf(*g)))]
    out_shape = [jax.ShapeDtypeStruct((s, nh * wq), BF16)]
    if fox:
        out_specs.append(pl.BlockSpec((None, t, 1), lambda *g: (hf(*g), qf(*g), 0)))
        out_shape.append(jax.ShapeDtypeStruct((nh, s, 1), F32))
    grid_spec = pltpu.PrefetchScalarGridSpec(
        num_scalar_prefetch=N_STEP_TABLES, grid=(n_steps,), in_specs=specs, out_specs=out_specs,
        scratch_shapes=[pltpu.VMEM((t, wq), F32), pltpu.VMEM((t, 1), F32), pltpu.VMEM((t, 1), F32)])
    out = pl.pallas_call(body, name=name, grid_spec=grid_spec, out_shape=out_shape,
                         compiler_params=_params("arbitrary"))(*tables, *operands)
    return out if fox else out[0]


def _attn_bwd_kv(name, mode, nh, s, ops, scale, o, do, do_base, lse):
    t, n_steps, tables = _attn_steps(mode, nh, s, key_major=True)
    hf = lambda p, ht, qt, kt, et: ht[p]
    qf = lambda p, ht, qt, kt, et: qt[p]
    kf = lambda p, ht, qt, kt, et: kt[p]
    operands, specs = _attn_operands(mode, nh, t, ops, hf, qf, kf)
    operands += [o, do, lse]
    specs += [pl.BlockSpec((t, HEAD), lambda *g: (qf(*g), do_base + hf(*g))),
              pl.BlockSpec((t, HEAD), lambda *g: (qf(*g), do_base + hf(*g))),
              pl.BlockSpec((None, t, 1), lambda *g: (hf(*g), qf(*g), 0))]
    n_in = len(operands)
    head_tile = pl.BlockSpec((t, HEAD), lambda *g: (kf(*g), hf(*g)))
    if mode == "fox":
        out_specs = [head_tile, head_tile, pl.BlockSpec((None, 1, t), lambda *g: (hf(*g), 0, kf(*g)))]
        out_shape = [jax.ShapeDtypeStruct((s, nh * HEAD), BF16)] * 2 + [jax.ShapeDtypeStruct((nh, 1, s), F32)]
        scratch = [pltpu.VMEM((t, HEAD), F32), pltpu.VMEM((t, HEAD), F32), pltpu.VMEM((1, t), F32)]
    elif mode == "mla":
        out_specs = [pl.BlockSpec((t, 2 * HEAD), lambda *g: (kf(*g), hf(*g))),
                     pl.BlockSpec((t, HEAD), lambda *g: (kf(*g), 0))]
        out_shape = [jax.ShapeDtypeStruct((s, nh * 2 * HEAD), BF16), jax.ShapeDtypeStruct((s, HEAD), F32)]
        scratch = [pltpu.VMEM((t, HEAD), F32), pltpu.VMEM((t, HEAD), F32), pltpu.VMEM((t, HEAD), F32)]
    else:
        out_specs = [head_tile, head_tile]
        out_shape = [jax.ShapeDtypeStruct((s, nh * HEAD), BF16)] * 2
        scratch = [pltpu.VMEM((t, HEAD), F32), pltpu.VMEM((t, HEAD), F32)]
    n_out = len(out_specs)

    def body(ht, qt, kt, et, *refs):
        o_ref, do_ref, lse_ref = refs[n_in - 3:n_in]
        outs = refs[n_in:n_in + n_out]
        dk_acc, dv_acc = refs[n_in + n_out], refs[n_in + n_out + 1]
        extra = refs[n_in + n_out + 2] if mode != "dil" else None
        sid = pl.program_id(0)
        j, h, qb = kt[sid], ht[sid], qt[sid]
        is_first, is_last = qb == j, et[sid] == 1

        @pl.when(is_first)
        def _():
            dk_acc[...] = jnp.zeros_like(dk_acc)
            dv_acc[...] = jnp.zeros_like(dv_acc)
            if mode == "fox":
                extra[...] = jnp.zeros_like(extra)

        if mode == "mla":
            @pl.when(is_first & (h == 0))
            def _():
                extra[...] = jnp.zeros_like(extra)

        def step(masked):
            q1, k1, v, q2, k2, cq, ck = _unpack(mode, refs)
            sc, w = _weights_and_scores(mode, scale, t, qb, j, q1, k1, q2, k2, cq, ck, masked)
            p = jnp.exp(sc - lse_ref[...])
            if w is not None:
                p = w * p
            dov = do_ref[...]
            delta = jnp.sum(o_ref[...].astype(F32) * dov.astype(F32), axis=1, keepdims=True)
            dv_acc[...] += lax.dot_general(p.astype(BF16), dov, TN, preferred_element_type=F32)
            dp = lax.dot_general(dov, v, NT, preferred_element_type=F32)
            ds = p * (dp - delta)
            dsb = ds.astype(BF16)
            dk_acc[...] += lax.dot_general(dsb, q1, TN, preferred_element_type=F32)
            if mode == "mla":
                extra[...] += lax.dot_general(dsb, q2, TN, preferred_element_type=F32)
            if mode == "fox":
                extra[...] -= jnp.sum(ds, axis=0, keepdims=True)

        _causal_steps(mode, is_first, step)

        @pl.when(is_last)
        def _():
            if mode == "mla":
                outs[0][:, :HEAD] = (dk_acc[...] * scale).astype(BF16)
                outs[0][:, HEAD:] = dv_acc[...].astype(BF16)
            else:
                outs[0][...] = (dk_acc[...] * scale).astype(BF16)
                outs[1][...] = dv_acc[...].astype(BF16)
            if mode == "fox":
                outs[2][...] = extra[...]

        if mode == "mla":
            @pl.when(is_last & (h == nh - 1))
            def _():
                outs[1][...] = extra[...] * scale

    grid_spec = pltpu.PrefetchScalarGridSpec(
        num_scalar_prefetch=N_STEP_TABLES, grid=(n_steps,), in_specs=specs, out_specs=out_specs,
        scratch_shapes=scratch)
    return pl.pallas_call(body, name=name, grid_spec=grid_spec, out_shape=out_shape,
                          compiler_params=_params("arbitrary"))(*tables, *operands)


ROW_TILE_BYTES = 2 ** 20


def _row_tile(rows, cols):
    best = 0
    for t in range(16, rows + 1, 16):
        if rows % t == 0 and t * cols * 4 <= ROW_TILE_BYTES:
            best = t
    return best if best else rows


def _adamw(name, w, g, m, v):
    nl, r, c = w.shape
    tr = _row_tile(r, c)

    def body(w_ref, g_ref, m_ref, v_ref, d_ref, nm_ref, nv_ref):
        gv = g_ref[...]
        nm = ADAM_B1 * m_ref[...] + (1.0 - ADAM_B1) * gv
        nv = ADAM_B2 * v_ref[...] + (1.0 - ADAM_B2) * jnp.square(gv)
        m_hat = nm / (1.0 - ADAM_B1 ** ADAM_STEP)
        v_hat = nv / (1.0 - ADAM_B2 ** ADAM_STEP)
        d_ref[...] = -ADAM_LR * (m_hat / (jnp.sqrt(v_hat) + ADAM_EPS) + ADAM_WD * w_ref[...])
        nm_ref[...] = nm
        nv_ref[...] = nv

    blk = pl.BlockSpec((None, tr, c), lambda l, i: (l, i, 0))
    return pl.pallas_call(
        body, name=name, grid=(nl, r // tr), in_specs=[blk] * 4, out_specs=[blk] * 3,
        out_shape=[jax.ShapeDtypeStruct(w.shape, F32)] * 3,
        compiler_params=_params("parallel", "parallel"))(w, g, m, v)


def _add_half(name, g, recv, c_arr):
    nl, r, c = g.shape
    hl = nl // 2
    tr = _row_tile(r, c)

    def body(c_ref, g_ref, r_ref, o_ref):
        o_ref[...] = (g_ref[...] + r_ref[...]).astype(BF16)

    grid_spec = pltpu.PrefetchScalarGridSpec(
        num_scalar_prefetch=1, grid=(hl, r // tr),
        in_specs=[pl.BlockSpec((None, tr, c), lambda l, i, c_ref: (c_ref[0] * hl + l, i, 0)),
                  pl.BlockSpec((None, tr, c), lambda l, i, c_ref: (l, i, 0))],
        out_specs=pl.BlockSpec((None, tr, c), lambda l, i, c_ref: (l, i, 0)))
    return pl.pallas_call(
        body, name=name, grid_spec=grid_spec, out_shape=jax.ShapeDtypeStruct((hl, r, c), BF16),
        compiler_params=_params("parallel", "parallel"))(c_arr, g, recv)


def _sum_chips(name, pst, got, chip_arr):
    _, nl, r, c = pst.shape
    tr = _row_tile(r, c)

    def body(chip_ref, own_ref, g0_ref, g1_ref, g2_ref, o_ref):
        o_ref[...] = ((own_ref[...].astype(F32) + g0_ref[...].astype(F32)) + g1_ref[...].astype(F32)
                      ) + g2_ref[...].astype(F32)

    def slot(k):
        return pl.BlockSpec((None, None, tr, c), lambda l, i, chip_ref: (k, l, i, 0))

    grid_spec = pltpu.PrefetchScalarGridSpec(
        num_scalar_prefetch=1, grid=(nl, r // tr),
        in_specs=[pl.BlockSpec((None, None, tr, c), lambda l, i, chip_ref: (chip_ref[0], l, i, 0)),
                  slot(0), slot(1), slot(2)],
        out_specs=pl.BlockSpec((None, tr, c), lambda l, i, chip_ref: (l, i, 0)))
    return pl.pallas_call(
        body, name=name, grid_spec=grid_spec, out_shape=jax.ShapeDtypeStruct((nl, r, c), F32),
        compiler_params=_params("parallel", "parallel"))(chip_arr, pst, got, got, got)


def _adamw_layers(name, w, mines, others, m, v, c_arr):
    nl, r, c = w.shape
    hr = r // 2
    tr = _row_tile(hr, c)
    nb = hr // tr

    def body(c_ref, w_ref, m_ref, v_ref, *refs):
        halves, (g_ref, d_ref, nm_ref, nv_ref) = refs[:2 * nl], refs[2 * nl:]
        layer, is_mine = pl.program_id(0), pl.program_id(1) == c_ref[0]
        gv = jnp.where(is_mine, halves[0][...], halves[nl][...])
        for k in range(1, nl):
            gv = jnp.where(layer == k, jnp.where(is_mine, halves[k][...], halves[nl + k][...]), gv)
        nm = ADAM_B1 * m_ref[...] + (1.0 - ADAM_B1) * gv
        nv = ADAM_B2 * v_ref[...] + (1.0 - ADAM_B2) * jnp.square(gv)
        m_hat = nm / (1.0 - ADAM_B1 ** ADAM_STEP)
        v_hat = nv / (1.0 - ADAM_B2 ** ADAM_STEP)
        g_ref[...] = gv
        d_ref[...] = -ADAM_LR * (m_hat / (jnp.sqrt(v_hat) + ADAM_EPS) + ADAM_WD * w_ref[...])
        nm_ref[...] = nm
        nv_ref[...] = nv

    def half_of(k):
        return pl.BlockSpec((None, tr, c), lambda l, h, i, c_ref: (0, jnp.where(l == k, i, 0), 0))

    blk = pl.BlockSpec((None, tr, c), lambda l, h, i, c_ref: (l, h * nb + i, 0))
    grid_spec = pltpu.PrefetchScalarGridSpec(
        num_scalar_prefetch=1, grid=(nl, 2, nb),
        in_specs=[blk, blk, blk] + [half_of(k) for k in range(nl)] * 2, out_specs=[blk] * 4)
    return pl.pallas_call(
        body, name=name, grid_spec=grid_spec, out_shape=[jax.ShapeDtypeStruct(w.shape, F32)] * 4,
        compiler_params=_params("parallel", "parallel", "parallel"))(c_arr, w, m, v, *mines, *others)


def _sum_slots(name, rc, order, out_dtype=F32):
    _, nl, r, c = rc.shape
    tr = _row_tile(r, c)

    def body(*refs):
        acc = refs[0][...].astype(F32)
        for ref in refs[1:-1]:
            acc = acc + ref[...].astype(F32)
        refs[-1][...] = acc.astype(out_dtype)

    def slot(k):
        return pl.BlockSpec((None, None, tr, c), lambda l, i: (k, l, i, 0))

    return pl.pallas_call(
        body, name=name, grid=(nl, r // tr), in_specs=[slot(k) for k in order],
        out_specs=pl.BlockSpec((None, tr, c), lambda l, i: (l, i, 0)),
        out_shape=jax.ShapeDtypeStruct((nl, r, c), out_dtype),
        compiler_params=_params("parallel", "parallel"))(*([rc] * len(order)))


class _Comm:
    def __init__(self, name, ins, out_shapes, build, n_first, n_then=0, n_local=0):
        self.name, self.ins, self.out_shapes, self.build = name, list(ins), list(out_shapes), build
        self.n_first, self.n_then, self.n_local = n_first, n_then, n_local


def _comm_steps(comm, in_refs, out_refs, send_sems, recv_sems, local_sems=None):
    def descriptors(with_then):
        x, y, c = lax.axis_index("x"), lax.axis_index("y"), lax.axis_index("c")
        first, then, local = comm.build(in_refs, out_refs, x, y, c)
        assert (len(first), len(then), len(local)) == (comm.n_first, comm.n_then, comm.n_local)

        def remote(k, src, dst, flips):
            fx, fy, fc = flips
            peer = (1 - x if fx else x, 1 - y if fy else y, 1 - c if fc else c)
            return pltpu.make_async_remote_copy(src_ref=src, dst_ref=dst, send_sem=send_sems.at[k],
                                                recv_sem=recv_sems.at[k], device_id=peer, device_id_type=MESH)

        f = [remote(k, s, d, fl) for k, (s, d, fl) in enumerate(first)]
        t = [(remote(len(first) + k, s, d, fl), dep) for k, (s, d, fl, dep) in enumerate(then)] if with_then else []
        lc = [pltpu.make_async_copy(s, d, local_sems.at[k]) for k, (s, d) in enumerate(local)]
        return f, t, lc

    def start():
        f, _, lc = descriptors(False)
        for cp in f + lc:
            cp.start()

    def finish():
        f, t, lc = descriptors(True)
        for k, cp in enumerate(f):
            cp.wait_recv()
            for cp2, dep in t:
                if dep == k:
                    cp2.start()
        for cp2, _ in t:
            cp2.wait_recv()
        for cp in f + [cp2 for cp2, _ in t]:
            cp.wait_send()
        for cp in lc:
            cp.wait()

    return start, finish


def _run_comm(comm):
    n_in, n_out = len(comm.ins), len(comm.out_shapes)

    def body(*refs):
        start, finish = _comm_steps(comm, refs[:n_in], refs[n_in:n_in + n_out], *refs[n_in + n_out:])
        start()
        finish()

    n_sem = comm.n_first + comm.n_then
    return pl.pallas_call(
        body, name=comm.name, in_specs=[ANY] * n_in, out_specs=[ANY] * n_out, out_shape=comm.out_shapes,
        scratch_shapes=[pltpu.SemaphoreType.DMA((n_sem,)), pltpu.SemaphoreType.DMA((n_sem,)),
                        pltpu.SemaphoreType.DMA((max(comm.n_local, 1),))])(*comm.ins)


def _call(body, name, grid, in_specs, out_specs, out_shape, scratch, semantics, operands, queue=None, prefetch=()):
    n_pre = len(prefetch)

    def run(kernel_body, ins, outs, shapes, scratches, sems, args):
        grid_spec = pltpu.PrefetchScalarGridSpec(num_scalar_prefetch=n_pre, grid=grid, in_specs=ins,
                                                 out_specs=outs, scratch_shapes=scratches)
        return pl.pallas_call(kernel_body, name=name, grid_spec=grid_spec, out_shape=shapes,
                              compiler_params=_params(*sems))(*prefetch, *args)

    job = queue.take() if queue is not None else None
    if job is None:
        return run(body, list(in_specs), list(out_specs), list(out_shape), list(scratch), semantics, operands)
    comm = job[1]
    assert comm.n_local == 0
    n_in, n_out, n_cin, n_cout = len(in_specs), len(out_specs), len(comm.ins), len(comm.out_shapes)
    n_sem = comm.n_first + comm.n_then

    def hosted(*refs):
        pre, refs = refs[:n_pre], refs[n_pre:]
        a, b = n_in, n_in + n_cin
        c, d = b + n_out, b + n_out + n_cout
        start, finish = _comm_steps(comm, refs[a:b], refs[c:d], refs[-2], refs[-1])
        ids = [pl.program_id(k) for k in range(len(grid))]
        is_first = functools.reduce(jnp.logical_and, [i == 0 for i in ids])
        is_last = functools.reduce(jnp.logical_and, [i == g - 1 for i, g in zip(ids, grid)])
        pl.when(is_first)(start)
        body(*pre, *refs[:a], *refs[b:c], *refs[d:len(refs) - 2])
        pl.when(is_last)(finish)

    name = name + "_and_" + comm.name
    outs = run(hosted, list(in_specs) + [ANY] * n_cin, list(out_specs) + [ANY] * n_cout,
               list(out_shape) + comm.out_shapes,
               list(scratch) + [pltpu.SemaphoreType.DMA((n_sem,)), pltpu.SemaphoreType.DMA((n_sem,))],
               ("arbitrary",) * len(grid), list(operands) + comm.ins)
    queue.done(job, outs[n_out:])
    return outs[:n_out]


class _CommQueue:
    def __init__(self):
        self.jobs = []

    def add(self, task):
        self.done([task, None], None, first=True)

    def take(self):
        return self.jobs.pop(0) if self.jobs else None

    def done(self, job, results, first=False):
        try:
            comm = next(job[0]) if first else job[0].send(results)
        except StopIteration:
            return
        self.jobs.append([job[0], comm])

    def drain(self, until=lambda: False):
        while self.jobs and not until():
            job = self.take()
            self.done(job, _run_comm(job[1]))


CHIP_FLIPS = ((1, 0), (0, 1), (1, 1))


def _flip(v, f):
    return 1 - v if f else v


def _gather_comm(name, shards):
    def build(ins, outs, x, y, c):
        first, then = [], []
        me = 2 * x + y
        for w_ref, out_ref in zip(ins, outs):
            base = len(first)
            for j, (fx, fy) in enumerate(CHIP_FLIPS):
                theirs = out_ref.at[2 * _flip(x, fx) + _flip(y, fy), pl.ds(c, 1)]
                first.append((w_ref.at[pl.ds(c, 1)], out_ref.at[me, pl.ds(c, 1)], (fx, fy, 0)))
                then.append((theirs, theirs, (0, 0, 1), base + j))
            first.append((w_ref, out_ref.at[me], (0, 0, 1)))
        return first, then, []

    n = len(shards)
    return _Comm(name, shards, [jax.ShapeDtypeStruct((4,) + w.shape, w.dtype) for w in shards], build, 4 * n, 3 * n)


def _sibling_comm(name, gs):
    def build(ins, outs, x, y, c):
        return [(g.at[pl.ds(1 - c, 1)], o, (0, 0, 1)) for g, o in zip(ins, outs)], [], []

    return _Comm(name, gs, [jax.ShapeDtypeStruct((1,) + g.shape[1:], g.dtype) for g in gs], build, len(gs))


def _chips_comm(name, psts):
    def build(ins, outs, x, y, c):
        first = []
        for p_ref, o_ref in zip(ins, outs):
            for k, (fx, fy) in enumerate(CHIP_FLIPS):
                first.append((p_ref.at[2 * _flip(x, fx) + _flip(y, fy)], o_ref.at[k], (fx, fy, 0)))
        return first, [], []

    return _Comm(name, psts, [jax.ShapeDtypeStruct((3,) + p.shape[1:], p.dtype) for p in psts], build,
                 3 * len(psts))


def _swap_comm(name, ss):
    def build(ins, outs, x, y, c):
        return [(s, o, (0, 0, 1)) for s, o in zip(ins, outs)], [], []

    return _Comm(name, ss, [jax.ShapeDtypeStruct(s.shape, s.dtype) for s in ss], build, len(ss))


def _gather_all(name, blk):
    def build(ins, outs, x, y, c):
        mine = outs[0].at[4 * x + 2 * y + c]
        first = [(ins[0], mine, (m >> 2 & 1, m >> 1 & 1, m & 1)) for m in range(1, 8)]
        return first, [], [(ins[0], mine)]

    return _run_comm(_Comm(name, [blk], [jax.ShapeDtypeStruct((8,) + blk.shape, blk.dtype)], build, 7, 0, 1))[0]


class _Dims:
    def __init__(self, d, in_w):
        self.d = d
        self.fox_h = (d // 4) // HEAD
        self.mla_h = (d // 2) // HEAD
        self.dil_h = (d // 4) // HEAD
        self.gw = d // 4
        self.rank = d // 4
        gw, fh = self.gw, self.fox_h
        sizes = (gw, gw, gw, fh, self.rank, self.rank, MLA_ROPE, gw, gw, gw)
        assert sum(sizes) == in_w
        offs = [0]
        for z in sizes:
            offs.append(offs[-1] + z)
        self.nat = dict(zip(("fq", "fk", "fv", "fl", "cq", "ckv", "kr", "dq", "dk", "dv"), zip(offs[:-1], sizes)))
        self.main_order = ("fq", "fk", "fv", "cq", "ckv", "dq", "dk", "dv")
        self.main_w = 8 * gw
        self.col = {n: i * gw for i, n in enumerate(self.main_order)}

    def align_w_in(self, w):
        parts = [w[:, self.nat[n][0]:self.nat[n][0] + self.nat[n][1]] for n in self.main_order]
        parts += [w[:, self.nat["kr"][0]:self.nat["kr"][0] + MLA_ROPE],
                  w[:, self.nat["fl"][0]:self.nat["fl"][0] + self.fox_h],
                  jnp.zeros((w.shape[0], LANES - MLA_ROPE - self.fox_h), w.dtype)]
        return jnp.concatenate(parts, axis=1)

    def unalign_w_in(self, g):
        gw, t0 = self.gw, self.main_w
        src = {n: g[:, self.col[n]:self.col[n] + gw] for n in self.main_order}
        src["kr"] = g[:, t0:t0 + MLA_ROPE]
        src["fl"] = g[:, t0 + FORGET_LANE:t0 + FORGET_LANE + self.fox_h]
        return jnp.concatenate([src[n] for n in ("fq", "fk", "fv", "fl", "cq", "ckv", "kr", "dq", "dk", "dv")], axis=1)

    def pad_w_uq(self, w):
        r = w.shape[0]
        w3 = w.reshape(r, self.mla_h, HEAD + MLA_ROPE)
        return jnp.pad(w3, ((0, 0), (0, 0), (0, HEAD - MLA_ROPE))).reshape(r, self.mla_h * 2 * HEAD)

    def unpad_w_uq(self, g):
        r = g.shape[0]
        return g.reshape(r, self.mla_h, 2 * HEAD)[:, :, :HEAD + MLA_ROPE].reshape(r, self.mla_h * (HEAD + MLA_ROPE))


def _rope_tables(s):
    def tables(dim):
        inv = 1.0 / (ROPE_THETA ** (jnp.arange(0, dim, 2, dtype=F32) / dim))
        ang = jnp.arange(s, dtype=F32)[:, None] * inv[None, :]
        return jnp.cos(ang), jnp.sin(ang)

    def build(cos, sin, lead, trail_one, trail_zero):
        z = jnp.zeros_like(sin)
        ones = lambda n: jnp.ones((s, n), F32)
        zeros = lambda n: jnp.zeros((s, n), F32)
        c = jnp.concatenate([ones(lead), cos, cos, ones(trail_one), zeros(trail_zero)], axis=1)
        s1 = jnp.concatenate([zeros(lead), -sin, z, zeros(trail_one + trail_zero)], axis=1)
        s2 = jnp.concatenate([zeros(lead), z, sin, zeros(trail_one + trail_zero)], axis=1)
        return c, s1, s2

    cm, sm = tables(MLA_ROPE)
    cp, sp = tables(PARTIAL_ROPE)
    return {"mla_q": build(cm, sm, HEAD, 0, HEAD - MLA_ROPE),
            "mla_k": build(cm, sm, 0, 0, LANES - MLA_ROPE),
            "dil": build(cp, sp, 0, HEAD - PARTIAL_ROPE, 0)}


def _ffn_forward(tag, x, g, wg, wu, wd, queue):
    h = _rms_fwd(tag + "_norm", x, g)
    a, b, z = _ffn_up(tag + "_up", h, wg, wu, queue)
    y = _mm(tag + "_down", z, wd, res=x, alpha=0.5, out_dtype=F32)
    return y, (x, h, a, b, z)


def _ffn_backward(tag, dx, dxb, saved, g, wg, wu, wd, queue):
    x, h, a, b, z = saved
    da, db = _ffn_bwd_mid(tag + "_bwd_mid", dxb, wd, a, b, 0.5, queue)
    g_wd = _mm(tag + "_dwd", z, dxb, ta=True, alpha=0.5, out_dtype=F32, queue=queue)
    g_wg = _mm(tag + "_dwg", h, da, ta=True, out_dtype=F32, queue=queue)
    g_wu = _mm(tag + "_dwu", h, db, ta=True, out_dtype=F32, queue=queue)
    dh = _mm(tag + "_dh", da, wg, tb=True, a2=db, b2=wu, out_dtype=F32, queue=queue)
    dx, dxb, dg = _rms_bwd(tag + "_norm_bwd", dh, x, g, res=dx)
    return dx, dxb, dg[0], g_wg, g_wu, g_wd


def _mix_forward(dm, tabs, x, lw, s, queue):
    d, gw = dm.d, dm.gw
    h = _rms_fwd("mix_norm", x, lw["mix_norm"])
    pm = _mm("mix_in_main", h, lw["w_in_main"])
    tail = _mm("mix_in_tail", h, lw["w_in_tail"], out_dtype=F32)
    cum = _gate_fwd("fox_gate", tail, lw["gate_bias"], lw["gate_mask"])
    cum_h = cum[:, FORGET_LANE:FORGET_LANE + dm.fox_h].T
    ccol, crow = cum_h[:, :, None], cum_h[:, None, :]
    fox_ops = (pm, ccol, crow)
    out_a, lse_a = _attn_fwd("fox_fwd", "fox", dm.fox_h, s, fox_ops, HEAD ** -0.5, queue)

    cqn = _rms_fwd("mla_q_norm", pm, lw["mla_q_norm"], width=gw, col=dm.col["cq"])
    ckvn = _rms_fwd("mla_kv_norm", pm, lw["mla_kv_norm"], width=gw, col=dm.col["ckv"])
    qb_raw = _mm("mla_uq", cqn, lw["mla_w_uq"])
    qb = _rope("mla_q_rope", qb_raw, tabs["mla_q"], MLA_ROPE // 2, width=2 * HEAD, nblk=dm.mla_h)
    kv = _mm("mla_ukv", ckvn, lw["mla_w_ukv"])
    kr = _rope("mla_k_rope", tail, tabs["mla_k"], MLA_ROPE // 2, width=LANES)
    mla_ops = (qb, kv, kr)
    out_b, lse_b = _attn_fwd("mla_fwd", "mla", dm.mla_h, s, mla_ops, (HEAD + MLA_ROPE) ** -0.5, queue)

    dqr = _rope("dil_q_rope", pm, tabs["dil"], PARTIAL_ROPE // 2, width=HEAD, col=dm.col["dq"], nblk=dm.dil_h)
    dkr = _rope("dil_k_rope", pm, tabs["dil"], PARTIAL_ROPE // 2, width=HEAD, col=dm.col["dk"], nblk=dm.dil_h)
    dil_ops = (dqr, dkr, pm, dm.col["dv"] // HEAD)
    out_c, lse_c = _attn_fwd("dil_fwd", "dil", dm.dil_h, s, dil_ops, HEAD ** -0.5, queue)

    mixed = jnp.concatenate([out_a, out_b, out_c], axis=1)
    y = _mm("mix_out", mixed, lw["w_out"], res=x, out_dtype=F32)
    saved = (x, h, pm, tail, fox_ops, lse_a, cqn, ckvn, mla_ops, lse_b, dil_ops, lse_c, mixed)
    return y, saved


def _mix_backward(dm, tabs, dx, dxb, saved, lw, s, queue):
    x, h, pm, tail, fox_ops, lse_a, cqn, ckvn, mla_ops, lse_b, dil_ops, lse_c, mixed = saved
    gw = dm.gw
    grads = {}
    grads["w_out"] = _mm("mix_dwout", mixed, dxb, ta=True, out_dtype=F32)
    dmixed = _mm("mix_dmixed", dxb, lw["w_out"], tb=True)
    nha, nhb, nhc = dm.fox_h, dm.mla_h, dm.dil_h

    sa = HEAD ** -0.5
    dfq, dcum_q = _attn_bwd_q("fox_bwd_q", "fox", nha, s, fox_ops, sa, mixed, dmixed, 0, lse_a)
    dfk, dfv, dcum = _attn_bwd_kv("fox_bwd_kv", "fox", nha, s, fox_ops, sa, mixed, dmixed, 0, lse_a)

    sb = (HEAD + MLA_ROPE) ** -0.5
    dqb = _attn_bwd_q("mla_bwd_q", "mla", nhb, s, mla_ops, sb, mixed, dmixed, nha, lse_b)
    dkv, dkr = _attn_bwd_kv("mla_bwd_kv", "mla", nhb, s, mla_ops, sb, mixed, dmixed, nha, lse_b)
    dqb_raw = _rope("mla_q_rope_bwd", dqb, tabs["mla_q"], MLA_ROPE // 2, width=2 * HEAD, nblk=nhb, transpose=True)
    grads["mla_w_uq"] = _mm("mla_dwuq", cqn, dqb_raw, ta=True, out_dtype=F32)
    dcqn = _mm("mla_dcqn", dqb_raw, lw["mla_w_uq"], tb=True)
    grads["mla_w_ukv"] = _mm("mla_dwukv", ckvn, dkv, ta=True, out_dtype=F32)
    dckvn = _mm("mla_dckvn", dkv, lw["mla_w_ukv"], tb=True)
    dcq, dg_q = _rms_bwd("mla_q_norm_bwd", dcqn, pm, lw["mla_q_norm"], width=gw, col=dm.col["cq"])
    dckv, dg_kv = _rms_bwd("mla_kv_norm_bwd", dckvn, pm, lw["mla_kv_norm"], width=gw, col=dm.col["ckv"])
    dkr_raw = _rope("mla_k_rope_bwd", dkr, tabs["mla_k"], MLA_ROPE // 2, width=LANES, transpose=True, out_dtype=F32)

    sc = HEAD ** -0.5
    ddqr = _attn_bwd_q("dil_bwd_q", "dil", nhc, s, dil_ops, sc, mixed, dmixed, nha + nhb, lse_c)
    ddkr, ddv = _attn_bwd_kv("dil_bwd_kv", "dil", nhc, s, dil_ops, sc, mixed, dmixed, nha + nhb, lse_c)
    ddq = _rope("dil_q_rope_bwd", ddqr, tabs["dil"], PARTIAL_ROPE // 2, width=HEAD, nblk=nhc, transpose=True)
    ddk = _rope("dil_k_rope_bwd", ddkr, tabs["dil"], PARTIAL_ROPE // 2, width=HEAD, nblk=nhc, transpose=True)

    dcum_lanes = jnp.pad((dcum[:, 0, :] + dcum_q[:, :, 0]).T, ((0, 0), (FORGET_LANE, LANES - FORGET_LANE - nha)))
    dgate, dbias = _gate_bwd("fox_gate_bwd", dcum_lanes, tail, lw["gate_bias"], lw["gate_mask"])
    dtail = (dgate + dkr_raw).astype(BF16)
    dpm = jnp.concatenate([dfq, dfk, dfv, dcq, dckv, ddq, ddk, ddv], axis=1)

    g_main = _mm("mix_dwin_main", h, dpm, ta=True, out_dtype=F32, queue=queue)
    g_tail = _mm("mix_dwin_tail", h, dtail, ta=True, out_dtype=F32)
    grads["w_in"] = jnp.concatenate([g_main, g_tail], axis=1)
    dh = _mm("mix_dh_main", dpm, lw["w_in_main"], tb=True, out_dtype=F32, queue=queue)
    dh = _mm("mix_dh_tail", dtail, lw["w_in_tail"], tb=True, res=dh, out_dtype=F32)
    dx, dxb, dg = _rms_bwd("mix_norm_bwd", dh, x, lw["mix_norm"], res=dx)
    grads["mix_norm"] = dg[0]
    grads["mla_q_norm"] = dg_q[0]
    grads["mla_kv_norm"] = dg_kv[0]
    grads["fox_forget_bias"] = dbias[0, FORGET_LANE:FORGET_LANE + nha]
    return dx, dxb, grads


def kernel(x, ffn1_norm, ffn1_w_gate, ffn1_w_up, ffn1_w_down, mix_norm, w_in, fox_forget_bias, mla_q_norm, mla_kv_norm, mla_w_uq, mla_w_ukv, w_out, ffn2_norm, ffn2_w_gate, ffn2_w_up, ffn2_w_down, final_norm, loss_target, m_ffn1_norm, m_ffn1_w_gate, m_ffn1_w_up, m_ffn1_w_down, m_mix_norm, m_w_in, m_fox_forget_bias, m_mla_q_norm, m_mla_kv_norm, m_mla_w_uq, m_mla_w_ukv, m_w_out, m_ffn2_norm, m_ffn2_w_gate, m_ffn2_w_up, m_ffn2_w_down, m_final_norm, v_ffn1_norm, v_ffn1_w_gate, v_ffn1_w_up, v_ffn1_w_down, v_mix_norm, v_w_in, v_fox_forget_bias, v_mla_q_norm, v_mla_kv_norm, v_mla_w_uq, v_mla_w_ukv, v_w_out, v_ffn2_norm, v_ffn2_w_gate, v_ffn2_w_up, v_ffn2_w_down, v_final_norm):
    p = dict(locals())
    s, d = x.shape[1], x.shape[2]
    depth = ffn1_norm.shape[0]
    dm = _Dims(d, 4 * w_in.shape[2])
    tabs = _rope_tables(s)
    c_arr = lax.axis_index("c").astype(jnp.int32).reshape(1)
    chip_arr = (2 * lax.axis_index("x") + lax.axis_index("y")).astype(jnp.int32).reshape(1)

    shards = {n: p[n].astype(BF16) for n in BIG}
    stacked = [dict() for _ in range(depth)]

    def gather_task(names, l):
        halves = [shards[n][l].reshape(2, shards[n].shape[1] // 2, shards[n].shape[2]) for n in names]
        outs = yield _gather_comm("ag_" + names[0], halves)
        for n, o in zip(names, outs):
            stacked[l][n] = o.reshape(4, shards[n].shape[1], shards[n].shape[2])

    def launch_gathers(queue, l):
        for names in GATHER_BUNDLES:
            queue.add(gather_task(names, l))

    gate_lanes = ((0, 0), (FORGET_LANE, LANES - FORGET_LANE - dm.fox_h))
    gate_mask = jnp.pad(jnp.ones((1, dm.fox_h), F32), gate_lanes)

    fq = _CommQueue()
    for l in range(depth):
        launch_gathers(fq, l)

    def sublayer_weights(lw, l, names):
        fq.drain(until=lambda: all(n in stacked[l] for n in names))
        for n in names:
            st = stacked[l][n]
            lw[n] = (st.reshape(-1, st.shape[-1]) if n in ROW_SHARDED
                     else jnp.transpose(st, (1, 0, 2)).reshape(st.shape[1], -1))

    def layer_constants(l):
        lw = {n: p[n][l][None, :] for n in ("ffn1_norm", "mix_norm", "mla_q_norm", "mla_kv_norm", "ffn2_norm")}
        lw["gate_bias"] = jnp.pad(fox_forget_bias[l][None, :], gate_lanes)
        lw["gate_mask"] = gate_mask
        return lw

    xs = x[0]
    saved, layers = [], []
    for l in range(depth):
        lw = layer_constants(l)
        layers.append(lw)
        sublayer_weights(lw, l, ("ffn1_w_gate", "ffn1_w_up", "ffn1_w_down"))
        xs, s1 = _ffn_forward("ffn1", xs, lw["ffn1_norm"], lw["ffn1_w_gate"], lw["ffn1_w_up"], lw["ffn1_w_down"], fq)
        sublayer_weights(lw, l, ("w_in", "mla_w_uq", "mla_w_ukv", "w_out"))
        w_in_al = dm.align_w_in(lw["w_in"])
        lw["w_in_main"] = w_in_al[:, :dm.main_w]
        lw["w_in_tail"] = w_in_al[:, dm.main_w:]
        lw["mla_w_uq"] = dm.pad_w_uq(lw["mla_w_uq"])
        xs, s2 = _mix_forward(dm, tabs, xs, lw, s, fq)
        sublayer_weights(lw, l, ("ffn2_w_gate", "ffn2_w_up", "ffn2_w_down"))
        xs, s3 = _ffn_forward("ffn2", xs, lw["ffn2_norm"], lw["ffn2_w_gate"], lw["ffn2_w_up"], lw["ffn2_w_down"], fq)
        saved.append((s1, s2, s3))
    fq.drain()
    dx, dxb, dg_final, loss_blk = _loss_head("loss_head", xs, final_norm[None, :], loss_target[0])

    mines = {n: [None] * depth for n in BIG}
    others = {n: [None] * depth for n in BIG}

    def halves_view(n, g):
        rr, cc = g.shape
        if n in ROW_SHARDED:
            return jnp.transpose(g.reshape(4, 2, rr // 8, cc), (1, 0, 2, 3)).reshape(2, rr // 2, cc)
        return g.reshape(2, rr // 2, cc)

    def reduce_task(names, l, gl):
        views = [halves_view(n, gl[n]) for n in names]
        recvs = yield _sibling_comm("rs_sibling_" + names[0], views)
        for n, view, recv in zip(names, views, recvs):
            pair = _add_half("rs_pair_" + n, view, recv, c_arr)
            rr, cc = pair.shape[1], pair.shape[2]
            if n in ROW_SHARDED:
                pst = jnp.transpose(pair.reshape(1, 4, rr // 4, cc), (1, 0, 2, 3))
            else:
                pst = jnp.transpose(pair.reshape(1, rr, 4, cc // 4), (2, 0, 1, 3))
            (got,) = yield _chips_comm("rs_chips_" + n, [pst])
            mines[n][l] = _sum_chips("rs_sum_" + n, pst, got, chip_arr)

    bq = _CommQueue()
    g_layers = [None] * depth
    for l in reversed(range(depth)):
        lw = layers[l]
        s1, s2, s3 = saved[l]
        gl = {}
        dx, dxb, gl["ffn2_norm"], gl["ffn2_w_gate"], gl["ffn2_w_up"], gl["ffn2_w_down"] = _ffn_backward(
            "ffn2", dx, dxb, s3, lw["ffn2_norm"], lw["ffn2_w_gate"], lw["ffn2_w_up"], lw["ffn2_w_down"], bq)
        bq.add(reduce_task(REDUCE_BUNDLES[0], l, gl))
        dx, dxb, gm = _mix_backward(dm, tabs, dx, dxb, s2, lw, s, bq)
        gl.update(gm)
        gl["w_in"] = dm.unalign_w_in(gl["w_in"])
        gl["mla_w_uq"] = dm.unpad_w_uq(gl["mla_w_uq"])
        bq.add(reduce_task(REDUCE_BUNDLES[2], l, gl))
        dx, dxb, gl["ffn1_norm"], gl["ffn1_w_gate"], gl["ffn1_w_up"], gl["ffn1_w_down"] = _ffn_backward(
            "ffn1", dx, dxb, s1, lw["ffn1_norm"], lw["ffn1_w_gate"], lw["ffn1_w_up"], lw["ffn1_w_down"], bq)
        bq.add(reduce_task(REDUCE_BUNDLES[1], l, gl))
        g_layers[l] = gl
    bq.drain()
    swapped = _run_comm(_swap_comm("rs_swap", [mines[n][l] for n in BIG for l in range(depth)]))
    for k, n in enumerate(BIG):
        others[n] = list(swapped[k * depth:(k + 1) * depth])
    grad_x = dx[None]

    grads, deltas, new_m, new_v = {}, {}, {}, {}
    for n in BIG:
        grads[n], deltas[n], new_m[n], new_v[n] = _adamw_layers(
            "adamw_" + n, p[n], mines[n], others[n], p["m_" + n], p["v_" + n], c_arr)

    def pack(get):
        flat = lambda n: jnp.pad(get(n).reshape(1, -1), ((0, 0), (0, d - get(n).size)))
        rows = [get("ffn1_norm"), get("mix_norm"), get("ffn2_norm"), get("final_norm")[None, :],
                flat("mla_q_norm"), flat("mla_kv_norm"), flat("fox_forget_bias")]
        return jnp.concatenate(rows, axis=0)

    def unpack(blk):
        o = 3 * depth
        return {"ffn1_norm": blk[0:depth], "mix_norm": blk[depth:2 * depth], "ffn2_norm": blk[2 * depth:o],
                "final_norm": blk[o], "mla_q_norm": blk[o + 1, :depth * dm.rank].reshape(depth, dm.rank),
                "mla_kv_norm": blk[o + 2, :depth * dm.rank].reshape(depth, dm.rank),
                "fox_forget_bias": blk[o + 3, :depth * dm.fox_h].reshape(depth, dm.fox_h)}

    assert depth * dm.rank <= d
    local_small = {n: (dg_final[0] if n == "final_norm" else jnp.stack([g_layers[l][n] for l in range(depth)]))
                   for n in SMALL}
    n_rows = 3 * depth + 4
    pad_rows = -(n_rows + 1) % 8
    blk = jnp.concatenate([pack(lambda n: local_small[n]), jnp.broadcast_to(loss_blk[0:1, 0:1], (1, d)),
                           jnp.zeros((pad_rows, d), F32)], axis=0)
    everyone = _gather_all("small_gather", blk)[:, None]
    total = _sum_slots("small_sum", everyone, tuple(range(8)))[0]
    loss = total[n_rows, 0]
    small_g = unpack(total)
    w_blk = jnp.concatenate([pack(lambda n: p[n]), jnp.zeros((pad_rows + 1, d), F32)], axis=0)
    m_blk = jnp.concatenate([pack(lambda n: p["m_" + n]), jnp.zeros((pad_rows + 1, d), F32)], axis=0)
    v_blk = jnp.concatenate([pack(lambda n: p["v_" + n]), jnp.zeros((pad_rows + 1, d), F32)], axis=0)
    g_blk = jnp.concatenate([total[:n_rows], jnp.zeros((pad_rows + 1, d), F32)], axis=0)
    d_blk, nm_blk, nv_blk = _adamw("adamw_small", w_blk[None], g_blk[None], m_blk[None], v_blk[None])
    small_d, small_m, small_v = unpack(d_blk[0]), unpack(nm_blk[0]), unpack(nv_blk[0])
    for n in SMALL:
        grads[n], deltas[n], new_m[n], new_v[n] = small_g[n], small_d[n], small_m[n], small_v[n]

    return (loss, grad_x, *[grads[n] for n in WEIGHTS], *[deltas[n] for n in WEIGHTS],
            *[new_m[n] for n in WEIGHTS], *[new_v[n] for n in WEIGHTS])
```
